```python
import math
import jax, jax.numpy as jnp
from jax import lax
import numpy as np

D_MODEL = 1024
BATCH = 2
SEQ = 8192
DEPTH = 2
DEC_BATCH = 32
DEC_SEQ = 4
PAST_LEN = 8192
PAGE_SIZE = 128

HEAD_DIM = 64
NSA_HEADS = 8
NSA_KV_HEADS = 2
NSA_GROUP = NSA_HEADS // NSA_KV_HEADS
SB_HEADS = 4
MOBA_HEADS = 4
CMP_LEN = 32
CMP_STRIDE = 16
CMP_HID = 2 * HEAD_DIM
SEL_BLOCK = 64
SEL_TOPN = 16
WINDOW = 512
MOBA_BLOCK = 256
MOBA_TOPK = 3
D_FF = 2816
PLE_DIM = 256
ROPE_THETA = 10000.0
Q_BLOCK = 128
RMS_EPS = 1e-6
NEG = -1e30
FORCE_BONUS = 1e4
N_NORMS = 7

NSA_Q_COLS = NSA_HEADS * HEAD_DIM
NSA_KV_COLS = 6 * NSA_KV_HEADS * HEAD_DIM
NSA_G_COLS = 3 * NSA_HEADS
SB_COLS = 3 * SB_HEADS * HEAD_DIM
MOBA_COLS = 3 * MOBA_HEADS * HEAD_DIM
IN_COLS = NSA_Q_COLS + NSA_KV_COLS + NSA_G_COLS + SB_COLS + MOBA_COLS

kernel_name = 'nsa_stickbreak_moba_hybrid_step'


def rmsnorm(x, g):
    xf = x.astype(jnp.float32)
    y = xf * lax.rsqrt(jnp.mean(xf * xf, axis=-1, keepdims=True) + RMS_EPS)
    return (y * g.astype(jnp.float32)).astype(x.dtype)


def swiglu(x, wg, wu, wd):
    return (jax.nn.silu(x @ wg) * (x @ wu)) @ wd


def rope(x, pos):
    half = x.shape[-1] // 2
    inv = ROPE_THETA ** (-jnp.arange(half, dtype=jnp.float32) / half)
    ang = pos.astype(jnp.float32)[:, None] * inv[None, :]
    cos = jnp.cos(ang)[None, :, None, :]
    sin = jnp.sin(ang)[None, :, None, :]
    xf = x.astype(jnp.float32)
    x1, x2 = xf[..., :half], xf[..., half:]
    return jnp.concatenate([x1 * cos - x2 * sin, x2 * cos + x1 * sin], axis=-1).astype(x.dtype)


def masked_softmax(s, mask):
    p = jax.nn.softmax(jnp.where(mask, s, NEG), axis=-1)
    return jnp.where(mask, p, 0.0)


def sweep_queries(fn, q_pos, *qs):
    tq = q_pos.shape[0]
    if tq <= Q_BLOCK or tq % Q_BLOCK:
        return fn(q_pos, *qs)
    nb = tq // Q_BLOCK
    blk = lambda a: jnp.moveaxis(a.reshape(a.shape[0], nb, Q_BLOCK, *a.shape[2:]), 1, 0)
    out = lax.map(lambda args: fn(*args), (q_pos.reshape(nb, Q_BLOCK),) + tuple(blk(a) for a in qs))
    out = jnp.moveaxis(out, 0, 1)
    return out.reshape(out.shape[0], tq, *out.shape[3:])


def to_blocks(x, blk):
    b, t, h, d = x.shape
    nb = -(-t // blk)
    x = jnp.pad(x, ((0, 0), (0, nb * blk - t), (0, 0), (0, 0)))
    return x.reshape(b, nb, blk, h, d).transpose(0, 3, 1, 2, 4)


def gather_blocks(blocks, idx):
    return jax.vmap(jax.vmap(lambda a, i: a[i]))(blocks, idx)


def nsa_compress(x, pe, w1, w2):
    b, t, h, d = x.shape
    r = CMP_LEN // CMP_STRIDE
    n_chunk = t // CMP_STRIDE
    n_cmp = n_chunk - r + 1
    c = x[:, :n_chunk * CMP_STRIDE].reshape(b, n_chunk, CMP_STRIDE, h, d)
    blk = jnp.concatenate([c[:, i:i + n_cmp] for i in range(r)], axis=2)
    blk = blk + pe[None, None, :, None, :]
    flat = blk.transpose(0, 1, 3, 2, 4).reshape(b, n_cmp, h, CMP_LEN * d)
    return jax.nn.gelu(flat @ w1) @ w2


def nsa_attend(q, gates, q_pos, kv_nsa, win_rows, win_start, cmp_pe, cmp_w1, cmp_w2):
    t_kv = kv_nsa.shape[1]
    scale = HEAD_DIM ** -0.5
    k_cmp = nsa_compress(kv_nsa[:, :, 0], cmp_pe[0], cmp_w1[0], cmp_w2[0])
    v_cmp = nsa_compress(kv_nsa[:, :, 1], cmp_pe[1], cmp_w1[1], cmp_w2[1])
    n_cmp = k_cmp.shape[1]
    cmp_start = jnp.arange(n_cmp) * CMP_STRIDE
    cmp_end = cmp_start + CMP_LEN - 1
    n_sel = -(-t_kv // SEL_BLOCK)
    sel_start = jnp.arange(n_sel) * SEL_BLOCK
    cover = ((cmp_start[:, None] < sel_start[None, :] + SEL_BLOCK)
             & (cmp_end[:, None] >= sel_start[None, :])).astype(jnp.float32)
    k_selb = to_blocks(kv_nsa[:, :, 2], SEL_BLOCK)
    v_selb = to_blocks(kv_nsa[:, :, 3], SEL_BLOCK)
    n_top = min(SEL_TOPN, n_sel)
    pad = ((0, 0), (WINDOW, 0), (0, 0), (0, 0))
    k_win = jnp.pad(win_rows[:, :, 0], pad)
    v_win = jnp.pad(win_rows[:, :, 1], pad)
    win_pos = win_start - WINDOW + jnp.arange(k_win.shape[1])

    def block(qp, qb, gb):
        s = jnp.einsum('bqhgd,bnhd->bqhgn', qb, k_cmp, preferred_element_type=jnp.float32) * scale
        p_cmp = masked_softmax(s, (cmp_end[None, :] <= qp[:, None])[None, :, None, None, :])
        o_cmp = jnp.einsum('bqhgn,bnhd->bqhgd', p_cmp.astype(v_cmp.dtype), v_cmp)
        imp = jnp.einsum('bqhgn,ns->bqhs', p_cmp, cover)
        cur = qp[:, None] // SEL_BLOCK
        j = jnp.arange(n_sel)[None, :]
        forced = ((j == 0) | (j == cur) | (j == cur - 1)).astype(jnp.float32)
        valid = sel_start[None, :] <= qp[:, None]
        imp = jnp.where(valid[None, :, None, :], imp + FORCE_BONUS * forced[None, :, None, :], NEG)
        _, idx = lax.top_k(imp, n_top)
        idx = idx.transpose(0, 2, 1, 3)
        ks = gather_blocks(k_selb, idx)
        vs = gather_blocks(v_selb, idx)
        s = jnp.einsum('bqhgd,bhqnld->bqhgnl', qb, ks, preferred_element_type=jnp.float32) * scale
        kpos = idx[..., None] * SEL_BLOCK + jnp.arange(SEL_BLOCK)
        m = (kpos <= qp[None, None, :, None, None]).transpose(0, 2, 1, 3, 4)[:, :, :, None]
        shp = s.shape
        p = masked_softmax(s.reshape(*shp[:4], -1), m.reshape(*m.shape[:4], -1)).reshape(shp)
        o_sel = jnp.einsum('bqhgnl,bhqnld->bqhgd', p.astype(vs.dtype), vs)
        start = qp[0] - win_pos[0] - WINDOW
        length = WINDOW + qp.shape[0]
        kw = lax.dynamic_slice_in_dim(k_win, start, length, axis=1)
        vw = lax.dynamic_slice_in_dim(v_win, start, length, axis=1)
        pw = lax.dynamic_slice_in_dim(win_pos, start, length)
        s = jnp.einsum('bqhgd,bkhd->bqhgk', qb, kw, preferred_element_type=jnp.float32) * scale
        dist = qp[:, None] - pw[None, :]
        mw = (dist >= 0) & (dist <= WINDOW) & (pw[None, :] >= win_start)
        p = masked_softmax(s, mw[None, :, None, None, :])
        o_win = jnp.einsum('bqhgk,bkhd->bqhgd', p.astype(vw.dtype), vw)
        return gb[..., 0:1] * o_cmp + gb[..., 1:2] * o_sel + gb[..., 2:3] * o_win

    return sweep_queries(block, q_pos, q, gates)


def sb_attend(q, q_pos, k, v):
    kpos = jnp.arange(k.shape[1])
    scale = HEAD_DIM ** -0.5

    def block(qp, qb):
        z = jnp.einsum('bqhd,bkhd->bhqk', qb, k, preferred_element_type=jnp.float32) * scale
        causal = (kpos[None, :] < qp[:, None])[None, None]
        log_keep = jnp.where(causal, jax.nn.log_sigmoid(-z), 0.0)
        after = lax.cumsum(log_keep, axis=3, reverse=True) - log_keep
        a = jnp.where(causal, jnp.exp(jax.nn.log_sigmoid(z) + after), 0.0)
        return jnp.einsum('bhqk,bkhd->bqhd', a.astype(v.dtype), v)

    return sweep_queries(block, q_pos, q)


def moba_attend(q, q_pos, k, v):
    kb = to_blocks(k, MOBA_BLOCK)
    vb = to_blocks(v, MOBA_BLOCK)
    n_blk = kb.shape[2]
    k_mean = jnp.mean(kb.astype(jnp.float32), axis=3)
    n_top = min(MOBA_TOPK, n_blk)
    scale = HEAD_DIM ** -0.5

    def block(qp, qb):
        own = qp // MOBA_BLOCK
        past = jnp.arange(n_blk)[None, :] < own[:, None]
        s = jnp.einsum('bqhd,bhjd->bqhj', qb.astype(jnp.float32), k_mean)
        s = jnp.where(past[None, :, None, :], s, NEG)
        _, top_i = lax.top_k(s, n_top)
        own_b = jnp.broadcast_to(own[None, :, None, None], top_i.shape[:3] + (1,))
        idx = jnp.concatenate([top_i, own_b], axis=-1)
        ok = jnp.concatenate([top_i < own_b, jnp.ones_like(own_b, dtype=bool)], axis=-1)
        idx_t = idx.transpose(0, 2, 1, 3)
        ks = gather_blocks(kb, idx_t)
        vs = gather_blocks(vb, idx_t)
        s = jnp.einsum('bqhd,bhqnld->bqhnl', qb, ks, preferred_element_type=jnp.float32) * scale
        kpos = idx[..., None] * MOBA_BLOCK + jnp.arange(MOBA_BLOCK)
        m = ok[..., None] & (kpos <= qp[None, :, None, None, None])
        shp = s.shape
        p = masked_softmax(s.reshape(*shp[:3], -1), m.reshape(*shp[:3], -1)).reshape(shp)
        return jnp.einsum('bqhnl,bhqnld->bqhd', p.astype(vs.dtype), vs)

    return sweep_queries(block, q_pos, q)


def token_mixing(u, past, w_in, cmp_pe, cmp_w1, cmp_w2, w_br_nsa, w_br_sb, w_br_mb, w_gate, w_out):
    b, t, _ = u.shape
    hd = HEAD_DIM
    past_len = 0 if past is None else past[0].shape[1]
    q_pos = past_len + jnp.arange(t, dtype=jnp.int32)
    o1 = NSA_Q_COLS
    o2 = o1 + NSA_KV_COLS
    o3 = o2 + NSA_G_COLS
    o4 = o3 + SB_COLS
    q_nsa, kv_nsa, g_nsa, qkv_sb, qkv_mb = jnp.split(u @ w_in, [o1, o2, o3, o4], axis=-1)
    q_nsa = rope(q_nsa.reshape(b, t, NSA_HEADS, hd), q_pos).reshape(b, t, NSA_KV_HEADS, NSA_GROUP, hd)
    kv_nsa = kv_nsa.reshape(b, t, 6, NSA_KV_HEADS, hd)
    k_rot = rope(kv_nsa[:, :, 0::2].reshape(b, t, 3 * NSA_KV_HEADS, hd), q_pos).reshape(b, t, 3, NSA_KV_HEADS, hd)
    new_nsa = jnp.stack([k_rot[:, :, 0], kv_nsa[:, :, 1], k_rot[:, :, 1], kv_nsa[:, :, 3]], axis=2)
    new_win = jnp.stack([k_rot[:, :, 2], kv_nsa[:, :, 5]], axis=2)
    gates_nsa = jax.nn.sigmoid(g_nsa).reshape(b, t, NSA_KV_HEADS, NSA_GROUP, 3)
    qkv_sb = qkv_sb.reshape(b, t, 3, SB_HEADS, hd)
    q_sb = qkv_sb[:, :, 0]
    new_sb = qkv_sb[:, :, 1:]
    qkv_mb = qkv_mb.reshape(b, t, 3, MOBA_HEADS, hd)
    qk_mb = rope(qkv_mb[:, :, :2].reshape(b, t, 2 * MOBA_HEADS, hd), q_pos).reshape(b, t, 2, MOBA_HEADS, hd)
    q_mb = qk_mb[:, :, 0]
    new_mb = jnp.stack([qk_mb[:, :, 1], qkv_mb[:, :, 2]], axis=2)

    if past is None:
        full_nsa, full_sb, full_mb, win_rows = new_nsa, new_sb, new_mb, new_win
        keep = min(WINDOW, t)
    else:
        past_nsa, past_sb, past_mb, past_win = past
        full_nsa = jnp.concatenate([past_nsa, new_nsa], axis=1)
        full_sb = jnp.concatenate([past_sb, new_sb], axis=1)
        full_mb = jnp.concatenate([past_mb, new_mb], axis=1)
        win_rows = jnp.concatenate([past_win, new_win], axis=1)
        keep = past_win.shape[1]
    win_start = past_len - (win_rows.shape[1] - t)

    o_nsa = nsa_attend(q_nsa, gates_nsa, q_pos, full_nsa, win_rows, win_start, cmp_pe, cmp_w1, cmp_w2)
    o_sb = sb_attend(q_sb, q_pos, full_sb[:, :, 0], full_sb[:, :, 1])
    o_mb = moba_attend(q_mb, q_pos, full_mb[:, :, 0], full_mb[:, :, 1])
    branches = jnp.stack([o_nsa.reshape(b, t, -1) @ w_br_nsa,
                          o_sb.reshape(b, t, -1) @ w_br_sb,
                          o_mb.reshape(b, t, -1) @ w_br_mb], axis=2)
    gate = jax.nn.sigmoid(u @ w_gate).reshape(b, t, 3, -1)
    y = jnp.sum(gate * branches, axis=2) @ w_out
    win_state = win_rows[:, win_rows.shape[1] - keep:]
    return y, new_nsa, new_sb, new_mb, win_state


def decoder_layer(h, p_l, past, lw):
    (norm_g, ffn_wg, ffn_wu, ffn_wd, w_in, cmp_pe, cmp_w1, cmp_w2,
     w_br_nsa, w_br_sb, w_br_mb, w_gate, w_out, w_ple_proj, w_ple_gate) = lw
    h = h + 0.5 * rmsnorm(swiglu(rmsnorm(h, norm_g[0]), ffn_wg[0], ffn_wu[0], ffn_wd[0]), norm_g[1])
    y, new_nsa, new_sb, new_mb, win_state = token_mixing(
        rmsnorm(h, norm_g[2]), past, w_in, cmp_pe, cmp_w1, cmp_w2, w_br_nsa, w_br_sb, w_br_mb, w_gate, w_out)
    h = h + rmsnorm(y, norm_g[3])
    h = h + 0.5 * rmsnorm(swiglu(rmsnorm(h, norm_g[4]), ffn_wg[1], ffn_wu[1], ffn_wd[1]), norm_g[5])
    ple = jax.nn.sigmoid(h @ w_ple_gate) * (p_l @ w_ple_proj)
    h = h + rmsnorm(ple, norm_g[6])
    return h, (new_nsa, new_sb, new_mb, win_state)


def setup_inputs(seed: int = 0) -> dict:
    key = jax.random.key(seed)
    ks = jax.random.split(key, 26)
    f32 = jnp.float32
    n_pages = PAST_LEN // PAGE_SIZE
    n_used = DEC_BATCH * n_pages
    n_pool = n_used + max(1, n_used // 4)
    win_buf = min(WINDOW, PAST_LEN)

    def nrm(k, shape):
        return jax.random.normal(k, shape, f32)

    def w(k, shape, fan_in):
        return nrm(k, shape) * fan_in ** -0.5

    page_table = jax.random.permutation(ks[8], n_pool)[:n_used].reshape(DEC_BATCH, n_pages).astype(jnp.int32)
    return {
        'x_prompt': nrm(ks[0], (BATCH, SEQ, D_MODEL)),
        'x_sample': nrm(ks[1], (DEC_BATCH, DEC_SEQ, D_MODEL)),
        'p_prompt': nrm(ks[2], (DEPTH, BATCH, SEQ, PLE_DIM)),
        'p_sample': nrm(ks[3], (DEPTH, DEC_BATCH, DEC_SEQ, PLE_DIM)),
        'cache_nsa': nrm(ks[4], (DEPTH, n_pool, PAGE_SIZE, 4, NSA_KV_HEADS, HEAD_DIM)),
        'cache_sb': nrm(ks[5], (DEPTH, n_pool, PAGE_SIZE, 2, SB_HEADS, HEAD_DIM)),
        'cache_moba': nrm(ks[6], (DEPTH, n_pool, PAGE_SIZE, 2, MOBA_HEADS, HEAD_DIM)),
        'cache_win': nrm(ks[7], (DEPTH, DEC_BATCH, win_buf, 2, NSA_KV_HEADS, HEAD_DIM)),
        'page_table': page_table,
        'norm_g': 1.0 + 0.05 * nrm(ks[9], (DEPTH, N_NORMS, D_MODEL)),
        'ffn_w_gate': w(ks[10], (DEPTH, 2, D_MODEL, D_FF), D_MODEL),
        'ffn_w_up': w(ks[11], (DEPTH, 2, D_MODEL, D_FF), D_MODEL),
        'ffn_w_down': w(ks[12], (DEPTH, 2, D_FF, D_MODEL), D_FF),
        'w_in': w(ks[13], (DEPTH, D_MODEL, IN_COLS), D_MODEL),
        'nsa_cmp_pe': 0.1 * nrm(ks[14], (DEPTH, 2, CMP_LEN, HEAD_DIM)),
        'nsa_cmp_w1': w(ks[15], (DEPTH, 2, CMP_LEN * HEAD_DIM, CMP_HID), CMP_LEN * HEAD_DIM),
        'nsa_cmp_w2': w(ks[16], (DEPTH, 2, CMP_HID, HEAD_DIM), CMP_HID),
        'w_branch_nsa': w(ks[17], (DEPTH, NSA_HEADS * HEAD_DIM, D_MODEL), NSA_HEADS * HEAD_DIM),
        'w_branch_sb': w(ks[18], (DEPTH, SB_HEADS * HEAD_DIM, D_MODEL), SB_HEADS * HEAD_DIM),
        'w_branch_moba': w(ks[19], (DEPTH, MOBA_HEADS * HEAD_DIM, D_MODEL), MOBA_HEADS * HEAD_DIM),
        'w_merge_gate': w(ks[20], (DEPTH, D_MODEL, 3 * D_MODEL), D_MODEL),
        'w_out': w(ks[21], (DEPTH, D_MODEL, D_MODEL), D_MODEL),
        'w_ple_proj': w(ks[22], (DEPTH, PLE_DIM, D_MODEL), PLE_DIM),
        'w_ple_gate': w(ks[23], (DEPTH, D_MODEL, D_MODEL), D_MODEL),
    }


def reference(x_prompt, x_sample, p_prompt, p_sample, cache_nsa, cache_sb, cache_moba, cache_win, page_table,
              norm_g, ffn_w_gate, ffn_w_up, ffn_w_down, w_in, nsa_cmp_pe, nsa_cmp_w1, nsa_cmp_w2,
              w_branch_nsa, w_branch_sb, w_branch_moba, w_merge_gate, w_out, w_ple_proj, w_ple_gate):
    dec_b, n_pages = page_table.shape
    past_len = n_pages * cache_nsa.shape[2]

    def paged_rows(cache, l):
        rows = cache[l, page_table]
        return rows.reshape(dec_b, past_len, *rows.shape[3:])

    h_p, h_s = x_prompt, x_sample
    st_p, st_s = [], []
    for l in range(DEPTH):
        lw = (norm_g[l], ffn_w_gate[l], ffn_w_up[l], ffn_w_down[l], w_in[l], nsa_cmp_pe[l], nsa_cmp_w1[l],
              nsa_cmp_w2[l], w_branch_nsa[l], w_branch_sb[l], w_branch_moba[l], w_merge_gate[l], w_out[l],
              w_ple_proj[l], w_ple_gate[l])
        h_p, s_p = decoder_layer(h_p, p_prompt[l], None, lw)
        past = (paged_rows(cache_nsa, l), paged_rows(cache_sb, l), paged_rows(cache_moba, l), cache_win[l])
        h_s, s_s = decoder_layer(h_s, p_sample[l], past, lw)
        st_p.append(s_p)
        st_s.append(s_s)

    def stack(states, i):
        return jnp.stack([s[i] for s in states], axis=0)

    return (h_p, h_s, stack(st_p, 0), stack(st_s, 0), stack(st_p, 1), stack(st_s, 1),
            stack(st_p, 2), stack(st_s, 2), stack(st_p, 3), stack(st_s, 3))
```

```python
import functools
import math

import jax
import jax.numpy as jnp
from jax import lax
from jax.experimental import pallas as pl
from jax.experimental.pallas import tpu as pltpu

HEAD_DIM = 64
NSA_HEADS = 8
NSA_KV_HEADS = 2
NSA_GROUP = NSA_HEADS // NSA_KV_HEADS
SB_HEADS = 4
MOBA_HEADS = 4
CMP_LEN = 32
CMP_STRIDE = 16
SEL_BLOCK = 64
SEL_TOPN = 16
WINDOW = 512
MOBA_BLOCK = 256
MOBA_TOPK = 3
ROPE_THETA = 10000.0
RMS_EPS = 1e-6
NEG = -1e30
FORCE_BONUS = 1e4

LANES = 128
SUBLANES = 8
MXU_DTYPE = jnp.bfloat16
VMEM_LIMIT = 56 * 1024 * 1024
SAMPLE_ROWS = 8
PAGES_PER_STEP = 8

NSA_Q_COLS = NSA_HEADS * HEAD_DIM
NSA_KV_COLS = 6 * NSA_KV_HEADS * HEAD_DIM
NSA_G_COLS = 3 * NSA_HEADS
SB_COLS = 3 * SB_HEADS * HEAD_DIM
MOBA_COLS = 3 * MOBA_HEADS * HEAD_DIM

_F32 = jnp.float32


def _dot(a, b):
    return jnp.dot(a, b, preferred_element_type=_F32)


def _dot_nt(a, b):
    return lax.dot_general(a, b, (((1,), (1,)), ((), ())), preferred_element_type=_F32)


def _dot_split(x, w):
    hi = x.astype(MXU_DTYPE)
    mid = (x - hi.astype(_F32)).astype(MXU_DTYPE)
    return _dot(hi, w) + _dot(mid, w)


def _rms(x, g):
    return x * lax.rsqrt(jnp.mean(x * x, axis=-1, keepdims=True) + RMS_EPS) * g


def _params(*sem):
    return pltpu.CompilerParams(dimension_semantics=sem, vmem_limit_bytes=VMEM_LIMIT)


def _full(shape):
    n = len(shape)
    return pl.BlockSpec(shape, lambda *_: (0,) * n)


def _ffn_kernel(*refs, n_ff, with_ple):
    if with_ple:
        (h_ref, gpre_ref, gpost_ref, wg_ref, wu_ref, wd_ref,
         p_ref, wpg_ref, wpp_ref, gple_ref, o_ref, xn_ref, acc_ref) = refs
    else:
        h_ref, gpre_ref, gpost_ref, wg_ref, wu_ref, wd_ref, o_ref, xn_ref, acc_ref = refs
    j = pl.program_id(1)

    @pl.when(j == 0)
    def _():
        xn_ref[...] = _rms(h_ref[...], gpre_ref[...]).astype(MXU_DTYPE)
        acc_ref[...] = jnp.zeros_like(acc_ref)

    xn = xn_ref[...]
    g = _dot(xn, wg_ref[...])
    u = _dot(xn, wu_ref[...])
    a = (g * jax.nn.sigmoid(g)) * u
    acc_ref[...] += _dot(a.astype(MXU_DTYPE), wd_ref[...])

    @pl.when(j == n_ff - 1)
    def _():
        h = h_ref[...] + 0.5 * _rms(acc_ref[...], gpost_ref[...])
        if with_ple:
            gate = jax.nn.sigmoid(_dot(h.astype(MXU_DTYPE), wpg_ref[...]))
            ple = gate * _dot(p_ref[...].astype(MXU_DTYPE), wpp_ref[...])
            h = h + _rms(ple, gple_ref[...])
        o_ref[...] = h


def _ffn(h, g_pre, g_post, wg, wu, wd, ple=None, *, tm, tf):
    n, d = h.shape
    f = wg.shape[1]
    n_ff = f // tf
    row = lambda i, j: (i, 0)
    in_specs = [pl.BlockSpec((tm, d), row), _full((1, d)), _full((1, d)),
                pl.BlockSpec((d, tf), lambda i, j: (0, j)),
                pl.BlockSpec((d, tf), lambda i, j: (0, j)),
                pl.BlockSpec((tf, d), lambda i, j: (j, 0))]
    args = [h, g_pre, g_post, wg, wu, wd]
    if ple is not None:
        p, wpg, wpp, g_ple = ple
        in_specs += [pl.BlockSpec((tm, p.shape[1]), row), _full(wpg.shape), _full(wpp.shape), _full((1, d))]
        args += [p, wpg, wpp, g_ple]
    return pl.pallas_call(
        functools.partial(_ffn_kernel, n_ff=n_ff, with_ple=ple is not None),
        grid=(n // tm, n_ff),
        in_specs=in_specs,
        out_specs=pl.BlockSpec((tm, d), row),
        out_shape=jax.ShapeDtypeStruct((n, d), _F32),
        scratch_shapes=[pltpu.VMEM((tm, d), MXU_DTYPE), pltpu.VMEM((tm, d), _F32)],
        compiler_params=_params("parallel", "arbitrary"),
        name="ffn_ple" if ple is not None else "ffn",
    )(*args)


_G_QNSA = 0
_G_KV = 4
_G_SB = 10
_G_MB = 16
_G_DUP = 22
_G_GATE = 30
_N_GROUPS = 32
_ROPE_GROUPS = (0, 1, 2, 3, 4, 6, 8, 16, 17, 18, 19, 22, 23, 26, 27)


def _inproj_kernel(h_ref, g_ref, w_ref, cos_ref, sa_ref, sb_ref,
                   qnsa_ref, nsa_ref, win_ref, qsb_ref, sbr_ref, sbb_ref, qmb_ref, mbr_ref, mbb_ref,
                   dup_ref, gate_ref):
    xn = _rms(h_ref[...], g_ref[...]).astype(MXU_DTYPE)
    cos, sa, sb = cos_ref[...], sa_ref[...], sb_ref[...]
    scale = HEAD_DIM ** -0.5

    def group(gi):
        y = _dot(xn, w_ref[:, gi * LANES:(gi + 1) * LANES])
        if gi in _ROPE_GROUPS:
            y = y * cos + pltpu.roll(y, LANES - HEAD_DIM // 2, 1) * sa + pltpu.roll(y, HEAD_DIM // 2, 1) * sb
        return y

    for i in range(4):
        qnsa_ref[:, i * LANES:(i + 1) * LANES] = (group(_G_QNSA + i) * scale).astype(qnsa_ref.dtype)
    for i in range(4):
        nsa_ref[:, i * LANES:(i + 1) * LANES] = group(_G_KV + i)
    for i in range(2):
        win_ref[:, i * LANES:(i + 1) * LANES] = group(_G_KV + 4 + i)
    for i in range(2):
        qsb_ref[:, i * LANES:(i + 1) * LANES] = (group(_G_SB + i) * scale).astype(qsb_ref.dtype)
    for i in range(4):
        y = group(_G_SB + 2 + i)
        sbr_ref[:, i * LANES:(i + 1) * LANES] = y
        sbb_ref[:, i * LANES:(i + 1) * LANES] = y.astype(sbb_ref.dtype)
    for i in range(2):
        qmb_ref[:, i * LANES:(i + 1) * LANES] = group(_G_MB + i)
    for i in range(4):
        y = group(_G_MB + 2 + i)
        mbr_ref[:, i * LANES:(i + 1) * LANES] = y
        mbb_ref[:, i * LANES:(i + 1) * LANES] = y.astype(mbb_ref.dtype)
    for i in range(8):
        dup_ref[:, i * LANES:(i + 1) * LANES] = group(_G_DUP + i).astype(dup_ref.dtype)
    for i in range(2):
        gate_ref[:, i * LANES:(i + 1) * LANES] = jax.nn.sigmoid(group(_G_GATE + i))


def _inproj(h, g, w_all, tabs, tab_map, *, tm):
    n, d = h.shape
    row = lambda i: (i, 0)
    tab_spec = pl.BlockSpec((tm, LANES), tab_map)
    widths = [(512, MXU_DTYPE), (512, _F32), (256, _F32), (256, MXU_DTYPE), (512, _F32), (512, MXU_DTYPE),
              (256, _F32), (512, _F32), (512, MXU_DTYPE), (1024, MXU_DTYPE), (256, _F32)]
    return pl.pallas_call(
        _inproj_kernel,
        grid=(n // tm,),
        in_specs=[pl.BlockSpec((tm, d), row), _full((1, d)), _full(w_all.shape), tab_spec, tab_spec, tab_spec],
        out_specs=[pl.BlockSpec((tm, c), row) for c, _ in widths],
        out_shape=[jax.ShapeDtypeStruct((n, c), dt) for c, dt in widths],
        compiler_params=_params("parallel"),
        name="inproj",
    )(h, g, w_all, *tabs)


def _merge_kernel(h_ref, gpre_ref, gpost_ref, onsa_ref, osb_ref, omb_ref,
                  wbn_ref, wbs_ref, wbm_ref, wgate_ref, wout_ref, o_ref):
    d = h_ref.shape[1]
    h = h_ref[...]
    u = _rms(h, gpre_ref[...]).astype(MXU_DTYPE)
    branches = (_dot(onsa_ref[...], wbn_ref[...]), _dot(osb_ref[...], wbs_ref[...]), _dot(omb_ref[...], wbm_ref[...]))
    mix = None
    for c, br in enumerate(branches):
        term = jax.nn.sigmoid(_dot(u, wgate_ref[:, c * d:(c + 1) * d])) * br
        mix = term if mix is None else mix + term
    y = _dot(mix.astype(MXU_DTYPE), wout_ref[...])
    o_ref[...] = h + _rms(y, gpost_ref[...])


def _merge(h, g_pre, g_post, o_nsa, o_sb, o_mb, wbn, wbs, wbm, wgate, wout, *, tm):
    n, d = h.shape
    row = lambda i: (i, 0)
    return pl.pallas_call(
        _merge_kernel,
        grid=(n // tm,),
        in_specs=[pl.BlockSpec((tm, d), row), _full((1, d)), _full((1, d)),
                  pl.BlockSpec((tm, o_nsa.shape[1]), row), pl.BlockSpec((tm, o_sb.shape[1]), row),
                  pl.BlockSpec((tm, o_mb.shape[1]), row),
                  _full(wbn.shape), _full(wbs.shape), _full(wbm.shape), _full(wgate.shape), _full(wout.shape)],
        out_specs=pl.BlockSpec((tm, d), row),
        out_shape=jax.ShapeDtypeStruct((n, d), _F32),
        compiler_params=_params("parallel"),
        name="merge",
    )(h, g_pre, g_post, o_nsa, o_sb, o_mb, wbn, wbs, wbm, wgate, wout)


def _lane_lo(shape=(1, LANES)):
    return (lax.broadcasted_iota(jnp.int32, shape, len(shape) - 1) % LANES) < HEAD_DIM


def _softplus(z):
    return jnp.maximum(z, 0.0) + jnp.log1p(jnp.exp(-jnp.abs(z)))


def _cumsum_matrix(tk):
    j = lax.broadcasted_iota(jnp.int32, (tk, 2 * tk), 0)
    s = lax.broadcasted_iota(jnp.int32, (tk, 2 * tk), 1)
    return jnp.where((j > s) | (s >= tk), 1.0, 0.0).astype(MXU_DTYPE)


def _online_update(state, z, v):
    m, l, acc = state
    m_new = jnp.maximum(m, jnp.max(z, axis=-1, keepdims=True))
    alpha = jnp.exp(m - m_new)
    p = jnp.exp(z - m_new)
    l = alpha * l + jnp.sum(p, axis=-1, keepdims=True)
    acc = alpha * acc + _dot(p.astype(MXU_DTYPE), v)
    return m_new, l, acc


def _online_init(rows):
    return (jnp.full((rows, 1), NEG, _F32), jnp.zeros((rows, 1), _F32), jnp.zeros((rows, LANES), _F32))


def _topk_mask(val, k):
    idx = lax.broadcasted_iota(jnp.int32, val.shape, 1).astype(_F32)
    sel = jnp.zeros(val.shape, _F32)
    for _ in range(k):
        m = jnp.max(val, axis=-1, keepdims=True)
        first = jnp.min(jnp.where(val == m, idx, 1e9), axis=-1, keepdims=True)
        hit = idx == first
        sel = jnp.where(hit, 1.0, sel)
        val = jnp.where(hit, -jnp.inf, val)
    return sel


def _stack_group_queries(q2):
    lo = _lane_lo()
    qa, qb = q2[:, :LANES], q2[:, LANES:]
    zero = jnp.zeros_like(qa)
    return jnp.concatenate([jnp.where(lo, qa, zero), jnp.where(lo, zero, qa),
                            jnp.where(lo, qb, zero), jnp.where(lo, zero, qb)], axis=0)


def _cmp_branch(qs, kc, vc, qpos, tq):
    nc = kc.shape[0]
    s = _dot_nt(qs, kc)
    cend = lax.broadcasted_iota(jnp.int32, (1, nc), 1) * CMP_STRIDE + (CMP_LEN - 1)
    vis = cend <= qpos
    sm = jnp.where(vis, s, NEG)
    m = jnp.max(sm, axis=-1, keepdims=True)
    e = jnp.where(vis, jnp.exp(sm - m), 0.0)
    l = jnp.sum(e, axis=-1, keepdims=True)
    p = e * (1.0 / jnp.where(l > 0.0, l, 1.0))
    o_cmp = _dot(p.astype(MXU_DTYPE), vc)
    psum = p[0:tq] + p[tq:2 * tq] + p[2 * tq:3 * tq] + p[3 * tq:4 * tq]
    return o_cmp, psum


def _combine_nsa(gates, o_cmp, o_sel, o_win, tq):
    lo = _lane_lo()
    outs = []
    for gi in range(NSA_GROUP):
        r = slice(gi * tq, (gi + 1) * tq)
        outs.append(gates[:, 3 * gi:3 * gi + 1] * o_cmp[r] + gates[:, 3 * gi + 1:3 * gi + 2] * o_sel[r]
                    + gates[:, 3 * gi + 2:3 * gi + 3] * o_win[r])
    return jnp.concatenate([jnp.where(lo, outs[0], outs[1]), jnp.where(lo, outs[2], outs[3])], axis=1)


def _finish(state):
    _, l, acc = state
    return acc * (1.0 / l)


def _stage_chunks(src_ref, lead, n_chunks, dst_ref, c0):
    for r in range(CMP_STRIDE):
        x = src_ref[lead + (pl.ds(r, n_chunks, stride=CMP_STRIDE), slice(None))]
        dst_ref[pl.ds(c0, n_chunks), r * LANES:(r + 1) * LANES] = x


def _compress_finish(xk_ref, xv_ref, wk_ref, wv_ref, pe_ref, w2_ref, ok_ref, ov_ref):
    nc = xk_ref.shape[0]
    for x_ref, w_ref, t, o_ref in ((xk_ref, wk_ref, 0, ok_ref), (xv_ref, wv_ref, 1, ov_ref)):
        ab = _dot(x_ref[...].astype(MXU_DTYPE), w_ref[...])
        cst = _dot(pe_ref[t].astype(MXU_DTYPE), w_ref[...])
        for h in range(NSA_KV_HEADS):
            a = ab[:, h * 256:h * 256 + LANES]
            b = ab[:, h * 256 + LANES:(h + 1) * 256]
            c = cst[0:1, h * 256:h * 256 + LANES] + cst[1:2, h * 256 + LANES:(h + 1) * 256]
            hid = jax.nn.gelu(a + pltpu.roll(b, nc - 1, 0) + c)
            o_ref[0, h] = _dot(hid.astype(MXU_DTYPE), w2_ref[t]).astype(o_ref.dtype)


def _compress_prompt_kernel(rk_ref, rv_ref, wk_ref, wv_ref, pe_ref, w2_ref, ok_ref, ov_ref, xk_ref, xv_ref):
    n_chunks = rk_ref.shape[1] // CMP_STRIDE
    _stage_chunks(rk_ref, (0,), n_chunks, xk_ref, 0)
    _stage_chunks(rv_ref, (0,), n_chunks, xv_ref, 0)
    _compress_finish(xk_ref, xv_ref, wk_ref, wv_ref, pe_ref, w2_ref, ok_ref, ov_ref)


def _compress_sample_kernel(pt_ref, *refs, n_pages_step):
    del pt_ref
    pages_k = refs[:n_pages_step]
    pages_v = refs[n_pages_step:2 * n_pages_step]
    wk_ref, wv_ref, pe_ref, w2_ref, ok_ref, ov_ref, xk_ref, xv_ref = refs[2 * n_pages_step:]
    s = pl.program_id(1)
    per_page = pages_k[0].shape[2] // CMP_STRIDE
    for gp in range(n_pages_step):
        c0 = pl.multiple_of((s * n_pages_step + gp) * per_page, per_page)
        _stage_chunks(pages_k[gp], (0, 0), per_page, xk_ref, c0)
        _stage_chunks(pages_v[gp], (0, 0), per_page, xv_ref, c0)

    @pl.when(s == pl.num_programs(1) - 1)
    def _():
        _compress_finish(xk_ref, xv_ref, wk_ref, wv_ref, pe_ref, w2_ref, ok_ref, ov_ref)


def _compress_weights(pe, w1, w2):
    eye = jnp.eye(NSA_KV_HEADS, dtype=w1.dtype)
    big, pes = [], []
    for t in range(2):
        w1r = w1[t].reshape(2, CMP_STRIDE, HEAD_DIM, w1.shape[-1])
        wb = jnp.einsum("hH,ardn->rhdHan", eye, w1r)
        big.append(wb.reshape(CMP_STRIDE * NSA_KV_HEADS * HEAD_DIM, -1).astype(MXU_DTYPE))
        per = pe[t].reshape(2, CMP_STRIDE, 1, HEAD_DIM)
        rows = jnp.broadcast_to(per, (2, CMP_STRIDE, NSA_KV_HEADS, HEAD_DIM)).reshape(2, -1)
        pes.append(jnp.concatenate([rows, jnp.zeros((SUBLANES - 2, rows.shape[1]), rows.dtype)], axis=0))
    w2d = jnp.concatenate([w2, w2], axis=-1).astype(MXU_DTYPE)
    return big[0], big[1], jnp.stack(pes), w2d


def _compress_out(b, nc):
    shape = jax.ShapeDtypeStruct((b, NSA_KV_HEADS, nc, LANES), MXU_DTYPE)
    return [shape, shape]


def _compress_prompt(nsa_rows, cw):
    b, t, _ = nsa_rows.shape
    nc = t // CMP_STRIDE
    wk, wv, pe2, w2d = cw
    out_spec = pl.BlockSpec((1, NSA_KV_HEADS, nc, LANES), lambda i: (i, 0, 0, 0))
    return pl.pallas_call(
        _compress_prompt_kernel,
        grid=(b,),
        in_specs=[pl.BlockSpec((1, t, LANES), lambda i: (i, 0, 0)),
                  pl.BlockSpec((1, t, LANES), lambda i: (i, 0, 1)),
                  _full(wk.shape), _full(wv.shape), _full(pe2.shape), _full(w2d.shape)],
        out_specs=[out_spec, out_spec],
        out_shape=_compress_out(b, nc),
        scratch_shapes=[pltpu.VMEM((nc, CMP_STRIDE * LANES), _F32)] * 2,
        compiler_params=_params("arbitrary"),
        name="nsa_compress_prompt",
    )(nsa_rows, nsa_rows, wk, wv, pe2, w2d)


def _page_specs(layer, n, block, col_block, reverse_from=None):
    def spec(gp):
        def index(b, s, pt):
            page = s * n + gp
            if reverse_from is not None:
                page = reverse_from - page
            return (layer, pt[b, page], 0, col_block)
        return pl.BlockSpec(block, index)
    return [spec(gp) for gp in range(n)]


def _compress_sample(cache, layer, page_table, cw):
    b, n_pages = page_table.shape
    page = cache.shape[2]
    nc = n_pages * page // CMP_STRIDE
    g = PAGES_PER_STEP
    wk, wv, pe2, w2d = cw
    const = lambda shape: pl.BlockSpec(shape, lambda i, s, pt: (0,) * len(shape))
    out_spec = pl.BlockSpec((1, NSA_KV_HEADS, nc, LANES), lambda i, s, pt: (i, 0, 0, 0))
    grid_spec = pltpu.PrefetchScalarGridSpec(
        num_scalar_prefetch=1,
        grid=(b, n_pages // g),
        in_specs=_page_specs(layer, g, (1, 1, page, LANES), 0) + _page_specs(layer, g, (1, 1, page, LANES), 1)
        + [const(wk.shape), const(wv.shape), const(pe2.shape), const(w2d.shape)],
        out_specs=[out_spec, out_spec],
        scratch_shapes=[pltpu.VMEM((nc, CMP_STRIDE * LANES), _F32)] * 2,
    )
    return pl.pallas_call(
        functools.partial(_compress_sample_kernel, n_pages_step=g),
        grid_spec=grid_spec,
        out_shape=_compress_out(b, nc),
        compiler_params=_params("arbitrary", "arbitrary"),
        name="nsa_compress_sample",
    )(page_table, *([cache] * (2 * g)), wk, wv, pe2, w2d)


def _nsa_prompt_kernel(q_ref, g_ref, ck_ref, cv_ref, sk_ref, sv_ref, wk_ref, wv_ref, eblk_ref, cover_ref,
                       o_ref, *, tq, n_top):
    qi = pl.program_id(2)
    t0 = qi * tq
    rows = NSA_GROUP * tq
    qs = _stack_group_queries(q_ref[0])
    qpos_r = t0 + lax.broadcasted_iota(jnp.int32, (rows, 1), 0) % tq
    kcol = lax.broadcasted_iota(jnp.int32, (1, tq), 1)

    o_cmp, psum = _cmp_branch(qs, ck_ref[0, 0], cv_ref[0, 0], qpos_r, tq)
    imp = _dot_split(psum, cover_ref[...])

    jblk = lax.broadcasted_iota(jnp.int32, (1, LANES), 1)
    qp = t0 + lax.broadcasted_iota(jnp.int32, (tq, 1), 0)
    cur = qp // SEL_BLOCK
    forced = (jblk == 0) | (jblk == cur) | (jblk == cur - 1)
    valid = jblk * SEL_BLOCK <= qp
    val = jnp.where(valid, imp + jnp.where(forced, FORCE_BONUS, 0.0), NEG)
    sel = jnp.where(valid, _topk_mask(val, n_top), 0.0)
    bias1 = (sel - 1.0).astype(MXU_DTYPE)
    bias = jnp.concatenate([bias1] * NSA_GROUP, axis=0)

    def sel_tile(kt, state, causal):
        start = pl.multiple_of(kt * tq, tq)
        z = _dot_nt(qs, sk_ref[0, pl.ds(start, tq), :]) + _dot_nt(bias, eblk_ref[pl.ds(start, tq), :])
        if causal:
            z = jnp.where(start + kcol <= qpos_r, z, NEG)
        return _online_update(state, z, sv_ref[0, pl.ds(start, tq), :])

    state = lax.fori_loop(0, qi, lambda kt, st: sel_tile(kt, st, False), _online_init(rows))
    o_sel = _finish(sel_tile(qi, state, True))

    def win_tile(i, state):
        start = pl.multiple_of((qi - i) * tq, tq)
        z = _dot_nt(qs, wk_ref[0, pl.ds(start, tq), :])
        dist = qpos_r - (start + kcol)
        z = jnp.where((dist >= 0) & (dist <= WINDOW), z, NEG)
        return _online_update(state, z, wv_ref[0, pl.ds(start, tq), :])

    n_win = jnp.minimum(qi, WINDOW // tq) + 1
    o_win = _finish(lax.fori_loop(0, n_win, win_tile, _online_init(rows)))

    o_ref[0] = _combine_nsa(g_ref[0], o_cmp, o_sel, o_win, tq).astype(o_ref.dtype)


def _nsa_prompt(q_nsa, gates, cmp_k, cmp_v, dup, eblk, cover, *, tq):
    b, t, _ = q_nsa.shape
    nc = cmp_k.shape[2]
    n_top = min(SEL_TOPN, -(-t // SEL_BLOCK))
    cmp_spec = pl.BlockSpec((1, 1, nc, LANES), lambda i, kv, qi: (i, kv, 0, 0))
    dup_spec = lambda g0: pl.BlockSpec((1, t, LANES), lambda i, kv, qi: (i, 0, g0 + kv))
    return pl.pallas_call(
        functools.partial(_nsa_prompt_kernel, tq=tq, n_top=n_top),
        grid=(b, NSA_KV_HEADS, t // tq),
        in_specs=[pl.BlockSpec((1, tq, 2 * LANES), lambda i, kv, qi: (i, qi, kv)),
                  pl.BlockSpec((1, tq, LANES), lambda i, kv, qi: (i, qi, kv)),
                  cmp_spec, cmp_spec, dup_spec(0), dup_spec(2), dup_spec(4), dup_spec(6),
                  _full(eblk.shape), _full(cover.shape)],
        out_specs=pl.BlockSpec((1, tq, 2 * LANES), lambda i, kv, qi: (i, qi, kv)),
        out_shape=jax.ShapeDtypeStruct((b, t, NSA_Q_COLS), MXU_DTYPE),
        compiler_params=_params("parallel", "parallel", "arbitrary"),
        name="nsa_prompt",
    )(q_nsa, gates, cmp_k, cmp_v, dup, dup, dup, dup, eblk, cover)


def _sb_block(qh, kt, vt, cum, carry, keep):
    c, o = carry
    tk = kt.shape[0]
    z = _dot_nt(qh, kt)
    lk = -_softplus(z)
    if keep is not None:
        lk = jnp.where(keep, lk, 0.0)
    r = _dot_split(lk, cum)
    a = jnp.exp(lk + z + r[:, :tk] + c)
    if keep is not None:
        a = jnp.where(keep, a, 0.0)
    return c + r[:, tk:], o + _dot(a.astype(MXU_DTYPE), vt)


def _sb_prompt_kernel(q_ref, k_ref, v_ref, o_ref, *, tq):
    qi = pl.program_id(2)
    lo = _lane_lo()
    q = q_ref[0]
    cum = _cumsum_matrix(tq)
    row = lax.broadcasted_iota(jnp.int32, (tq, tq), 0)
    col = lax.broadcasted_iota(jnp.int32, (tq, tq), 1)
    out = jnp.zeros((tq, LANES), _F32)
    for hh in range(2):
        mask = lo if hh == 0 else jnp.logical_not(lo)
        qh = jnp.where(mask, q, jnp.zeros_like(q))

        def block(kb, carry, keep, qh=qh):
            start = pl.multiple_of(kb * tq, tq)
            return _sb_block(qh, k_ref[0, pl.ds(start, tq), :], v_ref[0, pl.ds(start, tq), :], cum, carry, keep)

        carry = (jnp.zeros((tq, LANES), _F32), jnp.zeros((tq, LANES), _F32))
        carry = block(qi, carry, col < row)
        _, o = lax.fori_loop(0, qi, lambda i, cr: block(qi - 1 - i, cr, None), carry)
        out = jnp.where(mask, o, out)
    o_ref[0] = out.astype(o_ref.dtype)


def _sb_prompt(q_sb, sb_bf, *, tq):
    b, t, _ = q_sb.shape
    n_pair = SB_HEADS // 2
    return pl.pallas_call(
        functools.partial(_sb_prompt_kernel, tq=tq),
        grid=(b, n_pair, t // tq),
        in_specs=[pl.BlockSpec((1, tq, LANES), lambda i, p, qi: (i, qi, p)),
                  pl.BlockSpec((1, t, LANES), lambda i, p, qi: (i, 0, p)),
                  pl.BlockSpec((1, t, LANES), lambda i, p, qi: (i, 0, n_pair + p))],
        out_specs=pl.BlockSpec((1, tq, LANES), lambda i, p, qi: (i, qi, p)),
        out_shape=jax.ShapeDtypeStruct((b, t, SB_HEADS * HEAD_DIM), MXU_DTYPE),
        compiler_params=_params("parallel", "parallel", "arbitrary"),
        name="sb_prompt",
    )(q_sb, sb_bf, sb_bf)


def _kmean_kernel(k_ref, o_ref):
    x = k_ref[0]
    nb = x.shape[0] // MOBA_BLOCK
    o_ref[0] = jnp.mean(x.reshape(nb, MOBA_BLOCK, x.shape[1]), axis=1)


def _kmean_prompt(mb_rows):
    b, t, _ = mb_rows.shape
    nb = SUBLANES
    return pl.pallas_call(
        _kmean_kernel,
        grid=(b, t // (nb * MOBA_BLOCK)),
        in_specs=[pl.BlockSpec((1, nb * MOBA_BLOCK, 2 * LANES), lambda i, j: (i, j, 0))],
        out_specs=pl.BlockSpec((1, nb, 2 * LANES), lambda i, j: (i, j, 0)),
        out_shape=jax.ShapeDtypeStruct((b, t // MOBA_BLOCK, 2 * LANES), _F32),
        compiler_params=_params("parallel", "parallel"),
        name="moba_kmean_prompt",
    )(mb_rows)


def _moba_gate(qh, kmean, own):
    s = lax.dot_general(qh, kmean, (((1,), (1,)), ((), ())), preferred_element_type=_F32,
                        precision=lax.Precision.HIGHEST)
    j = lax.broadcasted_iota(jnp.int32, (1, kmean.shape[0]), 1)
    past = j < own
    sel = jnp.where(past, _topk_mask(jnp.where(past, s, NEG), MOBA_TOPK), 0.0)
    return (sel - 1.0).astype(MXU_DTYPE)


def _block_onehot(n_keys, blk):
    j = lax.broadcasted_iota(jnp.int32, (n_keys, LANES), 1)
    return jnp.where(j == blk, -NEG, 0.0).astype(MXU_DTYPE)


def _moba_prompt_kernel(q_ref, km_ref, k_ref, v_ref, o_ref, *, tq):
    qi = pl.program_id(2)
    lo = _lane_lo()
    q = q_ref[0]
    scale = HEAD_DIM ** -0.5
    row = lax.broadcasted_iota(jnp.int32, (tq, tq), 0)
    col = lax.broadcasted_iota(jnp.int32, (tq, tq), 1)
    out = jnp.zeros((tq, LANES), _F32)
    for hh in range(2):
        mask = lo if hh == 0 else jnp.logical_not(lo)
        qf = jnp.where(mask, q, 0.0)
        bias = _moba_gate(qf, jnp.where(mask, km_ref[0], 0.0), qi)
        qh = (qf * scale).astype(MXU_DTYPE)

        def tile(kb, state, own, qh=qh, bias=bias):
            start = pl.multiple_of(kb * tq, tq)
            z = _dot_nt(qh, k_ref[0, pl.ds(start, tq), :])
            if own:
                z = jnp.where(col <= row, z, NEG)
            else:
                z = z + _dot_nt(bias, _block_onehot(tq, kb))
            return _online_update(state, z, v_ref[0, pl.ds(start, tq), :])

        state = tile(qi, _online_init(tq), True)
        state = lax.fori_loop(0, qi, lambda kb, st: tile(kb, st, False), state)
        out = jnp.where(mask, _finish(state), out)
    o_ref[0] = out.astype(o_ref.dtype)


def _moba_prompt(q_mb, kmean, mb_bf):
    b, t, _ = q_mb.shape
    tq = MOBA_BLOCK
    n_pair = MOBA_HEADS // 2
    return pl.pallas_call(
        functools.partial(_moba_prompt_kernel, tq=tq),
        grid=(b, n_pair, t // tq),
        in_specs=[pl.BlockSpec((1, tq, LANES), lambda i, p, qi: (i, qi, p)),
                  pl.BlockSpec((1, LANES, LANES), lambda i, p, qi: (i, 0, p)),
                  pl.BlockSpec((1, t, LANES), lambda i, p, qi: (i, 0, p)),
                  pl.BlockSpec((1, t, LANES), lambda i, p, qi: (i, 0, n_pair + p))],
        out_specs=pl.BlockSpec((1, tq, LANES), lambda i, p, qi: (i, qi, p)),
        out_shape=jax.ShapeDtypeStruct((b, t, MOBA_HEADS * HEAD_DIM), MXU_DTYPE),
        compiler_params=_params("parallel", "parallel", "arbitrary"),
        name="moba_prompt",
    )(q_mb, kmean, mb_bf, mb_bf)


def _row_scores(q, keys, n):
    return [jnp.sum(q * keys[j:j + 1, :], axis=-1, keepdims=True) for j in range(n)]


def _new_key_softmax(q, keys, vals, n, visible):
    zs = [jnp.where(visible(j), z, NEG) for j, z in enumerate(_row_scores(q, keys, n))]
    m = functools.reduce(jnp.maximum, zs)
    l = jnp.zeros_like(m)
    acc = jnp.zeros((q.shape[0], vals.shape[1]), _F32)
    for j, z in enumerate(zs):
        p = jnp.exp(z - m)
        l = l + p
        acc = acc + p * vals[j:j + 1, :]
    return m, l, acc


def _state_store(refs, idx, state):
    for ref, x in zip(refs, state):
        ref[idx] = jnp.broadcast_to(x, ref.shape[1:])


def _state_load(refs, idx):
    m_ref, l_ref, acc_ref = refs
    return m_ref[idx][:, 0:1], l_ref[idx][:, 0:1], acc_ref[idx]


def _dup_kv(x, kv):
    rolled = pltpu.roll(x, HEAD_DIM, 1)
    lo = _lane_lo()
    return jnp.where(lo, x, rolled) if kv == 0 else jnp.where(lo, rolled, x)


def _stack_heads(q, n_heads):
    head = lax.broadcasted_iota(jnp.int32, (1, q.shape[1]), 1) // HEAD_DIM
    return jnp.concatenate([jnp.where(head == h, q, jnp.zeros_like(q)) for h in range(n_heads)], axis=0)


def _unstack_heads(o, n_heads):
    rows = o.shape[0] // n_heads
    head = lax.broadcasted_iota(jnp.int32, (1, o.shape[1]), 1) // HEAD_DIM
    out = jnp.zeros((rows, o.shape[1]), _F32)
    for h in range(n_heads):
        out = jnp.where(head == h, o[h * rows:(h + 1) * rows], out)
    return out


def _nsa_sample_kernel(pt_ref, q_ref, g_ref, ck_ref, cv_ref, new_ref, win_ref, cover_ref, *refs,
                       n_pages_step, n_new, past_len, n_top):
    del pt_ref
    pages = refs[:n_pages_step]
    o_ref, m_ref, l_ref, acc_ref, bias_ref, ocmp_ref, owin_ref = refs[n_pages_step:]
    s = pl.program_id(1)
    tq = SAMPLE_ROWS
    rows = NSA_GROUP * tq
    page_keys = pages[0].shape[2]
    n_past_blk = past_len // SEL_BLOCK
    qi_r = lax.broadcasted_iota(jnp.int32, (rows, 1), 0) % tq
    qpos_r = past_len + qi_r
    state_refs = (m_ref, l_ref, acc_ref)

    for kv in range(NSA_KV_HEADS):
        qs = _stack_group_queries(q_ref[0][:, kv * 2 * LANES:(kv + 1) * 2 * LANES])
        new = new_ref[0].astype(_F32)
        sel_k_new = new[:, kv * LANES:(kv + 1) * LANES]
        sel_v_new = new[:, (2 + kv) * LANES:(3 + kv) * LANES]
        win_k_new = new[:, (4 + kv) * LANES:(5 + kv) * LANES]
        win_v_new = new[:, (6 + kv) * LANES:(7 + kv) * LANES]

        @pl.when(s == 0)
        def _(kv=kv, qs=qs, sel_k_new=sel_k_new, sel_v_new=sel_v_new, win_k_new=win_k_new, win_v_new=win_v_new):
            qf = qs.astype(_F32)
            o_cmp, psum = _cmp_branch(qs, ck_ref[0, kv], cv_ref[0, kv], qpos_r, tq)
            ocmp_ref[kv] = o_cmp
            imp = _dot_split(psum, cover_ref[...])
            jblk = lax.broadcasted_iota(jnp.int32, (1, LANES), 1)
            cur = (past_len + lax.broadcasted_iota(jnp.int32, (tq, 1), 0)) // SEL_BLOCK
            forced = (jblk == 0) | (jblk == cur) | (jblk == cur - 1)
            valid = jblk < n_past_blk
            val = jnp.where(valid, imp + jnp.where(forced, FORCE_BONUS, 0.0), NEG)
            sel = jnp.where(valid, _topk_mask(val, n_top - 1), 0.0)
            bias_ref[kv] = jnp.concatenate([sel - 1.0] * NSA_GROUP, axis=0)
            visible = lambda j: j <= qi_r
            _state_store(state_refs, kv, _new_key_softmax(qf, sel_k_new, sel_v_new, n_new, visible))
            wstate = _new_key_softmax(qf, win_k_new, win_v_new, n_new, visible)
            w = win_ref[0, 0]
            n_win = w.shape[0]
            kw = _dup_kv(w[:, :LANES], kv).astype(MXU_DTYPE)
            vw = _dup_kv(w[:, LANES:], kv).astype(MXU_DTYPE)
            dist = qpos_r - (past_len - n_win + lax.broadcasted_iota(jnp.int32, (1, n_win), 1))
            z = jnp.where((dist >= 0) & (dist <= WINDOW), _dot_nt(qs, kw), NEG)
            owin_ref[kv] = _finish(_online_update(wstate, z, vw))

        state = _state_load(state_refs, kv)
        bias = bias_ref[kv].astype(MXU_DTYPE)
        key_half = lax.broadcasted_iota(jnp.int32, (page_keys, LANES), 0) // SEL_BLOCK
        lane = lax.broadcasted_iota(jnp.int32, (page_keys, LANES), 1)
        for gp, page in enumerate(pages):
            blk0 = (s * n_pages_step + gp) * (page_keys // SEL_BLOCK)
            onehot = jnp.where(lane == blk0 + key_half, -NEG, 0.0).astype(MXU_DTYPE)
            x = page[0, 0]
            ks = _dup_kv(x[:, :LANES], kv).astype(MXU_DTYPE)
            vs = _dup_kv(x[:, LANES:], kv).astype(MXU_DTYPE)
            z = _dot_nt(qs, ks) + _dot_nt(bias, onehot)
            state = _online_update(state, z, vs)
        _state_store(state_refs, kv, state)

    @pl.when(s == pl.num_programs(1) - 1)
    def _():
        outs = []
        for kv in range(NSA_KV_HEADS):
            o_sel = _finish(_state_load(state_refs, kv))
            gates = g_ref[0][:, kv * LANES:(kv + 1) * LANES]
            outs.append(_combine_nsa(gates, ocmp_ref[kv], o_sel, owin_ref[kv], tq))
        o_ref[0] = jnp.concatenate(outs, axis=1).astype(o_ref.dtype)


def _sample_call(kernel_fn, name, page_table, pre_args, pre_specs, page_arrays, page_specs, out_block, out_shape,
                 scratch):
    b, n_pages = page_table.shape
    n_step = len(page_specs)
    grid_spec = pltpu.PrefetchScalarGridSpec(
        num_scalar_prefetch=1,
        grid=(b, n_pages // n_step),
        in_specs=pre_specs + page_specs,
        out_specs=out_block,
        scratch_shapes=scratch,
    )
    return pl.pallas_call(kernel_fn, grid_spec=grid_spec, out_shape=out_shape,
                          compiler_params=_params("arbitrary", "arbitrary"), name=name,
                          )(page_table, *pre_args, *page_arrays)


def _seq_block(shape):
    zeros = (0,) * (len(shape) - 1)
    return pl.BlockSpec((1,) + tuple(shape[1:]), lambda b, s, pt: (b,) + zeros)


def _const_block(shape):
    zeros = (0,) * len(shape)
    return pl.BlockSpec(tuple(shape), lambda b, s, pt: zeros)


def _nsa_sample(q, gates, cmp_k, cmp_v, dup_new, cache, cache_win, layer, page_table, cover, n_new):
    b, n_pages = page_table.shape
    page = cache.shape[2]
    past_len = n_pages * page
    g = PAGES_PER_STEP
    rows = NSA_GROUP * SAMPLE_ROWS
    n_top = min(SEL_TOPN, past_len // SEL_BLOCK + 1)
    win_spec = pl.BlockSpec((1, 1) + cache_win.shape[2:], lambda i, s, pt: (layer, i, 0, 0))
    vmem = lambda: pltpu.VMEM((NSA_KV_HEADS, rows, LANES), _F32)
    return _sample_call(
        functools.partial(_nsa_sample_kernel, n_pages_step=g, n_new=n_new, past_len=past_len, n_top=n_top),
        "nsa_sample", page_table,
        [q, gates, cmp_k, cmp_v, dup_new, cache_win, cover],
        [_seq_block(q.shape), _seq_block(gates.shape), _seq_block(cmp_k.shape), _seq_block(cmp_v.shape),
         _seq_block(dup_new.shape), win_spec, _const_block(cover.shape)],
        [cache] * g, _page_specs(layer, g, (1, 1, page, 2 * LANES), 1),
        _seq_block(q.shape), jax.ShapeDtypeStruct(q.shape, MXU_DTYPE),
        [vmem() for _ in range(6)])


def _sb_sample_kernel(pt_ref, q_ref, new_ref, *refs, n_pages_step, n_new):
    del pt_ref
    pages = refs[:n_pages_step]
    o_ref, c_ref, acc_ref = refs[n_pages_step:]
    s = pl.program_id(1)
    width = SB_HEADS * HEAD_DIM
    qs = _stack_heads(q_ref[0], SB_HEADS)
    rows = qs.shape[0]
    qi_r = lax.broadcasted_iota(jnp.int32, (rows, 1), 0) % SAMPLE_ROWS

    @pl.when(s == 0)
    def _():
        qf = qs.astype(_F32)
        new = new_ref[0].astype(_F32)
        zs = _row_scores(qf, new[:, :width], n_new)
        c = jnp.zeros((rows, 1), _F32)
        o = jnp.zeros((rows, width), _F32)
        for j in reversed(range(n_new)):
            seen = j < qi_r
            lk = jnp.where(seen, -_softplus(zs[j]), 0.0)
            a = jnp.where(seen, jnp.exp(lk + zs[j] + c), 0.0)
            o = o + a * new[j:j + 1, width:]
            c = c + lk
        c_ref[...] = jnp.broadcast_to(c, c_ref.shape)
        acc_ref[...] = o

    page_keys = pages[0].shape[2]
    cum = _cumsum_matrix(page_keys)
    carry = (c_ref[...], acc_ref[...])
    for page in pages:
        x = page[0, 0]
        carry = _sb_block(qs, x[:, :width].astype(MXU_DTYPE), x[:, width:].astype(MXU_DTYPE), cum, carry, None)
    c_ref[...], acc_ref[...] = carry

    @pl.when(s == pl.num_programs(1) - 1)
    def _():
        o_ref[0] = _unstack_heads(acc_ref[...], SB_HEADS).astype(o_ref.dtype)


def _sb_sample(q, new_bf, cache, layer, page_table, n_new):
    b, n_pages = page_table.shape
    page = cache.shape[2]
    g = PAGES_PER_STEP
    rows = SB_HEADS * SAMPLE_ROWS
    width = SB_HEADS * HEAD_DIM
    return _sample_call(
        functools.partial(_sb_sample_kernel, n_pages_step=g, n_new=n_new),
        "sb_sample", page_table,
        [q, new_bf], [_seq_block(q.shape), _seq_block(new_bf.shape)],
        [cache] * g, _page_specs(layer, g, (1, 1, page, 2 * width), 0, reverse_from=n_pages - 1),
        _seq_block(q.shape), jax.ShapeDtypeStruct(q.shape, MXU_DTYPE),
        [pltpu.VMEM((rows, LANES), _F32), pltpu.VMEM((rows, width), _F32)])


def _kmean_sample_kernel(pt_ref, *refs):
    del pt_ref
    pages, o_ref = refs[:-1], refs[-1]
    per_blk = MOBA_BLOCK // pages[0].shape[2]
    means = []
    for i in range(0, len(pages), per_blk):
        tot = sum(jnp.sum(p[0, 0], axis=0, keepdims=True) for p in pages[i:i + per_blk])
        means.append(tot * (1.0 / MOBA_BLOCK))
    o_ref[0] = jnp.concatenate(means, axis=0)


def _kmean_sample(cache, layer, page_table):
    b, n_pages = page_table.shape
    page = cache.shape[2]
    width = MOBA_HEADS * HEAD_DIM
    g = SUBLANES * MOBA_BLOCK // page
    out_block = pl.BlockSpec((1, SUBLANES, width), lambda i, s, pt: (i, s, 0))
    return _sample_call(
        _kmean_sample_kernel, "moba_kmean_sample", page_table, [], [],
        [cache] * g, _page_specs(layer, g, (1, 1, page, width), 0),
        out_block, jax.ShapeDtypeStruct((b, n_pages * page // MOBA_BLOCK, width), _F32), [])


def _moba_sample_kernel(pt_ref, q_ref, km_ref, new_ref, *refs, n_pages_step, n_new, past_len):
    del pt_ref
    pages = refs[:n_pages_step]
    o_ref, m_ref, l_ref, acc_ref, bias_ref = refs[n_pages_step:]
    s = pl.program_id(1)
    width = MOBA_HEADS * HEAD_DIM
    qf = _stack_heads(q_ref[0], MOBA_HEADS)
    rows = qf.shape[0]
    qs = (qf * HEAD_DIM ** -0.5).astype(MXU_DTYPE)
    qi_r = lax.broadcasted_iota(jnp.int32, (rows, 1), 0) % SAMPLE_ROWS
    state_refs = (m_ref, l_ref, acc_ref)

    @pl.when(s == 0)
    def _():
        bias_ref[...] = _moba_gate(qf, km_ref[0], past_len // MOBA_BLOCK).astype(_F32)
        new = new_ref[0].astype(_F32)
        state = _new_key_softmax(qs.astype(_F32), new[:, :width], new[:, width:], n_new, lambda j: j <= qi_r)
        _state_store(state_refs, 0, state)

    state = _state_load(state_refs, 0)
    bias = bias_ref[...].astype(MXU_DTYPE)
    page_keys = pages[0].shape[2]
    for gp, page in enumerate(pages):
        blk = (s * n_pages_step + gp) * page_keys // MOBA_BLOCK
        x = page[0, 0]
        z = _dot_nt(qs, x[:, :width].astype(MXU_DTYPE)) + _dot_nt(bias, _block_onehot(page_keys, blk))
        state = _online_update(state, z, x[:, width:].astype(MXU_DTYPE))
    _state_store(state_refs, 0, state)

    @pl.when(s == pl.num_programs(1) - 1)
    def _():
        o_ref[0] = _unstack_heads(_finish(_state_load(state_refs, 0)), MOBA_HEADS).astype(o_ref.dtype)


def _moba_sample(q, kmean, new_bf, cache, layer, page_table, n_new):
    b, n_pages = page_table.shape
    page = cache.shape[2]
    g = PAGES_PER_STEP
    rows = MOBA_HEADS * SAMPLE_ROWS
    width = MOBA_HEADS * HEAD_DIM
    return _sample_call(
        functools.partial(_moba_sample_kernel, n_pages_step=g, n_new=n_new, past_len=n_pages * page),
        "moba_sample", page_table,
        [q, kmean, new_bf], [_seq_block(q.shape), _seq_block(kmean.shape), _seq_block(new_bf.shape)],
        [cache] * g, _page_specs(layer, g, (1, 1, page, 2 * width), 0),
        _seq_block(q.shape), jax.ShapeDtypeStruct(q.shape, MXU_DTYPE),
        [pltpu.VMEM((1, rows, LANES), _F32), pltpu.VMEM((1, rows, LANES), _F32),
         pltpu.VMEM((1, rows, width), _F32), pltpu.VMEM((rows, LANES), _F32)])


def _rope_tables(pos):
    half = HEAD_DIM // 2
    inv = ROPE_THETA ** (-jnp.arange(half, dtype=_F32) / half)
    ang = pos.astype(_F32)[:, None] * inv[None, :]
    cos, sin = jnp.cos(ang), jnp.sin(ang)
    zero = jnp.zeros_like(sin)
    reps = LANES // HEAD_DIM
    return (jnp.tile(cos, (1, 2 * reps)), jnp.tile(jnp.concatenate([-sin, zero], axis=1), (1, reps)),
            jnp.tile(jnp.concatenate([zero, sin], axis=1), (1, reps)))


def _pack_w_in(w_in):
    d = w_in.shape[0]
    o1 = NSA_Q_COLS
    o2 = o1 + NSA_KV_COLS
    o3 = o2 + NSA_G_COLS
    o4 = o3 + SB_COLS
    q, kv, g, sb, mb = w_in[:, :o1], w_in[:, o1:o2], w_in[:, o2:o3], w_in[:, o3:o4], w_in[:, o4:]

    def dup(i):
        grp = kv[:, i * LANES:(i + 1) * LANES]
        h0, h1 = grp[:, :HEAD_DIM], grp[:, HEAD_DIM:]
        return jnp.concatenate([h0, h0, h1, h1], axis=1)

    per_kv = NSA_GROUP * 3
    gate = jnp.pad(g.reshape(d, NSA_KV_HEADS, per_kv), ((0, 0), (0, 0), (0, LANES - per_kv)))
    return jnp.concatenate([q, kv, sb, mb, dup(2), dup(3), dup(4), dup(5), gate.reshape(d, -1)],
                           axis=1).astype(MXU_DTYPE)


def _block_tables(t):
    key_blk = jnp.arange(t)[:, None] // SEL_BLOCK
    j = jnp.arange(LANES)[None, :]
    eblk = jnp.where(key_blk == j, -NEG, 0.0).astype(MXU_DTYPE)
    i = jnp.arange(t // CMP_STRIDE)[:, None]
    ratio = SEL_BLOCK // CMP_STRIDE
    cover = ((i <= ratio * j + ratio - 1) & (i >= ratio * j - (CMP_LEN // CMP_STRIDE - 1))).astype(MXU_DTYPE)
    return eblk, cover


def _layer_weights(l, norm_g, ffn_w_gate, ffn_w_up, ffn_w_down, w_in, nsa_cmp_pe, nsa_cmp_w1, nsa_cmp_w2,
                   w_branch_nsa, w_branch_sb, w_branch_moba, w_merge_gate, w_out, w_ple_proj, w_ple_gate):
    c = lambda w: w.astype(MXU_DTYPE)
    return dict(
        g=[norm_g[l, i][None, :] for i in range(norm_g.shape[1])],
        ffn=[(c(ffn_w_gate[l, i]), c(ffn_w_up[l, i]), c(ffn_w_down[l, i])) for i in range(2)],
        w_all=_pack_w_in(w_in[l]),
        cw=_compress_weights(nsa_cmp_pe[l], nsa_cmp_w1[l], nsa_cmp_w2[l]),
        merge=(c(w_branch_nsa[l]), c(w_branch_sb[l]), c(w_branch_moba[l]), c(w_merge_gate[l]), c(w_out[l])),
        ple=(c(w_ple_gate[l]), c(w_ple_proj[l])),
    )


def _token_tile(n, cap):
    tm = cap
    while n % tm:
        tm //= 2
    return tm


def _ff_tile(f, cap=512):
    best = LANES
    for k in range(1, f // LANES + 1):
        if f % (k * LANES) == 0 and k * LANES <= cap:
            best = k * LANES
    return best


def _prompt_layer(h, p_l, lw, tabs, consts, b, t):
    n, d = h.shape
    tm = _token_tile(t, 512)
    tf = _ff_tile(lw["ffn"][0][0].shape[1])
    g = lw["g"]
    h = _ffn(h, g[0], g[1], *lw["ffn"][0], tm=tm, tf=tf)
    per_seq = t // tm
    outs = _inproj(h, g[2], lw["w_all"], tabs, lambda i: (i % per_seq, 0), tm=tm)
    q_nsa, nsa_rows, win_rows, q_sb, sb_rows, sb_bf, q_mb, mb_rows, mb_bf, dup, gates = [
        o.reshape(b, t, o.shape[1]) for o in outs]
    eblk, cover = consts
    cmp_k, cmp_v = _compress_prompt(nsa_rows, lw["cw"])
    o_nsa = _nsa_prompt(q_nsa, gates, cmp_k, cmp_v, dup, eblk, cover, tq=LANES)
    o_sb = _sb_prompt(q_sb, sb_bf, tq=LANES)
    kmean = _kmean_prompt(mb_rows)
    kmean = jnp.pad(kmean, ((0, 0), (0, LANES - kmean.shape[1]), (0, 0)))
    o_mb = _moba_prompt(q_mb, kmean, mb_bf)
    flat = lambda o: o.reshape(n, o.shape[2])
    h = _merge(h, g[2], g[3], flat(o_nsa), flat(o_sb), flat(o_mb), *lw["merge"], tm=tm)
    h = _ffn(h, g[4], g[5], *lw["ffn"][1], ple=(p_l, *lw["ple"], g[6]), tm=tm, tf=tf)
    return h, (nsa_rows, sb_rows, mb_rows, win_rows)


def _sample_layer(h, p_l, lw, tabs, cover, caches, layer, page_table, n_new):
    n, d = h.shape
    b = page_table.shape[0]
    tm = _token_tile(n, 512)
    tf = _ff_tile(lw["ffn"][0][0].shape[1])
    g = lw["g"]
    cache_nsa, cache_sb, cache_mb, cache_win = caches
    h = _ffn(h, g[0], g[1], *lw["ffn"][0], tm=tm, tf=tf)
    outs = _inproj(h, g[2], lw["w_all"], tabs, lambda i: (i, 0), tm=tm)
    q_nsa, nsa_rows, win_rows, q_sb, sb_rows, sb_bf, q_mb, mb_rows, mb_bf, dup, gates = [
        o.reshape(b, SAMPLE_ROWS, o.shape[1]) for o in outs]
    cmp_k, cmp_v = _compress_sample(cache_nsa, layer, page_table, lw["cw"])
    o_nsa = _nsa_sample(q_nsa, gates, cmp_k, cmp_v, dup, cache_nsa, cache_win, layer, page_table, cover, n_new)
    o_sb = _sb_sample(q_sb, sb_bf, cache_sb, layer, page_table, n_new)
    kmean = _kmean_sample(cache_mb, layer, page_table)
    kmean = jnp.pad(kmean, ((0, 0), (0, LANES - kmean.shape[1]), (0, 0)))
    o_mb = _moba_sample(q_mb, kmean, mb_bf, cache_mb, layer, page_table, n_new)
    flat = lambda o: o.reshape(n, o.shape[2])
    h = _merge(h, g[2], g[3], flat(o_nsa), flat(o_sb), flat(o_mb), *lw["merge"], tm=tm)
    h = _ffn(h, g[4], g[5], *lw["ffn"][1], ple=(p_l, *lw["ple"], g[6]), tm=tm, tf=tf)
    return h, (nsa_rows[:, :n_new], sb_rows[:, :n_new], mb_rows[:, :n_new], win_rows[:, :n_new])


def kernel(x_prompt, x_sample, p_prompt, p_sample, cache_nsa, cache_sb, cache_moba, cache_win, page_table,
           norm_g, ffn_w_gate, ffn_w_up, ffn_w_down, w_in, nsa_cmp_pe, nsa_cmp_w1, nsa_cmp_w2,
           w_branch_nsa, w_branch_sb, w_branch_moba, w_merge_gate, w_out, w_ple_proj, w_ple_gate):
    depth = norm_g.shape[0]
    b, t, d = x_prompt.shape
    bs, ts, _ = x_sample.shape
    n_pages = page_table.shape[1]
    page = cache_nsa.shape[2]
    past_len = n_pages * page
    assert ts <= SAMPLE_ROWS and t % LANES == 0 and t // SEL_BLOCK <= LANES and past_len // SEL_BLOCK <= LANES
    assert n_pages % (SUBLANES * MOBA_BLOCK // page) == 0 and cache_win.shape[2] == WINDOW
    weights = (norm_g, ffn_w_gate, ffn_w_up, ffn_w_down, w_in, nsa_cmp_pe, nsa_cmp_w1, nsa_cmp_w2,
               w_branch_nsa, w_branch_sb, w_branch_moba, w_merge_gate, w_out, w_ple_proj, w_ple_gate)
    tabs_p = _rope_tables(jnp.arange(t, dtype=jnp.int32))
    tabs_s = _rope_tables(past_len + jnp.arange(bs * SAMPLE_ROWS, dtype=jnp.int32) % SAMPLE_ROWS)
    consts = _block_tables(t)
    cover_s = _block_tables(past_len)[1]
    pad_rows = lambda x: jnp.pad(x, ((0, 0), (0, SAMPLE_ROWS - ts), (0, 0))).reshape(bs * SAMPLE_ROWS, -1)
    flat4 = lambda c: c.reshape(c.shape[0], c.shape[1], c.shape[2], -1)
    caches = (flat4(cache_nsa), flat4(cache_sb), flat4(cache_moba), flat4(cache_win))
    h_p = x_prompt.reshape(b * t, d)
    h_s = pad_rows(x_sample)
    st_p, st_s = [], []
    for l in range(depth):
        lw = _layer_weights(l, *weights)
        h_p, rows = _prompt_layer(h_p, p_prompt[l].reshape(b * t, -1), lw, tabs_p, consts, b, t)
        st_p.append(rows)
        h_s, rows = _sample_layer(h_s, pad_rows(p_sample[l]), lw, tabs_s, cover_s, caches, l, page_table, ts)
        win_state = jnp.concatenate([caches[3][l][:, ts:], rows[3]], axis=1)
        st_s.append(rows[:3] + (win_state,))
    keep = min(WINDOW, t)
    heads = ((4, NSA_KV_HEADS), (2, SB_HEADS), (2, MOBA_HEADS))
    out = [h_p.reshape(b, t, d), h_s.reshape(bs, SAMPLE_ROWS, d)[:, :ts]]
    for i, (parts, nh) in enumerate(heads):
        out.append(jnp.stack([s[i] for s in st_p]).reshape(depth, b, t, parts, nh, HEAD_DIM))
        out.append(jnp.stack([s[i] for s in st_s]).reshape(depth, bs, ts, parts, nh, HEAD_DIM))
    out.append(jnp.stack([s[3][:, t - keep:] for s in st_p]).reshape(depth, b, keep, 2, NSA_KV_HEADS, HEAD_DIM))
    out.append(jnp.stack([s[3] for s in st_s]).reshape(depth, bs, -1, 2, NSA_KV_HEADS, HEAD_DIM))
    return tuple(out)
```

```python
import functools

import jax
import jax.numpy as jnp
from jax import lax
from jax.experimental import pallas as pl
from jax.experimental.pallas import tpu as pltpu

HEAD_DIM = 64
NSA_HEADS = 8
NSA_KV_HEADS = 2
NSA_GROUP = NSA_HEADS // NSA_KV_HEADS
SB_HEADS = 4
MOBA_HEADS = 4
CMP_LEN = 32
CMP_STRIDE = 16
SEL_BLOCK = 64
SEL_TOPN = 16
WINDOW = 512
MOBA_BLOCK = 256
MOBA_TOPK = 3
ROPE_THETA = 10000.0
RMS_EPS = 1e-6
NEG = -1e30
FORCE_BONUS = 1e4
SB_DEAD = -104.0

LANES = 128
SUBLANES = 8
MXU_DTYPE = jnp.bfloat16
VMEM_LIMIT = 56 * 1024 * 1024
SAMPLE_ROWS = 8
PAGES_PER_STEP = 8

NSA_Q_COLS = NSA_HEADS * HEAD_DIM
NSA_KV_COLS = 6 * NSA_KV_HEADS * HEAD_DIM
NSA_G_COLS = 3 * NSA_HEADS
SB_COLS = 3 * SB_HEADS * HEAD_DIM
MOBA_COLS = 3 * MOBA_HEADS * HEAD_DIM

_F32 = jnp.float32


def _dot(a, b):
    return jnp.dot(a, b, preferred_element_type=_F32)


def _dot_nt(a, b):
    return lax.dot_general(a, b, (((1,), (1,)), ((), ())), preferred_element_type=_F32)


def _dot_split(x, w):
    hi = x.astype(MXU_DTYPE)
    mid = (x - hi.astype(_F32)).astype(MXU_DTYPE)
    return _dot(hi, w) + _dot(mid, w)


def _rms(x, g):
    return x * lax.rsqrt(jnp.mean(x * x, axis=-1, keepdims=True) + RMS_EPS) * g


def _params(*sem):
    return pltpu.CompilerParams(dimension_semantics=sem, vmem_limit_bytes=VMEM_LIMIT)


def _full(shape):
    n = len(shape)
    return pl.BlockSpec(shape, lambda *_: (0,) * n)


def _ffn_kernel(*refs, n_ff, with_ple):
    if with_ple:
        (h_ref, gpre_ref, gpost_ref, wg_ref, wu_ref, wd_ref,
         p_ref, wpg_ref, wpp_ref, gple_ref, o_ref, xn_ref, acc_ref) = refs
    else:
        h_ref, gpre_ref, gpost_ref, wg_ref, wu_ref, wd_ref, o_ref, xn_ref, acc_ref = refs
    j = pl.program_id(1)

    @pl.when(j == 0)
    def _():
        xn_ref[...] = _rms(h_ref[...], gpre_ref[...]).astype(MXU_DTYPE)
        acc_ref[...] = jnp.zeros_like(acc_ref)

    xn = xn_ref[...]
    g = _dot(xn, wg_ref[...])
    u = _dot(xn, wu_ref[...])
    a = (g * jax.nn.sigmoid(g)) * u
    acc_ref[...] += _dot(a.astype(MXU_DTYPE), wd_ref[...])

    @pl.when(j == n_ff - 1)
    def _():
        h = h_ref[...] + 0.5 * _rms(acc_ref[...], gpost_ref[...])
        if with_ple:
            gate = jax.nn.sigmoid(_dot(h.astype(MXU_DTYPE), wpg_ref[...]))
            ple = gate * _dot(p_ref[...].astype(MXU_DTYPE), wpp_ref[...])
            h = h + _rms(ple, gple_ref[...])
        o_ref[...] = h


def _ffn(h, g_pre, g_post, wg, wu, wd, ple=None, *, tm, tf):
    n, d = h.shape
    f = wg.shape[1]
    n_ff = f // tf
    row = lambda i, j: (i, 0)
    in_specs = [pl.BlockSpec((tm, d), row), _full((1, d)), _full((1, d)),
                pl.BlockSpec((d, tf), lambda i, j: (0, j)),
                pl.BlockSpec((d, tf), lambda i, j: (0, j)),
                pl.BlockSpec((tf, d), lambda i, j: (j, 0))]
    args = [h, g_pre, g_post, wg, wu, wd]
    if ple is not None:
        p, wpg, wpp, g_ple = ple
        in_specs += [pl.BlockSpec((tm, p.shape[1]), row), _full(wpg.shape), _full(wpp.shape), _full((1, d))]
        args += [p, wpg, wpp, g_ple]
    return pl.pallas_call(
        functools.partial(_ffn_kernel, n_ff=n_ff, with_ple=ple is not None),
        grid=(n // tm, n_ff),
        in_specs=in_specs,
        out_specs=pl.BlockSpec((tm, d), row),
        out_shape=jax.ShapeDtypeStruct((n, d), _F32),
        scratch_shapes=[pltpu.VMEM((tm, d), MXU_DTYPE), pltpu.VMEM((tm, d), _F32)],
        compiler_params=_params("parallel", "arbitrary"),
        name="ffn_ple" if ple is not None else "ffn",
    )(*args)


_G_QNSA = 0
_G_KV = 8
_G_SB = 14
_G_MB = 20
_G_GATE = 26
_ROPE_GROUPS = tuple(range(8)) + (8, 10, 12, 20, 21, 22, 23)
_INPROJ_OUT = (("q_nsa", _G_QNSA, 8, True, None), ("nsa", _G_KV, 4, False, _F32), ("win", _G_KV + 4, 2, False, _F32),
               ("q_sb", _G_SB, 2, True, None), ("sb", _G_SB + 2, 4, False, _F32),
               ("q_mb", _G_MB, 2, False, _F32), ("mb", _G_MB + 2, 4, False, _F32))
_INPROJ_NARROW = ("nsa", "win", "sb", "mb")


def _inproj_kernel(h_ref, g_ref, w_ref, cos_ref, sa_ref, sb_ref, *out_refs):
    xn = _rms(h_ref[...], g_ref[...]).astype(MXU_DTYPE)
    cos, sa, sb = cos_ref[...], sa_ref[...], sb_ref[...]
    scale = HEAD_DIM ** -0.5

    def group(gi):
        y = _dot(xn, w_ref[:, gi * LANES:(gi + 1) * LANES])
        if gi in _ROPE_GROUPS:
            y = y * cos + pltpu.roll(y, LANES - HEAD_DIM // 2, 1) * sa + pltpu.roll(y, HEAD_DIM // 2, 1) * sb
        return y

    outs = out_refs[:len(_INPROJ_OUT)]
    narrow = dict(zip(_INPROJ_NARROW, out_refs[len(_INPROJ_OUT):]))
    for (name, g0, n_groups, scaled, _), o_ref in zip(_INPROJ_OUT, outs):
        for i in range(n_groups):
            y = group(g0 + i)
            if scaled:
                y = y * scale
            o_ref[:, i * LANES:(i + 1) * LANES] = y.astype(o_ref.dtype)
            if name in narrow:
                narrow[name][:, i * LANES:(i + 1) * LANES] = y.astype(MXU_DTYPE)
    out_refs[-1][...] = jax.nn.sigmoid(group(_G_GATE))


def _inproj(h, g, w_all, tabs, tab_map, *, tm):
    n, d = h.shape
    row = lambda i: (i, 0)
    tab_spec = pl.BlockSpec((tm, LANES), tab_map)
    sizes = {o[0]: o[2] * LANES for o in _INPROJ_OUT}
    names = [o[0] for o in _INPROJ_OUT] + [k + "_bf" for k in _INPROJ_NARROW] + ["gates"]
    widths = [(o[2] * LANES, o[4] or MXU_DTYPE) for o in _INPROJ_OUT]
    widths += [(sizes[k], MXU_DTYPE) for k in _INPROJ_NARROW] + [(LANES, _F32)]
    outs = pl.pallas_call(
        _inproj_kernel,
        grid=(n // tm,),
        in_specs=[pl.BlockSpec((tm, d), row), _full((1, d)), _full(w_all.shape), tab_spec, tab_spec, tab_spec],
        out_specs=[pl.BlockSpec((tm, c), row) for c, _ in widths],
        out_shape=[jax.ShapeDtypeStruct((n, c), dt) for c, dt in widths],
        compiler_params=_params("parallel"),
        name="inproj",
    )(h, g, w_all, *tabs)
    return dict(zip(names, outs))


def _merge_kernel(h_ref, gpre_ref, gpost_ref, onsa_ref, osb_ref, omb_ref,
                  wbn_ref, wbs_ref, wbm_ref, wgate_ref, wout_ref, o_ref):
    d = h_ref.shape[1]
    h = h_ref[...]
    u = _rms(h, gpre_ref[...]).astype(MXU_DTYPE)
    branches = (_dot(onsa_ref[...], wbn_ref[...]), _dot(osb_ref[...], wbs_ref[...]), _dot(omb_ref[...], wbm_ref[...]))
    mix = None
    for c, br in enumerate(branches):
        term = jax.nn.sigmoid(_dot(u, wgate_ref[:, c * d:(c + 1) * d])) * br
        mix = term if mix is None else mix + term
    y = _dot(mix.astype(MXU_DTYPE), wout_ref[...])
    o_ref[...] = h + _rms(y, gpost_ref[...])


def _merge(h, g_pre, g_post, o_nsa, o_sb, o_mb, wbn, wbs, wbm, wgate, wout, *, tm):
    n, d = h.shape
    row = lambda i: (i, 0)
    return pl.pallas_call(
        _merge_kernel,
        grid=(n // tm,),
        in_specs=[pl.BlockSpec((tm, d), row), _full((1, d)), _full((1, d)),
                  pl.BlockSpec((tm, o_nsa.shape[1]), row), pl.BlockSpec((tm, o_sb.shape[1]), row),
                  pl.BlockSpec((tm, o_mb.shape[1]), row),
                  _full(wbn.shape), _full(wbs.shape), _full(wbm.shape), _full(wgate.shape), _full(wout.shape)],
        out_specs=pl.BlockSpec((tm, d), row),
        out_shape=jax.ShapeDtypeStruct((n, d), _F32),
        compiler_params=_params("parallel"),
        name="merge",
    )(h, g_pre, g_post, o_nsa, o_sb, o_mb, wbn, wbs, wbm, wgate, wout)


def _lane_lo(shape=(1, LANES)):
    return (lax.broadcasted_iota(jnp.int32, shape, len(shape) - 1) % LANES) < HEAD_DIM


def _softplus(z):
    return jnp.maximum(z, 0.0) + jnp.log1p(jnp.exp(-jnp.abs(z)))


def _cumsum_matrix(tk):
    j = lax.broadcasted_iota(jnp.int32, (tk, 2 * tk), 0)
    s = lax.broadcasted_iota(jnp.int32, (tk, 2 * tk), 1)
    return jnp.where((j > s) | (s >= tk), 1.0, 0.0).astype(MXU_DTYPE)


def _online_update1(state, z, v1, nt=False):
    m, acc = state
    m_new = jnp.maximum(m, jnp.max(z, axis=-1, keepdims=True))
    p = jnp.exp(z - m_new).astype(MXU_DTYPE)
    return m_new, jnp.exp(m - m_new) * acc + (_dot_nt(p, v1) if nt else _dot(p, v1))


def _online_init1(rows, width=LANES):
    return jnp.full((rows, 1), NEG, _F32), jnp.zeros((rows, width + LANES), _F32)


def _finish1(state):
    _, acc = state
    width = acc.shape[1] - LANES
    den = 1.0 / acc[:, width:]
    return acc[:, :width] * jnp.concatenate([den] * (width // LANES), axis=1)


def _with_ones(v, axis=1):
    shape = (v.shape[0], LANES) if axis == 1 else (LANES, v.shape[1])
    return jnp.concatenate([v, jnp.ones(shape, v.dtype)], axis=axis)


def _seed_state(m, l, acc):
    return m, jnp.concatenate([acc, jnp.broadcast_to(l, (l.shape[0], LANES))], axis=1)


def _topk_mask(val, k, axis):
    idx = lax.broadcasted_iota(jnp.int32, val.shape, axis).astype(_F32)
    sel = jnp.zeros(val.shape, _F32)
    for _ in range(k):
        m = jnp.max(val, axis=axis, keepdims=True)
        first = jnp.min(jnp.where(val == m, idx, 1e9), axis=axis, keepdims=True)
        hit = idx == first
        sel = jnp.where(hit, 1.0, sel)
        val = jnp.where(hit, -jnp.inf, val)
    return sel


def _stack_nsa_queries(q):
    return jnp.concatenate([q[:, hh * LANES:(hh + 1) * LANES] for hh in range(NSA_HEADS)], axis=0)


def _cmp_branch(qs, kc, vc, qpos, tq):
    nc = kc.shape[0]
    s = _dot_nt(qs, kc)
    cend = lax.broadcasted_iota(jnp.int32, (1, nc), 1) * CMP_STRIDE + (CMP_LEN - 1)
    vis = cend <= qpos
    sm = jnp.where(vis, s, NEG)
    m = jnp.max(sm, axis=-1, keepdims=True)
    e = jnp.where(vis, jnp.exp(sm - m), 0.0)
    l = jnp.sum(e, axis=-1, keepdims=True)
    p = e * (1.0 / jnp.where(l > 0.0, l, 1.0))
    o_cmp = _dot(p.astype(MXU_DTYPE), vc)
    psums = []
    for kv in range(NSA_KV_HEADS):
        blocks = [p[(kv * NSA_GROUP + g) * tq:(kv * NSA_GROUP + g + 1) * tq] for g in range(NSA_GROUP)]
        psums.append(functools.reduce(lambda a, b: a + b, blocks))
    return o_cmp, psums


def _nsa_select(imp, qp, n_blocks, n_top, axis):
    jblk = lax.broadcasted_iota(jnp.int32, imp.shape, axis)
    cur = qp // SEL_BLOCK
    forced = (jblk == 0) | (jblk == cur) | (jblk == cur - 1)
    valid = (jblk * SEL_BLOCK <= qp) & (jblk < n_blocks)
    val = jnp.where(valid, imp + jnp.where(forced, FORCE_BONUS, 0.0), NEG)
    return jnp.where(valid, _topk_mask(val, n_top, axis), 0.0) - 1.0


def _nsa_lhs(qs, biases):
    bias = jnp.concatenate([b for b in biases for _ in range(NSA_GROUP)], axis=0)
    return jnp.concatenate([qs, bias.astype(qs.dtype)], axis=1)


def _combine_nsa(gates, o_cmp, o_sel, o_win, tq):
    lo = _lane_lo()
    heads = []
    for hh in range(NSA_HEADS):
        r = slice(hh * tq, (hh + 1) * tq)
        o = (gates[:, 3 * hh:3 * hh + 1] * o_cmp[r] + gates[:, 3 * hh + 1:3 * hh + 2] * o_sel[r]
             + gates[:, 3 * hh + 2:3 * hh + 3] * o_win[r])
        if hh % 2 != hh // NSA_GROUP:
            o = pltpu.roll(o, HEAD_DIM, 1)
        heads.append(o)
    return jnp.concatenate([jnp.where(lo, heads[2 * i], heads[2 * i + 1]) for i in range(NSA_HEADS // 2)], axis=1)


def _moba_pick(s, own, axis):
    past = lax.broadcasted_iota(jnp.int32, s.shape, axis) < own
    return jnp.where(past, _topk_mask(jnp.where(past, s, NEG), MOBA_TOPK, axis), 0.0) - 1.0


def _sb_block(qh, kt, vt, cum, carry, keep):
    c, o = carry
    tk = kt.shape[0]
    z = _dot_nt(qh, kt)
    lk = -_softplus(z)
    if keep is not None:
        lk = jnp.where(keep, lk, 0.0)
    r = _dot_split(lk, cum)
    a = jnp.exp(lk + z + r[:, :tk] + c)
    if keep is not None:
        a = jnp.where(keep, a, 0.0)
    return c + r[:, tk:], o + _dot(a.astype(MXU_DTYPE), vt)


def _stage_chunks(src_ref, lead, n_chunks, dst_ref, c0):
    for r in range(CMP_STRIDE):
        x = src_ref[lead + (pl.ds(r, n_chunks, stride=CMP_STRIDE), slice(None))]
        dst_ref[pl.ds(c0, n_chunks), r * LANES:(r + 1) * LANES] = x


def _compress_finish(xk_ref, xv_ref, wk_ref, wv_ref, pe_ref, w2_ref, ok_ref, ov_ref):
    nc = xk_ref.shape[0]
    for x_ref, w_ref, t, o_ref in ((xk_ref, wk_ref, 0, ok_ref), (xv_ref, wv_ref, 1, ov_ref)):
        ab = _dot(x_ref[...].astype(MXU_DTYPE), w_ref[...])
        cst = _dot(pe_ref[t].astype(MXU_DTYPE), w_ref[...])
        for h in range(NSA_KV_HEADS):
            a = ab[:, h * 256:h * 256 + LANES]
            b = ab[:, h * 256 + LANES:(h + 1) * 256]
            c = cst[0:1, h * 256:h * 256 + LANES] + cst[1:2, h * 256 + LANES:(h + 1) * 256]
            hid = jax.nn.gelu(a + pltpu.roll(b, nc - 1, 0) + c)
            part = _dot(hid.astype(MXU_DTYPE), w2_ref[t, h])
            out = part if h == 0 else out + part
        o_ref[0] = out.astype(o_ref.dtype)


def _compress_prompt_kernel(rk_ref, rv_ref, wk_ref, wv_ref, pe_ref, w2_ref, ok_ref, ov_ref, xk_ref, xv_ref):
    n_chunks = rk_ref.shape[1] // CMP_STRIDE
    _stage_chunks(rk_ref, (0,), n_chunks, xk_ref, 0)
    _stage_chunks(rv_ref, (0,), n_chunks, xv_ref, 0)
    _compress_finish(xk_ref, xv_ref, wk_ref, wv_ref, pe_ref, w2_ref, ok_ref, ov_ref)


def _compress_sample_kernel(pt_ref, wk_ref, wv_ref, pe_ref, w2_ref, *refs, n_pages_step):
    del pt_ref
    pages = refs[:n_pages_step]
    ok_ref, ov_ref, xk_ref, xv_ref, rows_ref = refs[n_pages_step:]
    s = pl.program_id(1)
    per_page = pages[0].shape[3] // CMP_STRIDE
    for gp, page in enumerate(pages):
        x = page[0, 0]
        rows_ref[0] = x[:LANES].T
        rows_ref[1] = x[LANES:].T
        c0 = pl.multiple_of((s * n_pages_step + gp) * per_page, per_page)
        _stage_chunks(rows_ref, (0,), per_page, xk_ref, c0)
        _stage_chunks(rows_ref, (1,), per_page, xv_ref, c0)

    @pl.when(s == pl.num_programs(1) - 1)
    def _():
        _compress_finish(xk_ref, xv_ref, wk_ref, wv_ref, pe_ref, w2_ref, ok_ref, ov_ref)


def _compress_weights(pe, w1, w2):
    eye = jnp.eye(NSA_KV_HEADS, dtype=w1.dtype)
    big, pes = [], []
    for t in range(2):
        w1r = w1[t].reshape(2, CMP_STRIDE, HEAD_DIM, w1.shape[-1])
        wb = jnp.einsum("hH,ardn->rhdHan", eye, w1r)
        big.append(wb.reshape(CMP_STRIDE * NSA_KV_HEADS * HEAD_DIM, -1).astype(MXU_DTYPE))
        per = pe[t].reshape(2, CMP_STRIDE, 1, HEAD_DIM)
        rows = jnp.broadcast_to(per, (2, CMP_STRIDE, NSA_KV_HEADS, HEAD_DIM)).reshape(2, -1)
        pes.append(jnp.concatenate([rows, jnp.zeros((SUBLANES - 2, rows.shape[1]), rows.dtype)], axis=0))
    zero = jnp.zeros_like(w2)
    w2p = jnp.stack([jnp.concatenate([w2, zero], axis=-1), jnp.concatenate([zero, w2], axis=-1)], axis=1)
    return big[0], big[1], jnp.stack(pes), w2p.astype(MXU_DTYPE)


def _compress_out(b, nc):
    shape = jax.ShapeDtypeStruct((b, nc, LANES), MXU_DTYPE)
    return [shape, shape]


def _compress_prompt(nsa_rows, cw):
    b, t, _ = nsa_rows.shape
    nc = t // CMP_STRIDE
    wk, wv, pe2, w2p = cw
    out_spec = pl.BlockSpec((1, nc, LANES), lambda i: (i, 0, 0))
    return pl.pallas_call(
        _compress_prompt_kernel,
        grid=(b,),
        in_specs=[pl.BlockSpec((1, t, LANES), lambda i: (i, 0, 0)),
                  pl.BlockSpec((1, t, LANES), lambda i: (i, 0, 1)),
                  _full(wk.shape), _full(wv.shape), _full(pe2.shape), _full(w2p.shape)],
        out_specs=[out_spec, out_spec],
        out_shape=_compress_out(b, nc),
        scratch_shapes=[pltpu.VMEM((nc, CMP_STRIDE * LANES), _F32)] * 2,
        compiler_params=_params("arbitrary"),
        name="nsa_compress_prompt",
    )(nsa_rows, nsa_rows, wk, wv, pe2, w2p)


def _nsa_prompt_kernel(q_ref, g_ref, ck_ref, cv_ref, sk_ref, sv_ref, wk_ref, wv_ref, eblk_ref, cover_ref,
                       o_ref, *, tq, tk, n_top):
    qi = pl.program_id(1)
    t0 = qi * tq
    rows = NSA_HEADS * tq
    n_blocks = sk_ref.shape[1] // SEL_BLOCK
    qs = _stack_nsa_queries(q_ref[0])
    qpos_r = t0 + lax.broadcasted_iota(jnp.int32, (rows, 1), 0) % tq

    o_cmp, psums = _cmp_branch(qs, ck_ref[0], cv_ref[0], qpos_r, tq)
    qp_t = t0 + lax.broadcasted_iota(jnp.int32, (1, tq), 1)
    biases = []
    for psum in psums:
        imp_t = _dot_split(psum, cover_ref[...]).T
        biases.append(_nsa_select(imp_t, qp_t, n_blocks, n_top, 0).T)
    lhs = _nsa_lhs(qs, biases)

    kcol = lax.broadcasted_iota(jnp.int32, (1, tk), 1)

    def sel_tile(kt, state, causal):
        start = pl.multiple_of(kt * tk, tk)
        rhs = jnp.concatenate([sk_ref[0, pl.ds(start, tk), :], eblk_ref[pl.ds(start, tk), :]], axis=1)
        z = _dot_nt(lhs, rhs)
        if causal:
            z = jnp.where(start + kcol <= qpos_r, z, NEG)
        return _online_update1(state, z, _with_ones(sv_ref[0, pl.ds(start, tk), :]))

    n_full = t0 // tk
    state = lax.fori_loop(0, n_full, lambda kt, st: sel_tile(kt, st, False), _online_init1(rows))
    o_sel = _finish1(sel_tile(n_full, state, True))

    n_band = WINDOW + tq
    ws = pl.multiple_of(jnp.maximum(t0 - WINDOW, 0), tq)
    z = _dot_nt(qs, wk_ref[0, pl.ds(ws, n_band), :])
    dist = qpos_r - (ws + lax.broadcasted_iota(jnp.int32, (1, n_band), 1))
    z = jnp.where((dist >= 0) & (dist <= WINDOW), z, NEG)
    p = jnp.exp(z - jnp.max(z, axis=-1, keepdims=True))
    o_win = _dot(p.astype(MXU_DTYPE), wv_ref[0, pl.ds(ws, n_band), :]) * (1.0 / jnp.sum(p, axis=-1, keepdims=True))

    o_ref[0] = _combine_nsa(g_ref[0], o_cmp, o_sel, o_win, tq).astype(o_ref.dtype)


def _nsa_prompt(q_nsa, gates, cmp_k, cmp_v, nsa_bf, win_bf, eblk, cover, *, tq, tk):
    b, t, _ = q_nsa.shape
    nc = cmp_k.shape[1]
    n_top = min(SEL_TOPN, t // SEL_BLOCK)
    cmp_spec = pl.BlockSpec((1, nc, LANES), lambda i, qi: (i, 0, 0))
    col = lambda g: pl.BlockSpec((1, t, LANES), lambda i, qi: (i, 0, g))
    return pl.pallas_call(
        functools.partial(_nsa_prompt_kernel, tq=tq, tk=tk, n_top=n_top),
        grid=(b, t // tq),
        in_specs=[pl.BlockSpec((1, tq, NSA_HEADS * LANES), lambda i, qi: (i, qi, 0)),
                  pl.BlockSpec((1, tq, LANES), lambda i, qi: (i, qi, 0)),
                  cmp_spec, cmp_spec, col(2), col(3), col(0), col(1),
                  _full(eblk.shape), _full(cover.shape)],
        out_specs=pl.BlockSpec((1, tq, NSA_Q_COLS), lambda i, qi: (i, qi, 0)),
        out_shape=jax.ShapeDtypeStruct((b, t, NSA_Q_COLS), MXU_DTYPE),
        compiler_params=_params("parallel", "arbitrary"),
        name="nsa_prompt",
    )(q_nsa, gates, cmp_k, cmp_v, nsa_bf, nsa_bf, win_bf, win_bf, eblk, cover)


def _sb_prompt_kernel(q_ref, k_ref, v_ref, o_ref, *, tq):
    qi = pl.program_id(2)
    lo = _lane_lo()
    q = q_ref[0]
    zero_q = jnp.zeros_like(q)
    qh = (jnp.where(lo, q, zero_q), jnp.where(lo, zero_q, q))
    cum = _cumsum_matrix(tq)
    row = lax.broadcasted_iota(jnp.int32, (tq, tq), 0)
    col = lax.broadcasted_iota(jnp.int32, (tq, tq), 1)

    def blocks(kb, carry, keep):
        start = pl.multiple_of(kb * tq, tq)
        kt, vt = k_ref[0, pl.ds(start, tq), :], v_ref[0, pl.ds(start, tq), :]
        return tuple(_sb_block(qh[h], kt, vt, cum, carry[h], keep) for h in range(2))

    def alive(carry):
        return (jnp.max(jnp.maximum(carry[0][0], carry[1][0])) > SB_DEAD).astype(jnp.int32)

    zero = jnp.zeros((tq, LANES), _F32)
    carry = blocks(qi, ((zero, zero), (zero, zero)), col < row)

    def body(st):
        i, _, carry = st
        carry = blocks(qi - 1 - i, carry, None)
        return i + 1, alive(carry), carry

    _, _, carry = lax.while_loop(lambda st: (st[0] < qi) & (st[1] > 0), body, (0, alive(carry), carry))
    o_ref[0] = jnp.where(lo, carry[0][1], carry[1][1]).astype(o_ref.dtype)


def _sb_prompt(q_sb, sb_bf, *, tq):
    b, t, _ = q_sb.shape
    n_pair = SB_HEADS // 2
    return pl.pallas_call(
        functools.partial(_sb_prompt_kernel, tq=tq),
        grid=(b, n_pair, t // tq),
        in_specs=[pl.BlockSpec((1, tq, LANES), lambda i, p, qi: (i, qi, p)),
                  pl.BlockSpec((1, t, LANES), lambda i, p, qi: (i, 0, p)),
                  pl.BlockSpec((1, t, LANES), lambda i, p, qi: (i, 0, n_pair + p))],
        out_specs=pl.BlockSpec((1, tq, LANES), lambda i, p, qi: (i, qi, p)),
        out_shape=jax.ShapeDtypeStruct((b, t, SB_HEADS * HEAD_DIM), MXU_DTYPE),
        compiler_params=_params("parallel", "parallel", "arbitrary"),
        name="sb_prompt",
    )(q_sb, sb_bf, sb_bf)


def _kmean_kernel(k_ref, o_ref):
    x = k_ref[0]
    nb = x.shape[0] // MOBA_BLOCK
    o_ref[0] = jnp.mean(x.reshape(nb, MOBA_BLOCK, x.shape[1]), axis=1)


def _kmean_prompt(mb_rows):
    b, t, _ = mb_rows.shape
    nb = SUBLANES
    return pl.pallas_call(
        _kmean_kernel,
        grid=(b, t // (nb * MOBA_BLOCK)),
        in_specs=[pl.BlockSpec((1, nb * MOBA_BLOCK, 2 * LANES), lambda i, j: (i, j, 0))],
        out_specs=pl.BlockSpec((1, nb, 2 * LANES), lambda i, j: (i, j, 0)),
        out_shape=jax.ShapeDtypeStruct((b, t // MOBA_BLOCK, 2 * LANES), _F32),
        compiler_params=_params("parallel", "parallel"),
        name="moba_kmean_prompt",
    )(mb_rows)


def _moba_prompt_kernel(q_ref, km_ref, k_ref, v_ref, eblk_ref, o_ref, *, tq, tk):
    qi = pl.program_id(2)
    lo = _lane_lo()
    q = q_ref[0]
    scale = HEAD_DIM ** -0.5
    row = lax.broadcasted_iota(jnp.int32, (tq, tq), 0)
    col = lax.broadcasted_iota(jnp.int32, (tq, tq), 1)
    own0 = pl.multiple_of(qi * tq, tq)
    k_own, v_own = k_ref[0, pl.ds(own0, tq), :], _with_ones(v_ref[0, pl.ds(own0, tq), :])
    lhs, states = [], []
    for hh in range(2):
        qf = jnp.where(lo if hh == 0 else jnp.logical_not(lo), q, 0.0)
        gate = lax.dot_general(qf, km_ref[0], (((1,), (1,)), ((), ())), preferred_element_type=_F32,
                               precision=lax.Precision.HIGHEST)
        bias = _moba_pick(gate.T, qi, 0).T
        qh = (qf * scale).astype(MXU_DTYPE)
        lhs.append(jnp.concatenate([qh, bias.astype(MXU_DTYPE)], axis=1))
        z = jnp.where(col <= row, _dot_nt(qh, k_own), NEG)
        states.append(_online_update1(_online_init1(tq), z, v_own))

    def tile(j, states):
        start = pl.multiple_of(j * tk, tk)
        rhs = jnp.concatenate([k_ref[0, pl.ds(start, tk), :], eblk_ref[pl.ds(start, tk), :]], axis=1)
        v1 = _with_ones(v_ref[0, pl.ds(start, tk), :])
        return tuple(_online_update1(st, _dot_nt(l, rhs), v1) for l, st in zip(lhs, states))

    states = lax.fori_loop(0, (qi * tq + tk - 1) // tk, tile, tuple(states))
    o_ref[0] = jnp.where(lo, _finish1(states[0]), _finish1(states[1])).astype(o_ref.dtype)


def _moba_prompt(q_mb, kmean, mb_bf, eblk, *, tk):
    b, t, _ = q_mb.shape
    tq = MOBA_BLOCK
    n_pair = MOBA_HEADS // 2
    return pl.pallas_call(
        functools.partial(_moba_prompt_kernel, tq=tq, tk=tk),
        grid=(b, n_pair, t // tq),
        in_specs=[pl.BlockSpec((1, tq, LANES), lambda i, p, qi: (i, qi, p)),
                  pl.BlockSpec((1, LANES, LANES), lambda i, p, qi: (i, 0, p)),
                  pl.BlockSpec((1, t, LANES), lambda i, p, qi: (i, 0, p)),
                  pl.BlockSpec((1, t, LANES), lambda i, p, qi: (i, 0, n_pair + p)),
                  _full(eblk.shape)],
        out_specs=pl.BlockSpec((1, tq, LANES), lambda i, p, qi: (i, qi, p)),
        out_shape=jax.ShapeDtypeStruct((b, t, MOBA_HEADS * HEAD_DIM), MXU_DTYPE),
        compiler_params=_params("parallel", "parallel", "arbitrary"),
        name="moba_prompt",
    )(q_mb, kmean, mb_bf, mb_bf, eblk)


def _row_scores(q, keys, n):
    return [jnp.sum(q * keys[j:j + 1, :], axis=-1, keepdims=True) for j in range(n)]


def _new_key_softmax(q, keys, vals, n, visible):
    zs = [jnp.where(visible(j), z, NEG) for j, z in enumerate(_row_scores(q, keys, n))]
    m = functools.reduce(jnp.maximum, zs)
    l = jnp.zeros_like(m)
    acc = jnp.zeros((q.shape[0], vals.shape[1]), _F32)
    for j, z in enumerate(zs):
        p = jnp.exp(z - m)
        l = l + p
        acc = acc + p * vals[j:j + 1, :]
    return m, l, acc


def _stack_heads(q, n_heads):
    head = lax.broadcasted_iota(jnp.int32, (1, q.shape[1]), 1) // HEAD_DIM
    return jnp.concatenate([jnp.where(head == h, q, jnp.zeros_like(q)) for h in range(n_heads)], axis=0)


def _unstack_heads(o, n_heads):
    rows = o.shape[0] // n_heads
    head = lax.broadcasted_iota(jnp.int32, (1, o.shape[1]), 1) // HEAD_DIM
    out = jnp.zeros((rows, o.shape[1]), _F32)
    for h in range(n_heads):
        out = jnp.where(head == h, o[h * rows:(h + 1) * rows], out)
    return out


def _side_by_side(pages, f0, f1):
    return jnp.concatenate([p[0, 0, f0:f1, :] for p in pages], axis=1).astype(MXU_DTYPE)


def _key_block_onehot(n_keys, first_key, block):
    key_blk = (first_key + lax.broadcasted_iota(jnp.int32, (LANES, n_keys), 1)) // block
    return jnp.where(lax.broadcasted_iota(jnp.int32, (LANES, n_keys), 0) == key_blk, -NEG, 0.0).astype(MXU_DTYPE)


def _state_refs_store(m_ref, acc_ref, state):
    m_ref[...] = jnp.broadcast_to(state[0], m_ref.shape)
    acc_ref[...] = state[1]


def _page_specs(layer, n, block, row_block, page_of):
    def spec(gp):
        return pl.BlockSpec(block, lambda b, s, pt: (layer, pt[b, page_of(s, gp)], row_block, 0))
    return [spec(gp) for gp in range(n)]


def _sample_call(kernel_fn, name, page_table, n_steps, pre_args, pre_specs, page_arrays, page_specs, out_block,
                 out_shape, scratch):
    grid_spec = pltpu.PrefetchScalarGridSpec(
        num_scalar_prefetch=1,
        grid=(page_table.shape[0], n_steps),
        in_specs=pre_specs + page_specs,
        out_specs=out_block,
        scratch_shapes=scratch,
    )
    return pl.pallas_call(kernel_fn, grid_spec=grid_spec, out_shape=out_shape,
                          compiler_params=_params("arbitrary", "arbitrary"), name=name,
                          )(page_table, *pre_args, *page_arrays)


def _seq_block(shape):
    zeros = (0,) * (len(shape) - 1)
    return pl.BlockSpec((1,) + tuple(shape[1:]), lambda b, s, pt: (b,) + zeros)


def _const_block(shape):
    zeros = (0,) * len(shape)
    return pl.BlockSpec(tuple(shape), lambda b, s, pt: zeros)


def _compress_sample(cache, layer, page_table, cw):
    b, n_pages = page_table.shape
    page = cache.shape[3]
    nc = n_pages * page // CMP_STRIDE
    g = PAGES_PER_STEP
    wk, wv, pe2, w2p = cw
    out_spec = pl.BlockSpec((1, nc, LANES), lambda i, s, pt: (i, 0, 0))
    return _sample_call(
        functools.partial(_compress_sample_kernel, n_pages_step=g), "nsa_compress_sample", page_table, n_pages // g,
        [wk, wv, pe2, w2p], [_const_block(wk.shape), _const_block(wv.shape), _const_block(pe2.shape),
                             _const_block(w2p.shape)],
        [cache] * g, _page_specs(layer, g, (1, 1, 2 * LANES, page), 0, lambda s, gp: s * g + gp),
        [out_spec, out_spec], _compress_out(b, nc),
        [pltpu.VMEM((nc, CMP_STRIDE * LANES), _F32)] * 2 + [pltpu.VMEM((2, page, LANES), _F32)])


def _nsa_sample_kernel(pt_ref, q_ref, g_ref, ck_ref, cv_ref, new_ref, neww_ref, win_ref, cover_ref, *refs,
                       n_pages_step, n_new, past_len, n_top):
    del pt_ref
    pages = refs[:n_pages_step]
    o_ref, m_ref, acc_ref, bias_ref, ocmp_ref, owin_ref = refs[n_pages_step:]
    s = pl.program_id(1)
    tq = SAMPLE_ROWS
    rows = NSA_HEADS * tq
    qs = _stack_nsa_queries(q_ref[0])
    qi_r = lax.broadcasted_iota(jnp.int32, (rows, 1), 0) % tq
    qpos_r = past_len + qi_r

    @pl.when(s == 0)
    def _():
        qf = qs.astype(_F32)
        o_cmp, psums = _cmp_branch(qs, ck_ref[0], cv_ref[0], qpos_r, tq)
        ocmp_ref[...] = o_cmp
        qp = past_len + lax.broadcasted_iota(jnp.int32, (tq, 1), 0)
        biases = [_nsa_select(_dot_split(psum, cover_ref[...]), qp, past_len // SEL_BLOCK, n_top - 1, 1)
                  for psum in psums]
        bias_ref[...] = jnp.concatenate([b for b in biases for _ in range(NSA_GROUP)], axis=0)
        visible = lambda j: j <= qi_r
        new = new_ref[0].astype(_F32)
        state = _seed_state(*_new_key_softmax(qf, new[:, 2 * LANES:3 * LANES], new[:, 3 * LANES:], n_new, visible))
        _state_refs_store(m_ref, acc_ref, state)
        wnew = neww_ref[0].astype(_F32)
        wstate = _seed_state(*_new_key_softmax(qf, wnew[:, :LANES], wnew[:, LANES:], n_new, visible))
        w = win_ref[0, 0]
        n_win = w.shape[1]
        dist = qpos_r - (past_len - n_win + lax.broadcasted_iota(jnp.int32, (1, n_win), 1))
        z = jnp.where((dist >= 0) & (dist <= WINDOW), _dot(qs, w[:LANES].astype(MXU_DTYPE)), NEG)
        owin_ref[...] = _finish1(_online_update1(wstate, z, _with_ones(w[LANES:].astype(MXU_DTYPE), 0), nt=True))

    n_keys = n_pages_step * pages[0].shape[3]
    lhs = jnp.concatenate([qs, bias_ref[...].astype(MXU_DTYPE)], axis=1)
    rhs = jnp.concatenate([_side_by_side(pages, 0, LANES), _key_block_onehot(n_keys, s * n_keys, SEL_BLOCK)], axis=0)
    state = (m_ref[:, 0:1], acc_ref[...])
    state = _online_update1(state, _dot(lhs, rhs), _with_ones(_side_by_side(pages, LANES, 2 * LANES), 0), nt=True)
    _state_refs_store(m_ref, acc_ref, state)

    @pl.when(s == pl.num_programs(1) - 1)
    def _():
        o_sel = _finish1((m_ref[:, 0:1], acc_ref[...]))
        o_ref[0] = _combine_nsa(g_ref[0], ocmp_ref[...], o_sel, owin_ref[...], tq).astype(o_ref.dtype)


def _nsa_sample(q, gates, cmp_k, cmp_v, new_bf, new_win_bf, cache, cache_win, layer, page_table, cover, n_new):
    b, n_pages = page_table.shape
    page = cache.shape[3]
    past_len = n_pages * page
    g = PAGES_PER_STEP
    rows = NSA_HEADS * SAMPLE_ROWS
    n_top = min(SEL_TOPN, past_len // SEL_BLOCK + 1)
    win_spec = pl.BlockSpec((1, 1) + cache_win.shape[2:], lambda i, s, pt: (layer, i, 0, 0))
    out_shape = (b, SAMPLE_ROWS, NSA_Q_COLS)
    vmem = lambda w: pltpu.VMEM((rows, w), _F32)
    return _sample_call(
        functools.partial(_nsa_sample_kernel, n_pages_step=g, n_new=n_new, past_len=past_len, n_top=n_top),
        "nsa_sample", page_table, n_pages // g,
        [q, gates, cmp_k, cmp_v, new_bf, new_win_bf, cache_win, cover],
        [_seq_block(q.shape), _seq_block(gates.shape), _seq_block(cmp_k.shape), _seq_block(cmp_v.shape),
         _seq_block(new_bf.shape), _seq_block(new_win_bf.shape), win_spec, _const_block(cover.shape)],
        [cache] * g, _page_specs(layer, g, (1, 1, 2 * LANES, page), 1, lambda s, gp: s * g + gp),
        _seq_block(out_shape), jax.ShapeDtypeStruct(out_shape, MXU_DTYPE),
        [vmem(LANES), vmem(2 * LANES), vmem(LANES), vmem(LANES), vmem(LANES)])


def _sb_sample_kernel(pt_ref, q_ref, new_ref, *refs, n_pages_step, n_new):
    del pt_ref
    pages = refs[:n_pages_step]
    o_ref, c_ref, acc_ref = refs[n_pages_step:]
    s = pl.program_id(1)
    width = SB_HEADS * HEAD_DIM
    qs = _stack_heads(q_ref[0], SB_HEADS)
    rows = qs.shape[0]
    qi_r = lax.broadcasted_iota(jnp.int32, (rows, 1), 0) % SAMPLE_ROWS

    @pl.when(s == 0)
    def _():
        qf = qs.astype(_F32)
        new = new_ref[0].astype(_F32)
        zs = _row_scores(qf, new[:, :width], n_new)
        c = jnp.zeros((rows, 1), _F32)
        o = jnp.zeros((rows, width), _F32)
        for j in reversed(range(n_new)):
            seen = j < qi_r
            lk = jnp.where(seen, -_softplus(zs[j]), 0.0)
            a = jnp.where(seen, jnp.exp(lk + zs[j] + c), 0.0)
            o = o + a * new[j:j + 1, width:]
            c = c + lk
        c_ref[...] = jnp.broadcast_to(c, c_ref.shape)
        acc_ref[...] = o

    page_keys = pages[0].shape[3]
    cum = _cumsum_matrix(page_keys)
    z = _dot(qs, _side_by_side(pages, 0, width))
    lk = -_softplus(z)
    c = c_ref[...]
    later = []
    for gp in range(n_pages_step):
        r = _dot_split(lk[:, gp * page_keys:(gp + 1) * page_keys], cum)
        later.append(r[:, :page_keys] + c)
        c = c + r[:, page_keys:]
    a = jnp.exp(lk + z + jnp.concatenate(later, axis=1))
    acc_ref[...] += _dot_nt(a.astype(MXU_DTYPE), _side_by_side(pages, width, 2 * width))
    c_ref[...] = c

    @pl.when(s == pl.num_programs(1) - 1)
    def _():
        o_ref[0] = _unstack_heads(acc_ref[...], SB_HEADS).astype(o_ref.dtype)


def _sb_sample(q, new_bf, cache, layer, page_table, n_new):
    b, n_pages = page_table.shape
    page = cache.shape[3]
    g = PAGES_PER_STEP
    rows = SB_HEADS * SAMPLE_ROWS
    width = SB_HEADS * HEAD_DIM
    return _sample_call(
        functools.partial(_sb_sample_kernel, n_pages_step=g, n_new=n_new),
        "sb_sample", page_table, n_pages // g,
        [q, new_bf], [_seq_block(q.shape), _seq_block(new_bf.shape)],
        [cache] * g, _page_specs(layer, g, (1, 1, 2 * width, page), 0, lambda s, gp: n_pages - 1 - (s * g + gp)),
        _seq_block(q.shape), jax.ShapeDtypeStruct(q.shape, MXU_DTYPE),
        [pltpu.VMEM((rows, LANES), _F32), pltpu.VMEM((rows, width), _F32)])


def _moba_sample_kernel(pt_ref, q_ref, new_ref, *refs, n_pages_step, n_new, past_len):
    del pt_ref
    g = n_pages_step
    kpages, pages = refs[:g], refs[g:2 * g]
    o_ref, m_ref, acc_ref, bias_ref, km_ref = refs[2 * g:]
    s = pl.program_id(1)
    n_half = pl.num_programs(1) // 2
    width = MOBA_HEADS * HEAD_DIM
    page_keys = pages[0].shape[3]
    n_keys = g * page_keys
    qf = _stack_heads(q_ref[0], MOBA_HEADS)
    rows = qf.shape[0]
    qs = (qf * HEAD_DIM ** -0.5).astype(MXU_DTYPE)
    qi_r = lax.broadcasted_iota(jnp.int32, (rows, 1), 0) % SAMPLE_ROWS
    lane = lax.broadcasted_iota(jnp.int32, (1, LANES), 1)

    @pl.when(s == 0)
    def _():
        km_ref[...] = jnp.zeros_like(km_ref)

    @pl.when(s < n_half)
    def _():
        km = km_ref[...]
        for gp, page in enumerate(kpages):
            blk = (s * g + gp) * page_keys // MOBA_BLOCK
            part = jnp.sum(page[0, 0], axis=1, keepdims=True) * (1.0 / MOBA_BLOCK)
            km = km + jnp.where(lane == blk, part, 0.0)
        km_ref[...] = km

    @pl.when(s == n_half)
    def _():
        gate = lax.dot_general(qf, km_ref[...], (((1,), (0,)), ((), ())), preferred_element_type=_F32,
                               precision=lax.Precision.HIGHEST)
        bias_ref[...] = _moba_pick(gate, past_len // MOBA_BLOCK, 1)
        new = new_ref[0].astype(_F32)
        m, l, acc = _new_key_softmax(qs.astype(_F32), new[:, :width], new[:, width:], n_new, lambda j: j <= qi_r)
        _state_refs_store(m_ref, acc_ref, _seed_state(m, l, acc))

    @pl.when(s >= n_half)
    def _():
        lhs = jnp.concatenate([qs, bias_ref[...].astype(MXU_DTYPE)], axis=1)
        rhs = jnp.concatenate([_side_by_side(pages, 0, width),
                               _key_block_onehot(n_keys, (s - n_half) * n_keys, MOBA_BLOCK)], axis=0)
        state = (m_ref[:, 0:1], acc_ref[...])
        v1 = _with_ones(_side_by_side(pages, width, 2 * width), 0)
        _state_refs_store(m_ref, acc_ref, _online_update1(state, _dot(lhs, rhs), v1, nt=True))

    @pl.when(s == pl.num_programs(1) - 1)
    def _():
        o_ref[0] = _unstack_heads(_finish1((m_ref[:, 0:1], acc_ref[...])), MOBA_HEADS).astype(o_ref.dtype)


def _moba_sample(q, new_bf, cache, layer, page_table, n_new):
    b, n_pages = page_table.shape
    page = cache.shape[3]
    g = PAGES_PER_STEP
    n_half = n_pages // g
    rows = MOBA_HEADS * SAMPLE_ROWS
    width = MOBA_HEADS * HEAD_DIM
    out_shape = (b, SAMPLE_ROWS, width)
    first = lambda s, gp: jnp.minimum(s, n_half - 1) * g + gp
    second = lambda s, gp: jnp.maximum(s - n_half, 0) * g + gp
    return _sample_call(
        functools.partial(_moba_sample_kernel, n_pages_step=g, n_new=n_new, past_len=n_pages * page),
        "moba_sample", page_table, 2 * n_half,
        [q, new_bf], [_seq_block(q.shape), _seq_block(new_bf.shape)],
        [cache] * (2 * g),
        _page_specs(layer, g, (1, 1, width, page), 0, first) + _page_specs(layer, g, (1, 1, 2 * width, page), 0, second),
        _seq_block(out_shape), jax.ShapeDtypeStruct(out_shape, MXU_DTYPE),
        [pltpu.VMEM((rows, LANES), _F32), pltpu.VMEM((rows, width + LANES), _F32), pltpu.VMEM((rows, LANES), _F32),
         pltpu.VMEM((width, LANES), _F32)])


def _rope_tables(pos):
    half = HEAD_DIM // 2
    inv = ROPE_THETA ** (-jnp.arange(half, dtype=_F32) / half)
    ang = pos.astype(_F32)[:, None] * inv[None, :]
    cos, sin = jnp.cos(ang), jnp.sin(ang)
    zero = jnp.zeros_like(sin)
    reps = LANES // HEAD_DIM
    return (jnp.tile(cos, (1, 2 * reps)), jnp.tile(jnp.concatenate([-sin, zero], axis=1), (1, reps)),
            jnp.tile(jnp.concatenate([zero, sin], axis=1), (1, reps)))


def _pack_w_in(w_in):
    d = w_in.shape[0]
    o1 = NSA_Q_COLS
    o2 = o1 + NSA_KV_COLS
    o3 = o2 + NSA_G_COLS
    o4 = o3 + SB_COLS
    q, kv, g, sb, mb = w_in[:, :o1], w_in[:, o1:o2], w_in[:, o2:o3], w_in[:, o3:o4], w_in[:, o4:]
    zero = jnp.zeros((d, HEAD_DIM), w_in.dtype)
    q_groups = []
    for hh in range(NSA_HEADS):
        w = q[:, hh * HEAD_DIM:(hh + 1) * HEAD_DIM]
        q_groups += [w, zero] if hh // NSA_GROUP == 0 else [zero, w]
    gate = jnp.pad(g, ((0, 0), (0, LANES - NSA_G_COLS)))
    return jnp.concatenate(q_groups + [kv, sb, mb, gate], axis=1).astype(MXU_DTYPE)


def _block_tables(t):
    key = jnp.arange(t)[:, None]
    j = jnp.arange(LANES)[None, :]
    eblk = jnp.where(key // SEL_BLOCK == j, -NEG, 0.0).astype(MXU_DTYPE)
    eblk_mb = jnp.where(key // MOBA_BLOCK == j, -NEG, 0.0).astype(MXU_DTYPE)
    i = jnp.arange(t // CMP_STRIDE)[:, None]
    ratio = SEL_BLOCK // CMP_STRIDE
    cover = ((i <= ratio * j + ratio - 1) & (i >= ratio * j - (CMP_LEN // CMP_STRIDE - 1))).astype(MXU_DTYPE)
    return eblk, eblk_mb, cover


def _layer_weights(l, norm_g, ffn_w_gate, ffn_w_up, ffn_w_down, w_in, nsa_cmp_pe, nsa_cmp_w1, nsa_cmp_w2,
                   w_branch_nsa, w_branch_sb, w_branch_moba, w_merge_gate, w_out, w_ple_proj, w_ple_gate):
    c = lambda w: w.astype(MXU_DTYPE)
    return dict(
        g=[norm_g[l, i][None, :] for i in range(norm_g.shape[1])],
        ffn=[(c(ffn_w_gate[l, i]), c(ffn_w_up[l, i]), c(ffn_w_down[l, i])) for i in range(2)],
        w_all=_pack_w_in(w_in[l]),
        cw=_compress_weights(nsa_cmp_pe[l], nsa_cmp_w1[l], nsa_cmp_w2[l]),
        merge=(c(w_branch_nsa[l]), c(w_branch_sb[l]), c(w_branch_moba[l]), c(w_merge_gate[l]), c(w_out[l])),
        ple=(c(w_ple_gate[l]), c(w_ple_proj[l])),
    )


def _token_tile(n, cap):
    tm = cap
    while n % tm:
        tm //= 2
    return tm


def _ff_tile(f, cap=512):
    best = LANES
    for k in range(1, f // LANES + 1):
        if f % (k * LANES) == 0 and k * LANES <= cap:
            best = k * LANES
    return best


def _prompt_layer(h, p_l, lw, tabs, consts, b, t):
    n, d = h.shape
    tm = _token_tile(t, 512)
    tf = _ff_tile(lw["ffn"][0][0].shape[1])
    tk = 4 * LANES
    g = lw["g"]
    h = _ffn(h, g[0], g[1], *lw["ffn"][0], tm=tm, tf=tf)
    per_seq = t // tm
    pr = _inproj(h, g[2], lw["w_all"], tabs, lambda i: (i % per_seq, 0), tm=tm)
    pr = {k: v.reshape(b, t, v.shape[1]) for k, v in pr.items()}
    eblk, eblk_mb, cover = consts
    cmp_k, cmp_v = _compress_prompt(pr["nsa"], lw["cw"])
    o_nsa = _nsa_prompt(pr["q_nsa"], pr["gates"], cmp_k, cmp_v, pr["nsa_bf"], pr["win_bf"], eblk, cover,
                        tq=LANES, tk=tk)
    o_sb = _sb_prompt(pr["q_sb"], pr["sb_bf"], tq=LANES)
    kmean = _kmean_prompt(pr["mb"])
    kmean = jnp.pad(kmean, ((0, 0), (0, LANES - kmean.shape[1]), (0, 0)))
    o_mb = _moba_prompt(pr["q_mb"], kmean, pr["mb_bf"], eblk_mb, tk=tk)
    flat = lambda o: o.reshape(n, o.shape[2])
    h = _merge(h, g[2], g[3], flat(o_nsa), flat(o_sb), flat(o_mb), *lw["merge"], tm=tm)
    h = _ffn(h, g[4], g[5], *lw["ffn"][1], ple=(p_l, *lw["ple"], g[6]), tm=tm, tf=tf)
    return h, (pr["nsa"], pr["sb"], pr["mb"], pr["win"])


def _sample_layer(h, p_l, lw, tabs, cover, caches, layer, page_table, n_new):
    n, d = h.shape
    b = page_table.shape[0]
    tm = _token_tile(n, 512)
    tf = _ff_tile(lw["ffn"][0][0].shape[1])
    g = lw["g"]
    cache_nsa, cache_sb, cache_mb, cache_win = caches
    h = _ffn(h, g[0], g[1], *lw["ffn"][0], tm=tm, tf=tf)
    pr = _inproj(h, g[2], lw["w_all"], tabs, lambda i: (i, 0), tm=tm)
    pr = {k: v.reshape(b, SAMPLE_ROWS, v.shape[1]) for k, v in pr.items()}
    cmp_k, cmp_v = _compress_sample(cache_nsa, layer, page_table, lw["cw"])
    o_nsa = _nsa_sample(pr["q_nsa"], pr["gates"], cmp_k, cmp_v, pr["nsa_bf"], pr["win_bf"], cache_nsa, cache_win,
                        layer, page_table, cover, n_new)
    o_sb = _sb_sample(pr["q_sb"], pr["sb_bf"], cache_sb, layer, page_table, n_new)
    o_mb = _moba_sample(pr["q_mb"], pr["mb_bf"], cache_mb, layer, page_table, n_new)
    flat = lambda o: o.reshape(n, o.shape[2])
    h = _merge(h, g[2], g[3], flat(o_nsa), flat(o_sb), flat(o_mb), *lw["merge"], tm=tm)
    h = _ffn(h, g[4], g[5], *lw["ffn"][1], ple=(p_l, *lw["ple"], g[6]), tm=tm, tf=tf)
    return h, tuple(pr[k][:, :n_new] for k in ("nsa", "sb", "mb", "win"))


def kernel(x_prompt, x_sample, p_prompt, p_sample, cache_nsa, cache_sb, cache_moba, cache_win, page_table,
           norm_g, ffn_w_gate, ffn_w_up, ffn_w_down, w_in, nsa_cmp_pe, nsa_cmp_w1, nsa_cmp_w2,
           w_branch_nsa, w_branch_sb, w_branch_moba, w_merge_gate, w_out, w_ple_proj, w_ple_gate):
    depth = norm_g.shape[0]
    b, t, d = x_prompt.shape
    bs, ts, _ = x_sample.shape
    n_pages = page_table.shape[1]
    page = cache_nsa.shape[2]
    past_len = n_pages * page
    assert ts <= SAMPLE_ROWS and t % (4 * LANES) == 0 and t >= WINDOW + LANES
    assert t // SEL_BLOCK <= LANES and past_len // SEL_BLOCK <= LANES
    assert n_pages % PAGES_PER_STEP == 0 and MOBA_BLOCK % page == 0 and cache_win.shape[2] == WINDOW
    weights = (norm_g, ffn_w_gate, ffn_w_up, ffn_w_down, w_in, nsa_cmp_pe, nsa_cmp_w1, nsa_cmp_w2,
               w_branch_nsa, w_branch_sb, w_branch_moba, w_merge_gate, w_out, w_ple_proj, w_ple_gate)
    tabs_p = _rope_tables(jnp.arange(t, dtype=jnp.int32))
    tabs_s = _rope_tables(past_len + jnp.arange(bs * SAMPLE_ROWS, dtype=jnp.int32) % SAMPLE_ROWS)
    consts = _block_tables(t)
    cover_s = _block_tables(past_len)[2]
    pad_rows = lambda x: jnp.pad(x, ((0, 0), (0, SAMPLE_ROWS - ts), (0, 0))).reshape(bs * SAMPLE_ROWS, -1)
    feature_major = lambda c: jnp.transpose(c, (0, 1, 3, 4, 5, 2)).reshape(c.shape[0], c.shape[1], -1, c.shape[2])
    caches = tuple(feature_major(c) for c in (cache_nsa, cache_sb, cache_moba, cache_win))
    h_p = x_prompt.reshape(b * t, d)
    h_s = pad_rows(x_sample)
    st_p, st_s = [], []
    for l in range(depth):
        lw = _layer_weights(l, *weights)
        h_p, rows = _prompt_layer(h_p, p_prompt[l].reshape(b * t, -1), lw, tabs_p, consts, b, t)
        st_p.append(rows)
        h_s, rows = _sample_layer(h_s, pad_rows(p_sample[l]), lw, tabs_s, cover_s, caches, l, page_table, ts)
        win_fm = jnp.concatenate([caches[3][l][:, :, ts:], jnp.swapaxes(rows[3], 1, 2)], axis=2)
        st_s.append(rows[:3] + (jnp.swapaxes(win_fm, 1, 2),))
    keep = min(WINDOW, t)
    heads = ((4, NSA_KV_HEADS), (2, SB_HEADS), (2, MOBA_HEADS))
    out = [h_p.reshape(b, t, d), h_s.reshape(bs, SAMPLE_ROWS, d)[:, :ts]]
    for i, (parts, nh) in enumerate(heads):
        out.append(jnp.stack([s[i] for s in st_p]).reshape(depth, b, t, parts, nh, HEAD_DIM))
        out.append(jnp.stack([s[i] for s in st_s]).reshape(depth, bs, ts, parts, nh, HEAD_DIM))
    out.append(jnp.stack([s[3][:, t - keep:] for s in st_p]).reshape(depth, b, keep, 2, NSA_KV_HEADS, HEAD_DIM))
    out.append(jnp.stack([s[3] for s in st_s]).reshape(depth, bs, -1, 2, NSA_KV_HEADS, HEAD_DIM))
    return tuple(out)
```

```python
import functools

import jax
import jax.numpy as jnp
from jax import lax
from jax.experimental import pallas as pl
from jax.experimental.pallas import tpu as pltpu

HEAD_DIM = 64
NSA_HEADS = 8
NSA_KV_HEADS = 2
NSA_GROUP = NSA_HEADS // NSA_KV_HEADS
SB_HEADS = 4
MOBA_HEADS = 4
CMP_LEN = 32
CMP_STRIDE = 16
SEL_BLOCK = 64
SEL_TOPN = 16
WINDOW = 512
MOBA_BLOCK = 256
MOBA_TOPK = 3
ROPE_THETA = 10000.0
RMS_EPS = 1e-6
NEG = -1e30
FORCE_BONUS = 1e4
SB_DEAD = -104.0

LANES = 128
SUBLANES = 8
MXU_DTYPE = jnp.bfloat16
VMEM_LIMIT = 56 * 1024 * 1024
SAMPLE_ROWS = 8
PAGES_PER_STEP = 16
NSA_ROW_CHUNKS = 4

NSA_Q_COLS = NSA_HEADS * HEAD_DIM
NSA_KV_COLS = 6 * NSA_KV_HEADS * HEAD_DIM
NSA_G_COLS = 3 * NSA_HEADS
SB_COLS = 3 * SB_HEADS * HEAD_DIM
MOBA_COLS = 3 * MOBA_HEADS * HEAD_DIM

_F32 = jnp.float32


def _dot(a, b):
    return jnp.dot(a, b, preferred_element_type=_F32)


def _dot_nt(a, b):
    return lax.dot_general(a, b, (((1,), (1,)), ((), ())), preferred_element_type=_F32)


def _dot_split(x, w):
    hi = x.astype(MXU_DTYPE)
    mid = (x - hi.astype(_F32)).astype(MXU_DTYPE)
    return _dot(hi, w) + _dot(mid, w)


def _rms(x, g):
    return x * lax.rsqrt(jnp.mean(x * x, axis=-1, keepdims=True) + RMS_EPS) * g


def _params(*sem):
    return pltpu.CompilerParams(dimension_semantics=sem, vmem_limit_bytes=VMEM_LIMIT)


def _full(shape):
    n = len(shape)
    return pl.BlockSpec(shape, lambda *_: (0,) * n)


def _ffn_kernel(*refs, with_ple):
    if with_ple:
        (h_ref, gpre_ref, gpost_ref, wg_ref, wu_ref, wd_ref,
         p_ref, wpg_ref, wpp_ref, gple_ref, o_ref, xn_ref, acc_ref) = refs
    else:
        h_ref, gpre_ref, gpost_ref, wg_ref, wu_ref, wd_ref, o_ref, xn_ref, acc_ref = refs
    xn_ref[...] = _rms(h_ref[...], gpre_ref[...]).astype(MXU_DTYPE)
    acc_ref[...] = jnp.zeros_like(acc_ref)

    def chunk(j, carry):
        xn = xn_ref[...]
        g = _dot(xn, wg_ref[j])
        u = _dot(xn, wu_ref[j])
        a = (g * jax.nn.sigmoid(g)) * u
        acc_ref[...] += _dot(a.astype(MXU_DTYPE), wd_ref[j])
        return carry

    lax.fori_loop(0, wg_ref.shape[0], chunk, 0)
    h = h_ref[...] + 0.5 * _rms(acc_ref[...], gpost_ref[...])
    if with_ple:
        gate = jax.nn.sigmoid(_dot(h.astype(MXU_DTYPE), wpg_ref[...]))
        ple = gate * _dot(p_ref[...].astype(MXU_DTYPE), wpp_ref[...])
        h = h + _rms(ple, gple_ref[...])
    o_ref[...] = h


def _resident(shape):
    n = len(shape)
    return pl.BlockSpec(shape, lambda *_: (0,) * n, pipeline_mode=pl.Buffered(1))


def _ffn(h, g_pre, g_post, wg, wu, wd, ple=None, *, tm):
    n, d = h.shape
    row = lambda i: (i, 0)
    in_specs = [pl.BlockSpec((tm, d), row), _full((1, d)), _full((1, d)),
                _resident(wg.shape), _resident(wu.shape), _resident(wd.shape)]
    args = [h, g_pre, g_post, wg, wu, wd]
    if ple is not None:
        p, wpg, wpp, g_ple = ple
        in_specs += [pl.BlockSpec((tm, p.shape[1]), row), _resident(wpg.shape), _resident(wpp.shape), _full((1, d))]
        args += [p, wpg, wpp, g_ple]
    return pl.pallas_call(
        functools.partial(_ffn_kernel, with_ple=ple is not None),
        grid=(n // tm,),
        in_specs=in_specs,
        out_specs=pl.BlockSpec((tm, d), row),
        out_shape=jax.ShapeDtypeStruct((n, d), _F32),
        scratch_shapes=[pltpu.VMEM((tm, d), MXU_DTYPE), pltpu.VMEM((tm, d), _F32)],
        compiler_params=_params("parallel"),
        name="ffn_ple" if ple is not None else "ffn",
    )(*args)


_G_QNSA = 0
_G_KV = 8
_G_SB = 14
_G_MB = 20
_G_GATE = 26
_ROPE_GROUPS = tuple(range(8)) + (8, 10, 12, 20, 21, 22, 23)
_INPROJ_OUT = (("q_nsa", _G_QNSA, 8, True, None), ("nsa", _G_KV, 4, False, _F32), ("win", _G_KV + 4, 2, False, _F32),
               ("q_sb", _G_SB, 2, True, None), ("sb", _G_SB + 2, 4, False, _F32),
               ("q_mb", _G_MB, 2, False, _F32), ("mb", _G_MB + 2, 4, False, _F32))
_INPROJ_NARROW = ("nsa", "win", "sb", "mb")


def _inproj_kernel(h_ref, g_ref, w_ref, cos_ref, sa_ref, sb_ref, *out_refs):
    xn = _rms(h_ref[...], g_ref[...]).astype(MXU_DTYPE)
    cos, sa, sb = cos_ref[...], sa_ref[...], sb_ref[...]
    scale = HEAD_DIM ** -0.5

    def group(gi):
        y = _dot(xn, w_ref[:, gi * LANES:(gi + 1) * LANES])
        if gi in _ROPE_GROUPS:
            y = y * cos + pltpu.roll(y, LANES - HEAD_DIM // 2, 1) * sa + pltpu.roll(y, HEAD_DIM // 2, 1) * sb
        return y

    outs = out_refs[:len(_INPROJ_OUT)]
    narrow = dict(zip(_INPROJ_NARROW, out_refs[len(_INPROJ_OUT):]))
    for (name, g0, n_groups, scaled, _), o_ref in zip(_INPROJ_OUT, outs):
        for i in range(n_groups):
            y = group(g0 + i)
            if scaled:
                y = y * scale
            o_ref[:, i * LANES:(i + 1) * LANES] = y.astype(o_ref.dtype)
            if name in narrow:
                narrow[name][:, i * LANES:(i + 1) * LANES] = y.astype(MXU_DTYPE)
    out_refs[-1][...] = jax.nn.sigmoid(group(_G_GATE))


def _inproj(h, g, w_all, tabs, tab_map, *, tm):
    n, d = h.shape
    row = lambda i: (i, 0)
    tab_spec = pl.BlockSpec((tm, LANES), tab_map)
    sizes = {o[0]: o[2] * LANES for o in _INPROJ_OUT}
    names = [o[0] for o in _INPROJ_OUT] + [k + "_bf" for k in _INPROJ_NARROW] + ["gates"]
    widths = [(o[2] * LANES, o[4] or MXU_DTYPE) for o in _INPROJ_OUT]
    widths += [(sizes[k], MXU_DTYPE) for k in _INPROJ_NARROW] + [(LANES, _F32)]
    outs = pl.pallas_call(
        _inproj_kernel,
        grid=(n // tm,),
        in_specs=[pl.BlockSpec((tm, d), row), _full((1, d)), _full(w_all.shape), tab_spec, tab_spec, tab_spec],
        out_specs=[pl.BlockSpec((tm, c), row) for c, _ in widths],
        out_shape=[jax.ShapeDtypeStruct((n, c), dt) for c, dt in widths],
        compiler_params=_params("parallel"),
        name="inproj",
    )(h, g, w_all, *tabs)
    return dict(zip(names, outs))


def _merge_kernel(h_ref, gpre_ref, gpost_ref, onsa_ref, osb_ref, omb_ref,
                  wbn_ref, wbs_ref, wbm_ref, wgate_ref, wout_ref, o_ref):
    d = h_ref.shape[1]
    h = h_ref[...]
    u = _rms(h, gpre_ref[...]).astype(MXU_DTYPE)
    branches = (_dot(onsa_ref[...], wbn_ref[...]), _dot(osb_ref[...], wbs_ref[...]), _dot(omb_ref[...], wbm_ref[...]))
    mix = None
    for c, br in enumerate(branches):
        term = jax.nn.sigmoid(_dot(u, wgate_ref[:, c * d:(c + 1) * d])) * br
        mix = term if mix is None else mix + term
    y = _dot(mix.astype(MXU_DTYPE), wout_ref[...])
    o_ref[...] = h + _rms(y, gpost_ref[...])


def _merge(h, g_pre, g_post, o_nsa, o_sb, o_mb, wbn, wbs, wbm, wgate, wout, *, tm):
    n, d = h.shape
    row = lambda i: (i, 0)
    return pl.pallas_call(
        _merge_kernel,
        grid=(n // tm,),
        in_specs=[pl.BlockSpec((tm, d), row), _full((1, d)), _full((1, d)),
                  pl.BlockSpec((tm, o_nsa.shape[1]), row), pl.BlockSpec((tm, o_sb.shape[1]), row),
                  pl.BlockSpec((tm, o_mb.shape[1]), row),
                  _full(wbn.shape), _full(wbs.shape), _full(wbm.shape), _full(wgate.shape), _full(wout.shape)],
        out_specs=pl.BlockSpec((tm, d), row),
        out_shape=jax.ShapeDtypeStruct((n, d), _F32),
        compiler_params=_params("parallel"),
        name="merge",
    )(h, g_pre, g_post, o_nsa, o_sb, o_mb, wbn, wbs, wbm, wgate, wout)


def _lane_lo(shape=(1, LANES)):
    return (lax.broadcasted_iota(jnp.int32, shape, len(shape) - 1) % LANES) < HEAD_DIM


def _softplus(z):
    return jnp.maximum(z, 0.0) + jnp.log1p(jnp.exp(-jnp.abs(z)))


def _cumsum_matrix(tk):
    j = lax.broadcasted_iota(jnp.int32, (tk, 2 * tk), 0)
    s = lax.broadcasted_iota(jnp.int32, (tk, 2 * tk), 1)
    return jnp.where((j > s) | (s >= tk), 1.0, 0.0).astype(MXU_DTYPE)


def _online_update1(state, z, v1, nt=False):
    m, acc = state
    m_new = jnp.maximum(m, jnp.max(z, axis=-1, keepdims=True))
    p = jnp.exp(z - m_new).astype(MXU_DTYPE)
    return m_new, jnp.exp(m - m_new) * acc + (_dot_nt(p, v1) if nt else _dot(p, v1))


def _online_init1(rows, width=LANES):
    return jnp.full((rows, 1), NEG, _F32), jnp.zeros((rows, width + LANES), _F32)


def _finish1(state):
    _, acc = state
    width = acc.shape[1] - LANES
    den = 1.0 / acc[:, width:]
    return acc[:, :width] * jnp.concatenate([den] * (width // LANES), axis=1)


def _with_ones(v, axis=1):
    shape = (v.shape[0], LANES) if axis == 1 else (LANES, v.shape[1])
    return jnp.concatenate([v, jnp.ones(shape, v.dtype)], axis=axis)


def _seed_state(m, l, acc):
    return m, jnp.concatenate([acc, jnp.broadcast_to(l, (l.shape[0], LANES))], axis=1)


def _topk_mask(val, k, axis):
    idx = lax.broadcasted_iota(jnp.int32, val.shape, axis).astype(_F32)
    sel = jnp.zeros(val.shape, _F32)
    for _ in range(k):
        m = jnp.max(val, axis=axis, keepdims=True)
        first = jnp.min(jnp.where(val == m, idx, 1e9), axis=axis, keepdims=True)
        hit = idx == first
        sel = jnp.where(hit, 1.0, sel)
        val = jnp.where(hit, -jnp.inf, val)
    return sel


def _stack_nsa_queries(q):
    return jnp.concatenate([q[:, hh * LANES:(hh + 1) * LANES] for hh in range(NSA_HEADS)], axis=0)


def _cmp_branch(qs, kc, vc, qpos, tq):
    nc = kc.shape[0]
    s = _dot_nt(qs, kc)
    cend = lax.broadcasted_iota(jnp.int32, (1, nc), 1) * CMP_STRIDE + (CMP_LEN - 1)
    vis = cend <= qpos
    sm = jnp.where(vis, s, NEG)
    m = jnp.max(sm, axis=-1, keepdims=True)
    e = jnp.where(vis, jnp.exp(sm - m), 0.0)
    l = jnp.sum(e, axis=-1, keepdims=True)
    p = e * (1.0 / jnp.where(l > 0.0, l, 1.0))
    o_cmp = _dot(p.astype(MXU_DTYPE), vc)
    psums = []
    for kv in range(NSA_KV_HEADS):
        blocks = [p[(kv * NSA_GROUP + g) * tq:(kv * NSA_GROUP + g + 1) * tq] for g in range(NSA_GROUP)]
        psums.append(functools.reduce(lambda a, b: a + b, blocks))
    return o_cmp, psums


def _nsa_select(imp, qp, n_blocks, n_top, axis):
    jblk = lax.broadcasted_iota(jnp.int32, imp.shape, axis)
    cur = qp // SEL_BLOCK
    forced = (jblk == 0) | (jblk == cur) | (jblk == cur - 1)
    valid = (jblk * SEL_BLOCK <= qp) & (jblk < n_blocks)
    val = jnp.where(valid, imp + jnp.where(forced, FORCE_BONUS, 0.0), NEG)
    return jnp.where(valid, _topk_mask(val, n_top, axis), 0.0) - 1.0


def _nsa_lhs(qs, biases):
    bias = jnp.concatenate([b for b in biases for _ in range(NSA_GROUP)], axis=0)
    return jnp.concatenate([qs, bias.astype(qs.dtype)], axis=1)


def _combine_nsa(gates, o_cmp, o_sel, o_win, tq):
    lo = _lane_lo()
    heads = []
    for hh in range(NSA_HEADS):
        r = slice(hh * tq, (hh + 1) * tq)
        o = (gates[:, 3 * hh:3 * hh + 1] * o_cmp[r] + gates[:, 3 * hh + 1:3 * hh + 2] * o_sel[r]
             + gates[:, 3 * hh + 2:3 * hh + 3] * o_win[r])
        if hh % 2 != hh // NSA_GROUP:
            o = pltpu.roll(o, HEAD_DIM, 1)
        heads.append(o)
    return jnp.concatenate([jnp.where(lo, heads[2 * i], heads[2 * i + 1]) for i in range(NSA_HEADS // 2)], axis=1)


def _moba_pick(s, own, axis):
    past = lax.broadcasted_iota(jnp.int32, s.shape, axis) < own
    return jnp.where(past, _topk_mask(jnp.where(past, s, NEG), MOBA_TOPK, axis), 0.0) - 1.0


def _sb_block(qh, kt, vt, cum, carry, keep):
    c, o = carry
    tk = kt.shape[0]
    z = _dot_nt(qh, kt)
    lk = -_softplus(z)
    if keep is not None:
        lk = jnp.where(keep, lk, 0.0)
    r = _dot_split(lk, cum)
    a = jnp.exp(lk + z + r[:, :tk] + c)
    if keep is not None:
        a = jnp.where(keep, a, 0.0)
    return c + r[:, tk:], o + _dot(a.astype(MXU_DTYPE), vt)


def _stage_chunks(src_ref, lead, n_chunks, dst_ref, c0):
    for r in range(CMP_STRIDE):
        x = src_ref[lead + (pl.ds(r, n_chunks, stride=CMP_STRIDE), slice(None))]
        dst_ref[pl.ds(c0, n_chunks), r * LANES:(r + 1) * LANES] = x


def _compress_finish(xk_ref, xv_ref, wk_ref, wv_ref, pe_ref, w2_ref, ok_ref, ov_ref):
    nc = xk_ref.shape[0]
    for x_ref, w_ref, t, o_ref in ((xk_ref, wk_ref, 0, ok_ref), (xv_ref, wv_ref, 1, ov_ref)):
        ab = _dot(x_ref[...].astype(MXU_DTYPE), w_ref[...])
        cst = _dot(pe_ref[t].astype(MXU_DTYPE), w_ref[...])
        for h in range(NSA_KV_HEADS):
            a = ab[:, h * 256:h * 256 + LANES]
            b = ab[:, h * 256 + LANES:(h + 1) * 256]
            c = cst[0:1, h * 256:h * 256 + LANES] + cst[1:2, h * 256 + LANES:(h + 1) * 256]
            hid = jax.nn.gelu(a + pltpu.roll(b, nc - 1, 0) + c)
            part = _dot(hid.astype(MXU_DTYPE), w2_ref[t, h])
            out = part if h == 0 else out + part
        o_ref[0] = out.astype(o_ref.dtype)


def _compress_prompt_kernel(rk_ref, rv_ref, wk_ref, wv_ref, pe_ref, w2_ref, ok_ref, ov_ref, xk_ref, xv_ref):
    n_chunks = rk_ref.shape[1] // CMP_STRIDE
    _stage_chunks(rk_ref, (0,), n_chunks, xk_ref, 0)
    _stage_chunks(rv_ref, (0,), n_chunks, xv_ref, 0)
    _compress_finish(xk_ref, xv_ref, wk_ref, wv_ref, pe_ref, w2_ref, ok_ref, ov_ref)


def _compress_sample_kernel(pt_ref, wk_ref, wv_ref, pe_ref, w2_ref, *refs, n_pages_step):
    del pt_ref
    pages = refs[:n_pages_step]
    ok_ref, ov_ref, xk_ref, xv_ref, rows_ref = refs[n_pages_step:]
    s = pl.program_id(1)
    per_page = pages[0].shape[3] // CMP_STRIDE
    for gp, page in enumerate(pages):
        x = page[0, 0]
        rows_ref[0] = x[:LANES].T
        rows_ref[1] = x[LANES:].T
        c0 = pl.multiple_of((s * n_pages_step + gp) * per_page, per_page)
        _stage_chunks(rows_ref, (0,), per_page, xk_ref, c0)
        _stage_chunks(rows_ref, (1,), per_page, xv_ref, c0)

    @pl.when(s == pl.num_programs(1) - 1)
    def _():
        _compress_finish(xk_ref, xv_ref, wk_ref, wv_ref, pe_ref, w2_ref, ok_ref, ov_ref)


def _compress_weights(pe, w1, w2):
    eye = jnp.eye(NSA_KV_HEADS, dtype=w1.dtype)
    big, pes = [], []
    for t in range(2):
        w1r = w1[t].reshape(2, CMP_STRIDE, HEAD_DIM, w1.shape[-1])
        wb = jnp.einsum("hH,ardn->rhdHan", eye, w1r)
        big.append(wb.reshape(CMP_STRIDE * NSA_KV_HEADS * HEAD_DIM, -1).astype(MXU_DTYPE))
        per = pe[t].reshape(2, CMP_STRIDE, 1, HEAD_DIM)
        rows = jnp.broadcast_to(per, (2, CMP_STRIDE, NSA_KV_HEADS, HEAD_DIM)).reshape(2, -1)
        pes.append(jnp.concatenate([rows, jnp.zeros((SUBLANES - 2, rows.shape[1]), rows.dtype)], axis=0))
    zero = jnp.zeros_like(w2)
    w2p = jnp.stack([jnp.concatenate([w2, zero], axis=-1), jnp.concatenate([zero, w2], axis=-1)], axis=1)
    return big[0], big[1], jnp.stack(pes), w2p.astype(MXU_DTYPE)


def _compress_out(b, nc):
    shape = jax.ShapeDtypeStruct((b, nc, LANES), MXU_DTYPE)
    return [shape, shape]


def _compress_prompt(nsa_rows, cw):
    b, t, _ = nsa_rows.shape
    nc = t // CMP_STRIDE
    wk, wv, pe2, w2p = cw
    out_spec = pl.BlockSpec((1, nc, LANES), lambda i: (i, 0, 0))
    return pl.pallas_call(
        _compress_prompt_kernel,
        grid=(b,),
        in_specs=[pl.BlockSpec((1, t, LANES), lambda i: (i, 0, 0)),
                  pl.BlockSpec((1, t, LANES), lambda i: (i, 0, 1)),
                  _full(wk.shape), _full(wv.shape), _full(pe2.shape), _full(w2p.shape)],
        out_specs=[out_spec, out_spec],
        out_shape=_compress_out(b, nc),
        scratch_shapes=[pltpu.VMEM((nc, CMP_STRIDE * LANES), _F32)] * 2,
        compiler_params=_params("arbitrary"),
        name="nsa_compress_prompt",
    )(nsa_rows, nsa_rows, wk, wv, pe2, w2p)


def _nsa_prompt_kernel(q_ref, g_ref, ck_ref, cv_ref, sk_ref, sv_ref, wk_ref, wv_ref, eblk_ref, cover_ref,
                       o_ref, *, tq, tk, n_top):
    qi = pl.program_id(1)
    t0 = qi * tq
    rows = NSA_HEADS * tq
    n_blocks = sk_ref.shape[1] // SEL_BLOCK
    qs = _stack_nsa_queries(q_ref[0])
    qpos_r = t0 + lax.broadcasted_iota(jnp.int32, (rows, 1), 0) % tq

    o_cmp, psums = _cmp_branch(qs, ck_ref[0], cv_ref[0], qpos_r, tq)
    qp_t = t0 + lax.broadcasted_iota(jnp.int32, (1, tq), 1)
    biases = []
    for psum in psums:
        imp_t = _dot_split(psum, cover_ref[...]).T
        biases.append(_nsa_select(imp_t, qp_t, n_blocks, n_top, 0).T)
    lhs = _nsa_lhs(qs, biases)

    kcol = lax.broadcasted_iota(jnp.int32, (1, tk), 1)

    chunk = rows // NSA_ROW_CHUNKS
    lhs_c = [lhs[c * chunk:(c + 1) * chunk] for c in range(NSA_ROW_CHUNKS)]
    qpos_c = [qpos_r[c * chunk:(c + 1) * chunk] for c in range(NSA_ROW_CHUNKS)]

    def sel_tile(kt, states, causal):
        start = pl.multiple_of(kt * tk, tk)
        rhs = jnp.concatenate([sk_ref[0, pl.ds(start, tk), :], eblk_ref[pl.ds(start, tk), :]], axis=1)
        v1 = _with_ones(sv_ref[0, pl.ds(start, tk), :])
        out = []
        for c in range(NSA_ROW_CHUNKS):
            z = _dot_nt(lhs_c[c], rhs)
            if causal:
                z = jnp.where(start + kcol <= qpos_c[c], z, NEG)
            out.append(_online_update1(states[c], z, v1))
        return tuple(out)

    n_full = t0 // tk
    init = tuple(_online_init1(chunk) for _ in range(NSA_ROW_CHUNKS))
    states = lax.fori_loop(0, n_full, lambda kt, st: sel_tile(kt, st, False), init)
    o_sel = jnp.concatenate([_finish1(st) for st in sel_tile(n_full, states, True)], axis=0)

    n_band = WINDOW + tq
    ws = pl.multiple_of(jnp.maximum(t0 - WINDOW, 0), tq)
    z = _dot_nt(qs, wk_ref[0, pl.ds(ws, n_band), :])
    dist = qpos_r - (ws + lax.broadcasted_iota(jnp.int32, (1, n_band), 1))
    z = jnp.where((dist >= 0) & (dist <= WINDOW), z, NEG)
    p = jnp.exp(z - jnp.max(z, axis=-1, keepdims=True))
    o_win = _dot(p.astype(MXU_DTYPE), wv_ref[0, pl.ds(ws, n_band), :]) * (1.0 / jnp.sum(p, axis=-1, keepdims=True))

    o_ref[0] = _combine_nsa(g_ref[0], o_cmp, o_sel, o_win, tq).astype(o_ref.dtype)


def _nsa_prompt(q_nsa, gates, cmp_k, cmp_v, nsa_bf, win_bf, eblk, cover, *, tq, tk):
    b, t, _ = q_nsa.shape
    nc = cmp_k.shape[1]
    n_top = min(SEL_TOPN, t // SEL_BLOCK)
    cmp_spec = pl.BlockSpec((1, nc, LANES), lambda i, qi: (i, 0, 0))
    col = lambda g: pl.BlockSpec((1, t, LANES), lambda i, qi: (i, 0, g))
    return pl.pallas_call(
        functools.partial(_nsa_prompt_kernel, tq=tq, tk=tk, n_top=n_top),
        grid=(b, t // tq),
        in_specs=[pl.BlockSpec((1, tq, NSA_HEADS * LANES), lambda i, qi: (i, qi, 0)),
                  pl.BlockSpec((1, tq, LANES), lambda i, qi: (i, qi, 0)),
                  cmp_spec, cmp_spec, col(2), col(3), col(0), col(1),
                  _full(eblk.shape), _full(cover.shape)],
        out_specs=pl.BlockSpec((1, tq, NSA_Q_COLS), lambda i, qi: (i, qi, 0)),
        out_shape=jax.ShapeDtypeStruct((b, t, NSA_Q_COLS), MXU_DTYPE),
        compiler_params=_params("parallel", "arbitrary"),
        name="nsa_prompt",
    )(q_nsa, gates, cmp_k, cmp_v, nsa_bf, nsa_bf, win_bf, win_bf, eblk, cover)


def _sb_prompt_kernel(q_ref, k_ref, v_ref, o_ref, *, tq, tk):
    qi = pl.program_id(2)
    lo = _lane_lo()
    q = q_ref[0]
    zero_q = jnp.zeros_like(q)
    qh = (jnp.where(lo, q, zero_q), jnp.where(lo, zero_q, q))
    cum = _cumsum_matrix(tk)
    row = lax.broadcasted_iota(jnp.int32, (tq, tk), 0)
    col = lax.broadcasted_iota(jnp.int32, (tq, tk), 1)
    per_tile = tq // tk

    def blocks(kb, carry, keep):
        start = pl.multiple_of(kb * tk, tk)
        kt, vt = k_ref[0, pl.ds(start, tk), :], v_ref[0, pl.ds(start, tk), :]
        return tuple(_sb_block(qh[h], kt, vt, cum, carry[h], keep) for h in range(2))

    def alive(carry):
        return (jnp.max(jnp.maximum(carry[0][0], carry[1][0])) > SB_DEAD).astype(jnp.int32)

    zero = jnp.zeros((tq, LANES), _F32)
    carry = ((zero, zero), (zero, zero))
    for d in reversed(range(per_tile)):
        carry = blocks(qi * per_tile + d, carry, d * tk + col < row)

    def body(st):
        i, _, carry = st
        carry = blocks(qi * per_tile - 1 - i, carry, None)
        return i + 1, alive(carry), carry

    n_past = qi * per_tile
    _, _, carry = lax.while_loop(lambda st: (st[0] < n_past) & (st[1] > 0), body, (0, alive(carry), carry))
    o_ref[0] = jnp.where(lo, carry[0][1], carry[1][1]).astype(o_ref.dtype)


def _sb_prompt(q_sb, sb_bf, *, tq, tk):
    b, t, _ = q_sb.shape
    n_pair = SB_HEADS // 2
    return pl.pallas_call(
        functools.partial(_sb_prompt_kernel, tq=tq, tk=tk),
        grid=(b, n_pair, t // tq),
        in_specs=[pl.BlockSpec((1, tq, LANES), lambda i, p, qi: (i, qi, p)),
                  pl.BlockSpec((1, t, LANES), lambda i, p, qi: (i, 0, p)),
                  pl.BlockSpec((1, t, LANES), lambda i, p, qi: (i, 0, n_pair + p))],
        out_specs=pl.BlockSpec((1, tq, LANES), lambda i, p, qi: (i, qi, p)),
        out_shape=jax.ShapeDtypeStruct((b, t, SB_HEADS * HEAD_DIM), MXU_DTYPE),
        compiler_params=_params("parallel", "parallel", "arbitrary"),
        name="sb_prompt",
    )(q_sb, sb_bf, sb_bf)


def _kmean_kernel(k_ref, o_ref):
    x = k_ref[0]
    nb = x.shape[0] // MOBA_BLOCK
    o_ref[0] = jnp.mean(x.reshape(nb, MOBA_BLOCK, x.shape[1]), axis=1)


def _kmean_prompt(mb_rows):
    b, t, _ = mb_rows.shape
    nb = SUBLANES
    return pl.pallas_call(
        _kmean_kernel,
        grid=(b, t // (nb * MOBA_BLOCK)),
        in_specs=[pl.BlockSpec((1, nb * MOBA_BLOCK, 2 * LANES), lambda i, j: (i, j, 0))],
        out_specs=pl.BlockSpec((1, nb, 2 * LANES), lambda i, j: (i, j, 0)),
        out_shape=jax.ShapeDtypeStruct((b, t // MOBA_BLOCK, 2 * LANES), _F32),
        compiler_params=_params("parallel", "parallel"),
        name="moba_kmean_prompt",
    )(mb_rows)


def _moba_prompt_kernel(q_ref, km_ref, k_ref, v_ref, eblk_ref, o_ref, *, tq, tk):
    qi = pl.program_id(2)
    lo = _lane_lo()
    q = q_ref[0]
    scale = HEAD_DIM ** -0.5
    row = lax.broadcasted_iota(jnp.int32, (tq, tq), 0)
    col = lax.broadcasted_iota(jnp.int32, (tq, tq), 1)
    own0 = pl.multiple_of(qi * tq, tq)
    k_own, v_own = k_ref[0, pl.ds(own0, tq), :], _with_ones(v_ref[0, pl.ds(own0, tq), :])
    lhs, states = [], []
    for hh in range(2):
        qf = jnp.where(lo if hh == 0 else jnp.logical_not(lo), q, 0.0)
        gate = lax.dot_general(qf, km_ref[0], (((1,), (1,)), ((), ())), preferred_element_type=_F32,
                               precision=lax.Precision.HIGHEST)
        bias = _moba_pick(gate.T, qi, 0).T
        qh = (qf * scale).astype(MXU_DTYPE)
        lhs.append(jnp.concatenate([qh, bias.astype(MXU_DTYPE)], axis=1))
        z = jnp.where(col <= row, _dot_nt(qh, k_own), NEG)
        states.append(_online_update1(_online_init1(tq), z, v_own))

    def tile(j, states):
        start = pl.multiple_of(j * tk, tk)
        rhs = jnp.concatenate([k_ref[0, pl.ds(start, tk), :], eblk_ref[pl.ds(start, tk), :]], axis=1)
        v1 = _with_ones(v_ref[0, pl.ds(start, tk), :])
        return tuple(_online_update1(st, _dot_nt(l, rhs), v1) for l, st in zip(lhs, states))

    states = lax.fori_loop(0, (qi * tq + tk - 1) // tk, tile, tuple(states))
    o_ref[0] = jnp.where(lo, _finish1(states[0]), _finish1(states[1])).astype(o_ref.dtype)


def _moba_prompt(q_mb, kmean, mb_bf, eblk, *, tk):
    b, t, _ = q_mb.shape
    tq = MOBA_BLOCK
    n_pair = MOBA_HEADS // 2
    return pl.pallas_call(
        functools.partial(_moba_prompt_kernel, tq=tq, tk=tk),
        grid=(b, n_pair, t // tq),
        in_specs=[pl.BlockSpec((1, tq, LANES), lambda i, p, qi: (i, qi, p)),
                  pl.BlockSpec((1, LANES, LANES), lambda i, p, qi: (i, 0, p)),
                  pl.BlockSpec((1, t, LANES), lambda i, p, qi: (i, 0, p)),
                  pl.BlockSpec((1, t, LANES), lambda i, p, qi: (i, 0, n_pair + p)),
                  _full(eblk.shape)],
        out_specs=pl.BlockSpec((1, tq, LANES), lambda i, p, qi: (i, qi, p)),
        out_shape=jax.ShapeDtypeStruct((b, t, MOBA_HEADS * HEAD_DIM), MXU_DTYPE),
        compiler_params=_params("parallel", "parallel", "arbitrary"),
        name="moba_prompt",
    )(q_mb, kmean, mb_bf, mb_bf, eblk)


def _row_scores(q, keys, n):
    return [jnp.sum(q * keys[j:j + 1, :], axis=-1, keepdims=True) for j in range(n)]


def _new_key_softmax(q, keys, vals, n, visible):
    zs = [jnp.where(visible(j), z, NEG) for j, z in enumerate(_row_scores(q, keys, n))]
    m = functools.reduce(jnp.maximum, zs)
    l = jnp.zeros_like(m)
    acc = jnp.zeros((q.shape[0], vals.shape[1]), _F32)
    for j, z in enumerate(zs):
        p = jnp.exp(z - m)
        l = l + p
        acc = acc + p * vals[j:j + 1, :]
    return m, l, acc


def _stack_heads(q, n_heads):
    head = lax.broadcasted_iota(jnp.int32, (1, q.shape[1]), 1) // HEAD_DIM
    return jnp.concatenate([jnp.where(head == h, q, jnp.zeros_like(q)) for h in range(n_heads)], axis=0)


def _unstack_heads(o, n_heads):
    rows = o.shape[0] // n_heads
    head = lax.broadcasted_iota(jnp.int32, (1, o.shape[1]), 1) // HEAD_DIM
    out = jnp.zeros((rows, o.shape[1]), _F32)
    for h in range(n_heads):
        out = jnp.where(head == h, o[h * rows:(h + 1) * rows], out)
    return out


def _side_by_side(pages, f0, f1):
    return jnp.concatenate([p[0, 0, f0:f1, :] for p in pages], axis=1).astype(MXU_DTYPE)


def _key_block_onehot(n_keys, first_key, block):
    key_blk = (first_key + lax.broadcasted_iota(jnp.int32, (LANES, n_keys), 1)) // block
    return jnp.where(lax.broadcasted_iota(jnp.int32, (LANES, n_keys), 0) == key_blk, -NEG, 0.0).astype(MXU_DTYPE)


def _state_refs_store(m_ref, acc_ref, state):
    m_ref[...] = jnp.broadcast_to(state[0], m_ref.shape)
    acc_ref[...] = state[1]


def _page_specs(layer, n, block, row_block, page_of):
    def spec(gp):
        return pl.BlockSpec(block, lambda b, s, pt: (layer, pt[b, page_of(s, gp)], row_block, 0))
    return [spec(gp) for gp in range(n)]


def _sample_call(kernel_fn, name, page_table, n_steps, pre_args, pre_specs, page_arrays, page_specs, out_block,
                 out_shape, scratch):
    grid_spec = pltpu.PrefetchScalarGridSpec(
        num_scalar_prefetch=1,
        grid=(page_table.shape[0], n_steps),
        in_specs=pre_specs + page_specs,
        out_specs=out_block,
        scratch_shapes=scratch,
    )
    return pl.pallas_call(kernel_fn, grid_spec=grid_spec, out_shape=out_shape,
                          compiler_params=_params("arbitrary", "arbitrary"), name=name,
                          )(page_table, *pre_args, *page_arrays)


def _seq_block(shape):
    zeros = (0,) * (len(shape) - 1)
    return pl.BlockSpec((1,) + tuple(shape[1:]), lambda b, s, pt: (b,) + zeros)


def _const_block(shape):
    zeros = (0,) * len(shape)
    return pl.BlockSpec(tuple(shape), lambda b, s, pt: zeros)


def _compress_sample(cache, layer, page_table, cw):
    b, n_pages = page_table.shape
    page = cache.shape[3]
    nc = n_pages * page // CMP_STRIDE
    g = PAGES_PER_STEP
    wk, wv, pe2, w2p = cw
    out_spec = pl.BlockSpec((1, nc, LANES), lambda i, s, pt: (i, 0, 0))
    return _sample_call(
        functools.partial(_compress_sample_kernel, n_pages_step=g), "nsa_compress_sample", page_table, n_pages // g,
        [wk, wv, pe2, w2p], [_const_block(wk.shape), _const_block(wv.shape), _const_block(pe2.shape),
                             _const_block(w2p.shape)],
        [cache] * g, _page_specs(layer, g, (1, 1, 2 * LANES, page), 0, lambda s, gp: s * g + gp),
        [out_spec, out_spec], _compress_out(b, nc),
        [pltpu.VMEM((nc, CMP_STRIDE * LANES), _F32)] * 2 + [pltpu.VMEM((2, page, LANES), _F32)])


def _nsa_sample_kernel(pt_ref, q_ref, g_ref, ck_ref, cv_ref, new_ref, neww_ref, win_ref, cover_ref, *refs,
                       n_pages_step, n_new, past_len, n_top):
    del pt_ref
    pages = refs[:n_pages_step]
    o_ref, m_ref, acc_ref, bias_ref, ocmp_ref, owin_ref = refs[n_pages_step:]
    s = pl.program_id(1)
    tq = SAMPLE_ROWS
    rows = NSA_HEADS * tq
    qs = _stack_nsa_queries(q_ref[0])
    qi_r = lax.broadcasted_iota(jnp.int32, (rows, 1), 0) % tq
    qpos_r = past_len + qi_r

    @pl.when(s == 0)
    def _():
        qf = qs.astype(_F32)
        o_cmp, psums = _cmp_branch(qs, ck_ref[0], cv_ref[0], qpos_r, tq)
        ocmp_ref[...] = o_cmp
        qp = past_len + lax.broadcasted_iota(jnp.int32, (tq, 1), 0)
        biases = [_nsa_select(_dot_split(psum, cover_ref[...]), qp, past_len // SEL_BLOCK, n_top - 1, 1)
                  for psum in psums]
        bias_ref[...] = jnp.concatenate([b for b in biases for _ in range(NSA_GROUP)], axis=0)
        visible = lambda j: j <= qi_r
        new = new_ref[0].astype(_F32)
        state = _seed_state(*_new_key_softmax(qf, new[:, 2 * LANES:3 * LANES], new[:, 3 * LANES:], n_new, visible))
        _state_refs_store(m_ref, acc_ref, state)
        wnew = neww_ref[0].astype(_F32)
        wstate = _seed_state(*_new_key_softmax(qf, wnew[:, :LANES], wnew[:, LANES:], n_new, visible))
        w = win_ref[0, 0]
        n_win = w.shape[1]
        dist = qpos_r - (past_len - n_win + lax.broadcasted_iota(jnp.int32, (1, n_win), 1))
        z = jnp.where((dist >= 0) & (dist <= WINDOW), _dot(qs, w[:LANES].astype(MXU_DTYPE)), NEG)
        owin_ref[...] = _finish1(_online_update1(wstate, z, _with_ones(w[LANES:].astype(MXU_DTYPE), 0), nt=True))

    n_keys = n_pages_step * pages[0].shape[3]
    lhs = jnp.concatenate([qs, bias_ref[...].astype(MXU_DTYPE)], axis=1)
    rhs = jnp.concatenate([_side_by_side(pages, 0, LANES), _key_block_onehot(n_keys, s * n_keys, SEL_BLOCK)], axis=0)
    state = (m_ref[:, 0:1], acc_ref[...])
    state = _online_update1(state, _dot(lhs, rhs), _with_ones(_side_by_side(pages, LANES, 2 * LANES), 0), nt=True)
    _state_refs_store(m_ref, acc_ref, state)

    @pl.when(s == pl.num_programs(1) - 1)
    def _():
        o_sel = _finish1((m_ref[:, 0:1], acc_ref[...]))
        o_ref[0] = _combine_nsa(g_ref[0], ocmp_ref[...], o_sel, owin_ref[...], tq).astype(o_ref.dtype)


def _nsa_sample(q, gates, cmp_k, cmp_v, new_bf, new_win_bf, cache, cache_win, layer, page_table, cover, n_new):
    b, n_pages = page_table.shape
    page = cache.shape[3]
    past_len = n_pages * page
    g = PAGES_PER_STEP
    rows = NSA_HEADS * SAMPLE_ROWS
    n_top = min(SEL_TOPN, past_len // SEL_BLOCK + 1)
    win_spec = pl.BlockSpec((1, 1) + cache_win.shape[2:], lambda i, s, pt: (layer, i, 0, 0))
    out_shape = (b, SAMPLE_ROWS, NSA_Q_COLS)
    vmem = lambda w: pltpu.VMEM((rows, w), _F32)
    return _sample_call(
        functools.partial(_nsa_sample_kernel, n_pages_step=g, n_new=n_new, past_len=past_len, n_top=n_top),
        "nsa_sample", page_table, n_pages // g,
        [q, gates, cmp_k, cmp_v, new_bf, new_win_bf, cache_win, cover],
        [_seq_block(q.shape), _seq_block(gates.shape), _seq_block(cmp_k.shape), _seq_block(cmp_v.shape),
         _seq_block(new_bf.shape), _seq_block(new_win_bf.shape), win_spec, _const_block(cover.shape)],
        [cache] * g, _page_specs(layer, g, (1, 1, 2 * LANES, page), 1, lambda s, gp: s * g + gp),
        _seq_block(out_shape), jax.ShapeDtypeStruct(out_shape, MXU_DTYPE),
        [vmem(LANES), vmem(2 * LANES), vmem(LANES), vmem(LANES), vmem(LANES)])


def _sb_sample_kernel(pt_ref, q_ref, new_ref, *refs, n_pages_step, n_new):
    del pt_ref
    pages = refs[:n_pages_step]
    o_ref, c_ref, acc_ref = refs[n_pages_step:]
    s = pl.program_id(1)
    width = SB_HEADS * HEAD_DIM
    qs = _stack_heads(q_ref[0], SB_HEADS)
    rows = qs.shape[0]
    qi_r = lax.broadcasted_iota(jnp.int32, (rows, 1), 0) % SAMPLE_ROWS

    @pl.when(s == 0)
    def _():
        qf = qs.astype(_F32)
        new = new_ref[0].astype(_F32)
        zs = _row_scores(qf, new[:, :width], n_new)
        c = jnp.zeros((rows, 1), _F32)
        o = jnp.zeros((rows, width), _F32)
        for j in reversed(range(n_new)):
            seen = j < qi_r
            lk = jnp.where(seen, -_softplus(zs[j]), 0.0)
            a = jnp.where(seen, jnp.exp(lk + zs[j] + c), 0.0)
            o = o + a * new[j:j + 1, width:]
            c = c + lk
        c_ref[...] = jnp.broadcast_to(c, c_ref.shape)
        acc_ref[...] = o

    page_keys = pages[0].shape[3]
    cum = _cumsum_matrix(page_keys)
    z = _dot(qs, _side_by_side(pages, 0, width))
    lk = -_softplus(z)
    c = c_ref[...]
    later = []
    for gp in range(n_pages_step):
        r = _dot_split(lk[:, gp * page_keys:(gp + 1) * page_keys], cum)
        later.append(r[:, :page_keys] + c)
        c = c + r[:, page_keys:]
    a = jnp.exp(lk + z + jnp.concatenate(later, axis=1))
    acc_ref[...] += _dot_nt(a.astype(MXU_DTYPE), _side_by_side(pages, width, 2 * width))
    c_ref[...] = c

    @pl.when(s == pl.num_programs(1) - 1)
    def _():
        o_ref[0] = _unstack_heads(acc_ref[...], SB_HEADS).astype(o_ref.dtype)


def _sb_sample(q, new_bf, cache, layer, page_table, n_new):
    b, n_pages = page_table.shape
    page = cache.shape[3]
    g = PAGES_PER_STEP
    rows = SB_HEADS * SAMPLE_ROWS
    width = SB_HEADS * HEAD_DIM
    return _sample_call(
        functools.partial(_sb_sample_kernel, n_pages_step=g, n_new=n_new),
        "sb_sample", page_table, n_pages // g,
        [q, new_bf], [_seq_block(q.shape), _seq_block(new_bf.shape)],
        [cache] * g, _page_specs(layer, g, (1, 1, 2 * width, page), 0, lambda s, gp: n_pages - 1 - (s * g + gp)),
        _seq_block(q.shape), jax.ShapeDtypeStruct(q.shape, MXU_DTYPE),
        [pltpu.VMEM((rows, LANES), _F32), pltpu.VMEM((rows, width), _F32)])


def _moba_sample_kernel(pt_ref, q_ref, new_ref, *refs, n_pages_step, n_new, past_len):
    del pt_ref
    g = n_pages_step
    kpages, pages = refs[:g], refs[g:2 * g]
    o_ref, m_ref, acc_ref, bias_ref, km_ref = refs[2 * g:]
    s = pl.program_id(1)
    n_half = pl.num_programs(1) // 2
    width = MOBA_HEADS * HEAD_DIM
    page_keys = pages[0].shape[3]
    n_keys = g * page_keys
    qf = _stack_heads(q_ref[0], MOBA_HEADS)
    rows = qf.shape[0]
    qs = (qf * HEAD_DIM ** -0.5).astype(MXU_DTYPE)
    qi_r = lax.broadcasted_iota(jnp.int32, (rows, 1), 0) % SAMPLE_ROWS
    lane = lax.broadcasted_iota(jnp.int32, (1, LANES), 1)

    @pl.when(s == 0)
    def _():
        km_ref[...] = jnp.zeros_like(km_ref)

    @pl.when(s < n_half)
    def _():
        km = km_ref[...]
        for gp, page in enumerate(kpages):
            blk = (s * g + gp) * page_keys // MOBA_BLOCK
            part = jnp.sum(page[0, 0], axis=1, keepdims=True) * (1.0 / MOBA_BLOCK)
            km = km + jnp.where(lane == blk, part, 0.0)
        km_ref[...] = km

    @pl.when(s == n_half)
    def _():
        gate = lax.dot_general(qf, km_ref[...], (((1,), (0,)), ((), ())), preferred_element_type=_F32,
                               precision=lax.Precision.HIGHEST)
        bias_ref[...] = _moba_pick(gate, past_len // MOBA_BLOCK, 1)
        new = new_ref[0].astype(_F32)
        m, l, acc = _new_key_softmax(qs.astype(_F32), new[:, :width], new[:, width:], n_new, lambda j: j <= qi_r)
        _state_refs_store(m_ref, acc_ref, _seed_state(m, l, acc))

    @pl.when(s >= n_half)
    def _():
        lhs = jnp.concatenate([qs, bias_ref[...].astype(MXU_DTYPE)], axis=1)
        rhs = jnp.concatenate([_side_by_side(pages, 0, width),
                               _key_block_onehot(n_keys, (s - n_half) * n_keys, MOBA_BLOCK)], axis=0)
        state = (m_ref[:, 0:1], acc_ref[...])
        v1 = _with_ones(_side_by_side(pages, width, 2 * width), 0)
        _state_refs_store(m_ref, acc_ref, _online_update1(state, _dot(lhs, rhs), v1, nt=True))

    @pl.when(s == pl.num_programs(1) - 1)
    def _():
        o_ref[0] = _unstack_heads(_finish1((m_ref[:, 0:1], acc_ref[...])), MOBA_HEADS).astype(o_ref.dtype)


def _moba_sample(q, new_bf, cache, layer, page_table, n_new):
    b, n_pages = page_table.shape
    page = cache.shape[3]
    g = PAGES_PER_STEP
    n_half = n_pages // g
    rows = MOBA_HEADS * SAMPLE_ROWS
    width = MOBA_HEADS * HEAD_DIM
    out_shape = (b, SAMPLE_ROWS, width)
    first = lambda s, gp: jnp.minimum(s, n_half - 1) * g + gp
    second = lambda s, gp: jnp.maximum(s - n_half, 0) * g + gp
    return _sample_call(
        functools.partial(_moba_sample_kernel, n_pages_step=g, n_new=n_new, past_len=n_pages * page),
        "moba_sample", page_table, 2 * n_half,
        [q, new_bf], [_seq_block(q.shape), _seq_block(new_bf.shape)],
        [cache] * (2 * g),
        _page_specs(layer, g, (1, 1, width, page), 0, first) + _page_specs(layer, g, (1, 1, 2 * width, page), 0, second),
        _seq_block(out_shape), jax.ShapeDtypeStruct(out_shape, MXU_DTYPE),
        [pltpu.VMEM((rows, LANES), _F32), pltpu.VMEM((rows, width + LANES), _F32), pltpu.VMEM((rows, LANES), _F32),
         pltpu.VMEM((width, LANES), _F32)])


def _rope_tables(pos):
    half = HEAD_DIM // 2
    inv = ROPE_THETA ** (-jnp.arange(half, dtype=_F32) / half)
    ang = pos.astype(_F32)[:, None] * inv[None, :]
    cos, sin = jnp.cos(ang), jnp.sin(ang)
    zero = jnp.zeros_like(sin)
    reps = LANES // HEAD_DIM
    return (jnp.tile(cos, (1, 2 * reps)), jnp.tile(jnp.concatenate([-sin, zero], axis=1), (1, reps)),
            jnp.tile(jnp.concatenate([zero, sin], axis=1), (1, reps)))


def _pack_w_in(w_in):
    d = w_in.shape[0]
    o1 = NSA_Q_COLS
    o2 = o1 + NSA_KV_COLS
    o3 = o2 + NSA_G_COLS
    o4 = o3 + SB_COLS
    q, kv, g, sb, mb = w_in[:, :o1], w_in[:, o1:o2], w_in[:, o2:o3], w_in[:, o3:o4], w_in[:, o4:]
    zero = jnp.zeros((d, HEAD_DIM), w_in.dtype)
    q_groups = []
    for hh in range(NSA_HEADS):
        w = q[:, hh * HEAD_DIM:(hh + 1) * HEAD_DIM]
        q_groups += [w, zero] if hh // NSA_GROUP == 0 else [zero, w]
    gate = jnp.pad(g, ((0, 0), (0, LANES - NSA_G_COLS)))
    return jnp.concatenate(q_groups + [kv, sb, mb, gate], axis=1).astype(MXU_DTYPE)


def _block_tables(t):
    key = jnp.arange(t)[:, None]
    j = jnp.arange(LANES)[None, :]
    eblk = jnp.where(key // SEL_BLOCK == j, -NEG, 0.0).astype(MXU_DTYPE)
    eblk_mb = jnp.where(key // MOBA_BLOCK == j, -NEG, 0.0).astype(MXU_DTYPE)
    i = jnp.arange(t // CMP_STRIDE)[:, None]
    ratio = SEL_BLOCK // CMP_STRIDE
    cover = ((i <= ratio * j + ratio - 1) & (i >= ratio * j - (CMP_LEN // CMP_STRIDE - 1))).astype(MXU_DTYPE)
    return eblk, eblk_mb, cover


def _layer_weights(l, norm_g, ffn_w_gate, ffn_w_up, ffn_w_down, w_in, nsa_cmp_pe, nsa_cmp_w1, nsa_cmp_w2,
                   w_branch_nsa, w_branch_sb, w_branch_moba, w_merge_gate, w_out, w_ple_proj, w_ple_gate):
    c = lambda w: w.astype(MXU_DTYPE)
    d, f = ffn_w_gate.shape[2:]
    tf = _ff_tile(f)
    cols = lambda w: c(w).reshape(d, f // tf, tf).transpose(1, 0, 2)
    return dict(
        g=[norm_g[l, i][None, :] for i in range(norm_g.shape[1])],
        ffn=[(cols(ffn_w_gate[l, i]), cols(ffn_w_up[l, i]), c(ffn_w_down[l, i]).reshape(f // tf, tf, d))
             for i in range(2)],
        w_all=_pack_w_in(w_in[l]),
        cw=_compress_weights(nsa_cmp_pe[l], nsa_cmp_w1[l], nsa_cmp_w2[l]),
        merge=(c(w_branch_nsa[l]), c(w_branch_sb[l]), c(w_branch_moba[l]), c(w_merge_gate[l]), c(w_out[l])),
        ple=(c(w_ple_gate[l]), c(w_ple_proj[l])),
    )


def _token_tile(n, cap):
    tm = cap
    while n % tm:
        tm //= 2
    return tm


def _ff_tile(f, cap=512):
    best = LANES
    for k in range(1, f // LANES + 1):
        if f % (k * LANES) == 0 and k * LANES <= cap:
            best = k * LANES
    return best


def _prompt_layer(h, p_l, lw, tabs, consts, b, t):
    n, d = h.shape
    tm = _token_tile(t, 512)
    tm_ffn = _token_tile(t, 1024)
    tk = 4 * LANES
    g = lw["g"]
    h = _ffn(h, g[0], g[1], *lw["ffn"][0], tm=tm_ffn)
    per_seq = t // tm
    pr = _inproj(h, g[2], lw["w_all"], tabs, lambda i: (i % per_seq, 0), tm=tm)
    pr = {k: v.reshape(b, t, v.shape[1]) for k, v in pr.items()}
    eblk, eblk_mb, cover = consts
    cmp_k, cmp_v = _compress_prompt(pr["nsa"], lw["cw"])
    o_nsa = _nsa_prompt(pr["q_nsa"], pr["gates"], cmp_k, cmp_v, pr["nsa_bf"], pr["win_bf"], eblk, cover,
                        tq=LANES, tk=tk)
    o_sb = _sb_prompt(pr["q_sb"], pr["sb_bf"], tq=2 * LANES, tk=LANES)
    kmean = _kmean_prompt(pr["mb"])
    kmean = jnp.pad(kmean, ((0, 0), (0, LANES - kmean.shape[1]), (0, 0)))
    o_mb = _moba_prompt(pr["q_mb"], kmean, pr["mb_bf"], eblk_mb, tk=tk)
    flat = lambda o: o.reshape(n, o.shape[2])
    h = _merge(h, g[2], g[3], flat(o_nsa), flat(o_sb), flat(o_mb), *lw["merge"], tm=tm)
    h = _ffn(h, g[4], g[5], *lw["ffn"][1], ple=(p_l, *lw["ple"], g[6]), tm=tm_ffn)
    return h, (pr["nsa"], pr["sb"], pr["mb"], pr["win"])


def _sample_layer(h, p_l, lw, tabs, cover, caches, layer, page_table, n_new):
    n, d = h.shape
    b = page_table.shape[0]
    tm = _token_tile(n, 512)
    g = lw["g"]
    cache_nsa, cache_sb, cache_mb, cache_win = caches
    h = _ffn(h, g[0], g[1], *lw["ffn"][0], tm=tm)
    pr = _inproj(h, g[2], lw["w_all"], tabs, lambda i: (i, 0), tm=tm)
    pr = {k: v.reshape(b, SAMPLE_ROWS, v.shape[1]) for k, v in pr.items()}
    cmp_k, cmp_v = _compress_sample(cache_nsa, layer, page_table, lw["cw"])
    o_nsa = _nsa_sample(pr["q_nsa"], pr["gates"], cmp_k, cmp_v, pr["nsa_bf"], pr["win_bf"], cache_nsa, cache_win,
                        layer, page_table, cover, n_new)
    o_sb = _sb_sample(pr["q_sb"], pr["sb_bf"], cache_sb, layer, page_table, n_new)
    o_mb = _moba_sample(pr["q_mb"], pr["mb_bf"], cache_mb, layer, page_table, n_new)
    flat = lambda o: o.reshape(n, o.shape[2])
    h = _merge(h, g[2], g[3], flat(o_nsa), flat(o_sb), flat(o_mb), *lw["merge"], tm=tm)
    h = _ffn(h, g[4], g[5], *lw["ffn"][1], ple=(p_l, *lw["ple"], g[6]), tm=tm)
    return h, tuple(pr[k][:, :n_new] for k in ("nsa", "sb", "mb", "win"))


def kernel(x_prompt, x_sample, p_prompt, p_sample, cache_nsa, cache_sb, cache_moba, cache_win, page_table,
           norm_g, ffn_w_gate, ffn_w_up, ffn_w_down, w_in, nsa_cmp_pe, nsa_cmp_w1, nsa_cmp_w2,
           w_branch_nsa, w_branch_sb, w_branch_moba, w_merge_gate, w_out, w_ple_proj, w_ple_gate):
    depth = norm_g.shape[0]
    b, t, d = x_prompt.shape
    bs, ts, _ = x_sample.shape
    n_pages = page_table.shape[1]
    page = cache_nsa.shape[2]
    past_len = n_pages * page
    assert ts <= SAMPLE_ROWS and t % (4 * LANES) == 0 and t >= WINDOW + LANES
    assert t // SEL_BLOCK <= LANES and past_len // SEL_BLOCK <= LANES
    assert n_pages % PAGES_PER_STEP == 0 and MOBA_BLOCK % page == 0 and cache_win.shape[2] == WINDOW
    weights = (norm_g, ffn_w_gate, ffn_w_up, ffn_w_down, w_in, nsa_cmp_pe, nsa_cmp_w1, nsa_cmp_w2,
               w_branch_nsa, w_branch_sb, w_branch_moba, w_merge_gate, w_out, w_ple_proj, w_ple_gate)
    tabs_p = _rope_tables(jnp.arange(t, dtype=jnp.int32))
    tabs_s = _rope_tables(past_len + jnp.arange(bs * SAMPLE_ROWS, dtype=jnp.int32) % SAMPLE_ROWS)
    consts = _block_tables(t)
    cover_s = _block_tables(past_len)[2]
    pad_rows = lambda x: jnp.pad(x, ((0, 0), (0, SAMPLE_ROWS - ts), (0, 0))).reshape(bs * SAMPLE_ROWS, -1)
    feature_major = lambda c: jnp.transpose(c, (0, 1, 3, 4, 5, 2)).reshape(c.shape[0], c.shape[1], -1, c.shape[2])
    caches = tuple(feature_major(c) for c in (cache_nsa, cache_sb, cache_moba, cache_win))
    h_p = x_prompt.reshape(b * t, d)
    h_s = pad_rows(x_sample)
    st_p, st_s = [], []
    for l in range(depth):
        lw = _layer_weights(l, *weights)
        h_p, rows = _prompt_layer(h_p, p_prompt[l].reshape(b * t, -1), lw, tabs_p, consts, b, t)
        st_p.append(rows)
        h_s, rows = _sample_layer(h_s, pad_rows(p_sample[l]), lw, tabs_s, cover_s, caches, l, page_table, ts)
        win_fm = jnp.concatenate([caches[3][l][:, :, ts:], jnp.swapaxes(rows[3], 1, 2)], axis=2)
        st_s.append(rows[:3] + (jnp.swapaxes(win_fm, 1, 2),))
    keep = min(WINDOW, t)
    heads = ((4, NSA_KV_HEADS), (2, SB_HEADS), (2, MOBA_HEADS))
    out = [h_p.reshape(b, t, d), h_s.reshape(bs, SAMPLE_ROWS, d)[:, :ts]]
    for i, (parts, nh) in enumerate(heads):
        out.append(jnp.stack([s[i] for s in st_p]).reshape(depth, b, t, parts, nh, HEAD_DIM))
        out.append(jnp.stack([s[i] for s in st_s]).reshape(depth, bs, ts, parts, nh, HEAD_DIM))
    out.append(jnp.stack([s[3][:, t - keep:] for s in st_p]).reshape(depth, b, keep, 2, NSA_KV_HEADS, HEAD_DIM))
    out.append(jnp.stack([s[3] for s in st_s]).reshape(depth, bs, -1, 2, NSA_KV_HEADS, HEAD_DIM))
    return tuple(out)
```

```python
import functools

import jax
import jax.numpy as jnp
from jax import lax
from jax.experimental import pallas as pl
from jax.experimental.pallas import tpu as pltpu

HEAD_DIM = 64
NSA_HEADS = 8
NSA_KV_HEADS = 2
NSA_GROUP = NSA_HEADS // NSA_KV_HEADS
SB_HEADS = 4
MOBA_HEADS = 4
CMP_LEN = 32
CMP_STRIDE = 16
SEL_BLOCK = 64
SEL_TOPN = 16
WINDOW = 512
MOBA_BLOCK = 256
MOBA_TOPK = 3
ROPE_THETA = 10000.0
RMS_EPS = 1e-6
NEG = -1e30
FORCE_BONUS = 1e4
SB_DEAD = -104.0

LANES = 128
SUBLANES = 8
MXU_DTYPE = jnp.bfloat16
VMEM_LIMIT = 56 * 1024 * 1024
SAMPLE_ROWS = 8
PAGES_PER_STEP = 16
NSA_ROW_CHUNKS = 4

NSA_Q_COLS = NSA_HEADS * HEAD_DIM
NSA_KV_COLS = 6 * NSA_KV_HEADS * HEAD_DIM
NSA_G_COLS = 3 * NSA_HEADS
SB_COLS = 3 * SB_HEADS * HEAD_DIM
MOBA_COLS = 3 * MOBA_HEADS * HEAD_DIM

_F32 = jnp.float32


def _dot(a, b):
    return jnp.dot(a, b, preferred_element_type=_F32)


def _dot_nt(a, b):
    return lax.dot_general(a, b, (((1,), (1,)), ((), ())), preferred_element_type=_F32)


def _dot_split(x, w):
    hi = x.astype(MXU_DTYPE)
    mid = (x - hi.astype(_F32)).astype(MXU_DTYPE)
    return _dot(hi, w) + _dot(mid, w)


def _rms(x, g):
    return x * lax.rsqrt(jnp.mean(x * x, axis=-1, keepdims=True) + RMS_EPS) * g


def _params(*sem):
    return pltpu.CompilerParams(dimension_semantics=sem, vmem_limit_bytes=VMEM_LIMIT)


def _full(shape):
    n = len(shape)
    return pl.BlockSpec(shape, lambda *_: (0,) * n)


def _ffn_kernel(*refs, with_ple):
    if with_ple:
        (h_ref, gpre_ref, gpost_ref, wg_ref, wu_ref, wd_ref,
         p_ref, wpg_ref, wpp_ref, gple_ref, o_ref, xn_ref, acc_ref) = refs
    else:
        h_ref, gpre_ref, gpost_ref, wg_ref, wu_ref, wd_ref, o_ref, xn_ref, acc_ref = refs
    xn_ref[...] = _rms(h_ref[...], gpre_ref[...]).astype(MXU_DTYPE)
    acc_ref[...] = jnp.zeros_like(acc_ref)

    def chunk(j, carry):
        xn = xn_ref[...]
        g = _dot(xn, wg_ref[j])
        u = _dot(xn, wu_ref[j])
        a = (g * jax.nn.sigmoid(g)) * u
        acc_ref[...] += _dot(a.astype(MXU_DTYPE), wd_ref[j])
        return carry

    lax.fori_loop(0, wg_ref.shape[0], chunk, 0)
    h = h_ref[...] + 0.5 * _rms(acc_ref[...], gpost_ref[...])
    if with_ple:
        gate = jax.nn.sigmoid(_dot(h.astype(MXU_DTYPE), wpg_ref[...]))
        ple = gate * _dot(p_ref[...].astype(MXU_DTYPE), wpp_ref[...])
        h = h + _rms(ple, gple_ref[...])
    o_ref[...] = h


def _resident(shape):
    n = len(shape)
    return pl.BlockSpec(shape, lambda *_: (0,) * n, pipeline_mode=pl.Buffered(1))


def _ffn(h, g_pre, g_post, wg, wu, wd, ple=None, *, tm):
    n, d = h.shape
    row = lambda i: (i, 0)
    in_specs = [pl.BlockSpec((tm, d), row), _full((1, d)), _full((1, d)),
                _resident(wg.shape), _resident(wu.shape), _resident(wd.shape)]
    args = [h, g_pre, g_post, wg, wu, wd]
    if ple is not None:
        p, wpg, wpp, g_ple = ple
        in_specs += [pl.BlockSpec((tm, p.shape[1]), row), _resident(wpg.shape), _resident(wpp.shape), _full((1, d))]
        args += [p, wpg, wpp, g_ple]
    return pl.pallas_call(
        functools.partial(_ffn_kernel, with_ple=ple is not None),
        grid=(n // tm,),
        in_specs=in_specs,
        out_specs=pl.BlockSpec((tm, d), row),
        out_shape=jax.ShapeDtypeStruct((n, d), _F32),
        scratch_shapes=[pltpu.VMEM((tm, d), MXU_DTYPE), pltpu.VMEM((tm, d), _F32)],
        compiler_params=_params("parallel"),
        name="ffn_ple" if ple is not None else "ffn",
    )(*args)


_G_QNSA = 0
_G_KV = 8
_G_SB = 14
_G_MB = 20
_G_GATE = 26
_ROPE_GROUPS = tuple(range(8)) + (8, 10, 12, 20, 21, 22, 23)
_INPROJ_OUT = (("q_nsa", _G_QNSA, 8, True, None), ("nsa", _G_KV, 4, False, _F32), ("win", _G_KV + 4, 2, False, _F32),
               ("q_sb", _G_SB, 2, True, None), ("sb", _G_SB + 2, 4, False, _F32),
               ("q_mb", _G_MB, 2, False, _F32), ("mb", _G_MB + 2, 4, False, _F32))
_INPROJ_NARROW = ("nsa", "win", "sb", "mb")


_INPROJ_STATE = ("nsa", "sb", "mb")


def _inproj_kernel(h_ref, g_ref, w_ref, cos_ref, sa_ref, sb_ref, *refs, n_carried, feature_major):
    out_refs = refs[n_carried:]
    n_stacked = feature_major
    xn = _rms(h_ref[...], g_ref[...]).astype(MXU_DTYPE)
    cos, sa, sb = cos_ref[...], sa_ref[...], sb_ref[...]
    scale = HEAD_DIM ** -0.5
    tm = xn.shape[0]

    def group(gi):
        y = _dot(xn, w_ref[:, gi * LANES:(gi + 1) * LANES])
        if gi in _ROPE_GROUPS:
            y = y * cos + pltpu.roll(y, LANES - HEAD_DIM // 2, 1) * sa + pltpu.roll(y, HEAD_DIM // 2, 1) * sb
        return y

    outs = out_refs[:len(_INPROJ_OUT)]
    narrow = dict(zip(_INPROJ_NARROW, out_refs[len(_INPROJ_OUT):]))
    gate_ref = out_refs[len(_INPROJ_OUT) + len(_INPROJ_NARROW)]
    means = []
    for (name, g0, n_groups, scaled, _), o_ref in zip(_INPROJ_OUT, outs):
        for i in range(n_groups):
            y = group(g0 + i)
            if scaled:
                y = y * scale
            if n_stacked and name in _INPROJ_STATE:
                o_ref[0, 0, i * LANES:(i + 1) * LANES, :] = y.T
            else:
                o_ref[:, i * LANES:(i + 1) * LANES] = y.astype(o_ref.dtype)
            if name in narrow:
                narrow[name][:, i * LANES:(i + 1) * LANES] = y.astype(MXU_DTYPE)
            if n_stacked and name == "nsa" and i < 2:
                out_refs[-2][:, i * LANES:(i + 1) * LANES] = y
            if n_stacked and name == "mb" and i < 2:
                means.append([jnp.mean(y[r:r + MOBA_BLOCK], axis=0, keepdims=True) for r in range(0, tm, MOBA_BLOCK)])
    gate_ref[...] = jax.nn.sigmoid(group(_G_GATE))
    if n_stacked:
        rows = [jnp.concatenate([means[0][r], means[1][r]], axis=1) for r in range(tm // MOBA_BLOCK)]
        rows.append(jnp.zeros((SUBLANES - len(rows), 2 * LANES), _F32))
        out_refs[-1][...] = jnp.concatenate(rows, axis=0)


def _inproj(h, g, w_all, tabs, tab_map, *, tm, stack=None, carried=()):
    n, d = h.shape
    row = lambda i: (i, 0)
    tab_spec = pl.BlockSpec((tm, LANES), tab_map)
    sizes = {o[0]: o[2] * LANES for o in _INPROJ_OUT}
    names = [o[0] for o in _INPROJ_OUT] + [k + "_bf" for k in _INPROJ_NARROW] + ["gates"]
    shapes = [jax.ShapeDtypeStruct((n, o[2] * LANES), o[4] or MXU_DTYPE) for o in _INPROJ_OUT]
    shapes += [jax.ShapeDtypeStruct((n, sizes[k]), MXU_DTYPE) for k in _INPROJ_NARROW]
    shapes.append(jax.ShapeDtypeStruct((n, LANES), _F32))
    specs = [pl.BlockSpec((tm, s.shape[1]), row) for s in shapes]
    aliases = {}
    n_fixed_inputs = 6
    if stack is not None:
        depth, layer, b, t = stack
        per_seq = t // tm
        assert tm % MOBA_BLOCK == 0 and tm // MOBA_BLOCK <= SUBLANES and len(carried) in (0, len(_INPROJ_STATE))
        for k, name in enumerate(_INPROJ_STATE):
            idx = names.index(name)
            shapes[idx] = jax.ShapeDtypeStruct((depth, b, sizes[name], t), _F32)
            specs[idx] = pl.BlockSpec((1, 1, sizes[name], tm), lambda i: (layer, i // per_seq, 0, i % per_seq))
            if carried:
                aliases[n_fixed_inputs + k] = idx
        names += ["cmp_rows", "kmean"]
        shapes += [jax.ShapeDtypeStruct((n, 2 * LANES), _F32),
                   jax.ShapeDtypeStruct((n // tm * SUBLANES, 2 * LANES), _F32)]
        specs += [pl.BlockSpec((tm, 2 * LANES), row), pl.BlockSpec((SUBLANES, 2 * LANES), row)]
    outs = pl.pallas_call(
        functools.partial(_inproj_kernel, n_carried=len(carried), feature_major=stack is not None),
        grid=(n // tm,),
        in_specs=[pl.BlockSpec((tm, d), row), _full((1, d)), _full(w_all.shape), tab_spec, tab_spec, tab_spec]
        + [pl.BlockSpec(memory_space=pl.ANY)] * len(carried),
        out_specs=specs,
        out_shape=shapes,
        input_output_aliases=aliases,
        compiler_params=_params("parallel"),
        name="inproj",
    )(h, g, w_all, *tabs, *carried)
    return dict(zip(names, outs))


def _merge_kernel(h_ref, gpre_ref, gpost_ref, onsa_ref, osb_ref, omb_ref,
                  wbn_ref, wbs_ref, wbm_ref, wgate_ref, wout_ref, o_ref):
    d = h_ref.shape[1]
    h = h_ref[...]
    u = _rms(h, gpre_ref[...]).astype(MXU_DTYPE)
    branches = (_dot(onsa_ref[...], wbn_ref[...]), _dot(osb_ref[...], wbs_ref[...]), _dot(omb_ref[...], wbm_ref[...]))
    mix = None
    for c, br in enumerate(branches):
        term = jax.nn.sigmoid(_dot(u, wgate_ref[:, c * d:(c + 1) * d])) * br
        mix = term if mix is None else mix + term
    y = _dot(mix.astype(MXU_DTYPE), wout_ref[...])
    o_ref[...] = h + _rms(y, gpost_ref[...])


def _merge(h, g_pre, g_post, o_nsa, o_sb, o_mb, wbn, wbs, wbm, wgate, wout, *, tm):
    n, d = h.shape
    row = lambda i: (i, 0)
    return pl.pallas_call(
        _merge_kernel,
        grid=(n // tm,),
        in_specs=[pl.BlockSpec((tm, d), row), _full((1, d)), _full((1, d)),
                  pl.BlockSpec((tm, o_nsa.shape[1]), row), pl.BlockSpec((tm, o_sb.shape[1]), row),
                  pl.BlockSpec((tm, o_mb.shape[1]), row),
                  _full(wbn.shape), _full(wbs.shape), _full(wbm.shape), _full(wgate.shape), _full(wout.shape)],
        out_specs=pl.BlockSpec((tm, d), row),
        out_shape=jax.ShapeDtypeStruct((n, d), _F32),
        compiler_params=_params("parallel"),
        name="merge",
    )(h, g_pre, g_post, o_nsa, o_sb, o_mb, wbn, wbs, wbm, wgate, wout)


def _lane_lo(shape=(1, LANES)):
    return (lax.broadcasted_iota(jnp.int32, shape, len(shape) - 1) % LANES) < HEAD_DIM


def _softplus(z):
    return jnp.maximum(z, 0.0) + jnp.log1p(jnp.exp(-jnp.abs(z)))


def _cumsum_matrix(tk):
    j = lax.broadcasted_iota(jnp.int32, (tk, 2 * tk), 0)
    s = lax.broadcasted_iota(jnp.int32, (tk, 2 * tk), 1)
    return jnp.where((j > s) | (s >= tk), 1.0, 0.0).astype(MXU_DTYPE)


def _online_update1(state, z, v1, nt=False):
    m, acc = state
    m_new = jnp.maximum(m, jnp.max(z, axis=-1, keepdims=True))
    p = jnp.exp(z - m_new).astype(MXU_DTYPE)
    return m_new, jnp.exp(m - m_new) * acc + (_dot_nt(p, v1) if nt else _dot(p, v1))


def _online_init1(rows, width=LANES):
    return jnp.full((rows, 1), NEG, _F32), jnp.zeros((rows, width + LANES), _F32)


def _finish1(state):
    _, acc = state
    width = acc.shape[1] - LANES
    den = 1.0 / acc[:, width:]
    return acc[:, :width] * jnp.concatenate([den] * (width // LANES), axis=1)


def _with_ones(v, axis=1):
    shape = (v.shape[0], LANES) if axis == 1 else (LANES, v.shape[1])
    return jnp.concatenate([v, jnp.ones(shape, v.dtype)], axis=axis)


def _seed_state(m, l, acc):
    return m, jnp.concatenate([acc, jnp.broadcast_to(l, (l.shape[0], LANES))], axis=1)


def _topk_mask(val, k, axis):
    idx = lax.broadcasted_iota(jnp.int32, val.shape, axis).astype(_F32)
    sel = jnp.zeros(val.shape, _F32)
    for _ in range(k):
        m = jnp.max(val, axis=axis, keepdims=True)
        first = jnp.min(jnp.where(val == m, idx, 1e9), axis=axis, keepdims=True)
        hit = idx == first
        sel = jnp.where(hit, 1.0, sel)
        val = jnp.where(hit, -jnp.inf, val)
    return sel


def _stack_nsa_queries(q):
    return jnp.concatenate([q[:, hh * LANES:(hh + 1) * LANES] for hh in range(NSA_HEADS)], axis=0)


def _cmp_branch(qs, kc, vc, qpos, tq):
    nc = kc.shape[0]
    s = _dot_nt(qs, kc)
    cend = lax.broadcasted_iota(jnp.int32, (1, nc), 1) * CMP_STRIDE + (CMP_LEN - 1)
    vis = cend <= qpos
    sm = jnp.where(vis, s, NEG)
    m = jnp.max(sm, axis=-1, keepdims=True)
    e = jnp.where(vis, jnp.exp(sm - m), 0.0)
    l = jnp.sum(e, axis=-1, keepdims=True)
    p = e * (1.0 / jnp.where(l > 0.0, l, 1.0))
    o_cmp = _dot(p.astype(MXU_DTYPE), vc)
    psums = []
    for kv in range(NSA_KV_HEADS):
        blocks = [p[(kv * NSA_GROUP + g) * tq:(kv * NSA_GROUP + g + 1) * tq] for g in range(NSA_GROUP)]
        psums.append(functools.reduce(lambda a, b: a + b, blocks))
    return o_cmp, psums


def _nsa_select(imp, qp, n_blocks, n_top, axis):
    jblk = lax.broadcasted_iota(jnp.int32, imp.shape, axis)
    cur = qp // SEL_BLOCK
    forced = (jblk == 0) | (jblk == cur) | (jblk == cur - 1)
    valid = (jblk * SEL_BLOCK <= qp) & (jblk < n_blocks)
    val = jnp.where(valid, imp + jnp.where(forced, FORCE_BONUS, 0.0), NEG)
    return jnp.where(valid, _topk_mask(val, n_top, axis), 0.0) - 1.0


def _nsa_lhs(qs, biases):
    bias = jnp.concatenate([b for b in biases for _ in range(NSA_GROUP)], axis=0)
    return jnp.concatenate([qs, bias.astype(qs.dtype)], axis=1)


def _combine_nsa(gates, o_cmp, o_sel, o_win, tq):
    lo = _lane_lo()
    heads = []
    for hh in range(NSA_HEADS):
        r = slice(hh * tq, (hh + 1) * tq)
        o = (gates[:, 3 * hh:3 * hh + 1] * o_cmp[r] + gates[:, 3 * hh + 1:3 * hh + 2] * o_sel[r]
             + gates[:, 3 * hh + 2:3 * hh + 3] * o_win[r])
        if hh % 2 != hh // NSA_GROUP:
            o = pltpu.roll(o, HEAD_DIM, 1)
        heads.append(o)
    return jnp.concatenate([jnp.where(lo, heads[2 * i], heads[2 * i + 1]) for i in range(NSA_HEADS // 2)], axis=1)


def _moba_pick(s, own, axis):
    past = lax.broadcasted_iota(jnp.int32, s.shape, axis) < own
    return jnp.where(past, _topk_mask(jnp.where(past, s, NEG), MOBA_TOPK, axis), 0.0) - 1.0


def _sb_block(qh, kt, vt, cum, carry, keep):
    c, o = carry
    tk = kt.shape[0]
    z = _dot_nt(qh, kt)
    lk = -_softplus(z)
    if keep is not None:
        lk = jnp.where(keep, lk, 0.0)
    r = _dot_split(lk, cum)
    a = jnp.exp(lk + z + r[:, :tk] + c)
    if keep is not None:
        a = jnp.where(keep, a, 0.0)
    return c + r[:, tk:], o + _dot(a.astype(MXU_DTYPE), vt)


def _stage_chunks(src_ref, lead, n_chunks, dst_ref, c0):
    for r in range(CMP_STRIDE):
        x = src_ref[lead + (pl.ds(r, n_chunks, stride=CMP_STRIDE), slice(None))]
        dst_ref[pl.ds(c0, n_chunks), r * LANES:(r + 1) * LANES] = x


def _compress_finish(xk_ref, xv_ref, wk_ref, wv_ref, pe_ref, w2_ref, ok_ref, ov_ref):
    nc = xk_ref.shape[0]
    for x_ref, w_ref, t, o_ref in ((xk_ref, wk_ref, 0, ok_ref), (xv_ref, wv_ref, 1, ov_ref)):
        ab = _dot(x_ref[...].astype(MXU_DTYPE), w_ref[...])
        cst = _dot(pe_ref[t].astype(MXU_DTYPE), w_ref[...])
        for h in range(NSA_KV_HEADS):
            a = ab[:, h * 256:h * 256 + LANES]
            b = ab[:, h * 256 + LANES:(h + 1) * 256]
            c = cst[0:1, h * 256:h * 256 + LANES] + cst[1:2, h * 256 + LANES:(h + 1) * 256]
            hid = jax.nn.gelu(a + pltpu.roll(b, nc - 1, 0) + c)
            part = _dot(hid.astype(MXU_DTYPE), w2_ref[t, h])
            out = part if h == 0 else out + part
        o_ref[0] = out.astype(o_ref.dtype)


def _compress_prompt_kernel(rk_ref, rv_ref, wk_ref, wv_ref, pe_ref, w2_ref, ok_ref, ov_ref, xk_ref, xv_ref):
    n_chunks = rk_ref.shape[1] // CMP_STRIDE
    _stage_chunks(rk_ref, (0,), n_chunks, xk_ref, 0)
    _stage_chunks(rv_ref, (0,), n_chunks, xv_ref, 0)
    _compress_finish(xk_ref, xv_ref, wk_ref, wv_ref, pe_ref, w2_ref, ok_ref, ov_ref)


def _compress_sample_kernel(pt_ref, wk_ref, wv_ref, pe_ref, w2_ref, *refs, n_pages_step):
    del pt_ref
    pages = refs[:n_pages_step]
    ok_ref, ov_ref, xk_ref, xv_ref, rows_ref = refs[n_pages_step:]
    s = pl.program_id(1)
    per_page = pages[0].shape[3] // CMP_STRIDE
    for gp, page in enumerate(pages):
        x = page[0, 0]
        rows_ref[0] = x[:LANES].T
        rows_ref[1] = x[LANES:].T
        c0 = pl.multiple_of((s * n_pages_step + gp) * per_page, per_page)
        _stage_chunks(rows_ref, (0,), per_page, xk_ref, c0)
        _stage_chunks(rows_ref, (1,), per_page, xv_ref, c0)

    @pl.when(s == pl.num_programs(1) - 1)
    def _():
        _compress_finish(xk_ref, xv_ref, wk_ref, wv_ref, pe_ref, w2_ref, ok_ref, ov_ref)


def _compress_weights(pe, w1, w2):
    eye = jnp.eye(NSA_KV_HEADS, dtype=w1.dtype)
    big, pes = [], []
    for t in range(2):
        w1r = w1[t].reshape(2, CMP_STRIDE, HEAD_DIM, w1.shape[-1])
        wb = jnp.einsum("hH,ardn->rhdHan", eye, w1r)
        big.append(wb.reshape(CMP_STRIDE * NSA_KV_HEADS * HEAD_DIM, -1).astype(MXU_DTYPE))
        per = pe[t].reshape(2, CMP_STRIDE, 1, HEAD_DIM)
        rows = jnp.broadcast_to(per, (2, CMP_STRIDE, NSA_KV_HEADS, HEAD_DIM)).reshape(2, -1)
        pes.append(jnp.concatenate([rows, jnp.zeros((SUBLANES - 2, rows.shape[1]), rows.dtype)], axis=0))
    zero = jnp.zeros_like(w2)
    w2p = jnp.stack([jnp.concatenate([w2, zero], axis=-1), jnp.concatenate([zero, w2], axis=-1)], axis=1)
    return big[0], big[1], jnp.stack(pes), w2p.astype(MXU_DTYPE)


def _compress_out(b, nc):
    shape = jax.ShapeDtypeStruct((b, nc, LANES), MXU_DTYPE)
    return [shape, shape]


def _compress_prompt(nsa_rows, cw):
    b, t, _ = nsa_rows.shape
    nc = t // CMP_STRIDE
    wk, wv, pe2, w2p = cw
    out_spec = pl.BlockSpec((1, nc, LANES), lambda i: (i, 0, 0))
    return pl.pallas_call(
        _compress_prompt_kernel,
        grid=(b,),
        in_specs=[pl.BlockSpec((1, t, LANES), lambda i: (i, 0, 0)),
                  pl.BlockSpec((1, t, LANES), lambda i: (i, 0, 1)),
                  _full(wk.shape), _full(wv.shape), _full(pe2.shape), _full(w2p.shape)],
        out_specs=[out_spec, out_spec],
        out_shape=_compress_out(b, nc),
        scratch_shapes=[pltpu.VMEM((nc, CMP_STRIDE * LANES), _F32)] * 2,
        compiler_params=_params("arbitrary"),
        name="nsa_compress_prompt",
    )(nsa_rows, nsa_rows, wk, wv, pe2, w2p)


def _nsa_prompt_kernel(q_ref, g_ref, ck_ref, cv_ref, sk_ref, sv_ref, wk_ref, wv_ref, eblk_ref, cover_ref,
                       o_ref, *, tq, tk, n_top):
    qi = pl.program_id(1)
    t0 = qi * tq
    rows = NSA_HEADS * tq
    n_blocks = sk_ref.shape[1] // SEL_BLOCK
    qs = _stack_nsa_queries(q_ref[0])
    qpos_r = t0 + lax.broadcasted_iota(jnp.int32, (rows, 1), 0) % tq

    o_cmp, psums = _cmp_branch(qs, ck_ref[0], cv_ref[0], qpos_r, tq)
    qp_t = t0 + lax.broadcasted_iota(jnp.int32, (1, tq), 1)
    biases = []
    for psum in psums:
        imp_t = _dot_split(psum, cover_ref[...]).T
        biases.append(_nsa_select(imp_t, qp_t, n_blocks, n_top, 0).T)
    lhs = _nsa_lhs(qs, biases)

    kcol = lax.broadcasted_iota(jnp.int32, (1, tk), 1)

    chunk = rows // NSA_ROW_CHUNKS
    lhs_c = [lhs[c * chunk:(c + 1) * chunk] for c in range(NSA_ROW_CHUNKS)]
    qpos_c = [qpos_r[c * chunk:(c + 1) * chunk] for c in range(NSA_ROW_CHUNKS)]

    def sel_tile(kt, states, causal):
        start = pl.multiple_of(kt * tk, tk)
        rhs = jnp.concatenate([sk_ref[0, pl.ds(start, tk), :], eblk_ref[pl.ds(start, tk), :]], axis=1)
        v1 = _with_ones(sv_ref[0, pl.ds(start, tk), :])
        out = []
        for c in range(NSA_ROW_CHUNKS):
            z = _dot_nt(lhs_c[c], rhs)
            if causal:
                z = jnp.where(start + kcol <= qpos_c[c], z, NEG)
            out.append(_online_update1(states[c], z, v1))
        return tuple(out)

    n_full = t0 // tk
    init = tuple(_online_init1(chunk) for _ in range(NSA_ROW_CHUNKS))
    states = lax.fori_loop(0, n_full, lambda kt, st: sel_tile(kt, st, False), init)
    o_sel = jnp.concatenate([_finish1(st) for st in sel_tile(n_full, states, True)], axis=0)

    n_band = WINDOW + tq
    ws = pl.multiple_of(jnp.maximum(t0 - WINDOW, 0), tq)
    z = _dot_nt(qs, wk_ref[0, pl.ds(ws, n_band), :])
    dist = qpos_r - (ws + lax.broadcasted_iota(jnp.int32, (1, n_band), 1))
    z = jnp.where((dist >= 0) & (dist <= WINDOW), z, NEG)
    p = jnp.exp(z - jnp.max(z, axis=-1, keepdims=True))
    o_win = _dot(p.astype(MXU_DTYPE), wv_ref[0, pl.ds(ws, n_band), :]) * (1.0 / jnp.sum(p, axis=-1, keepdims=True))

    o_ref[0] = _combine_nsa(g_ref[0], o_cmp, o_sel, o_win, tq).astype(o_ref.dtype)


def _nsa_prompt(q_nsa, gates, cmp_k, cmp_v, nsa_bf, win_bf, eblk, cover, *, tq, tk):
    b, t, _ = q_nsa.shape
    nc = cmp_k.shape[1]
    n_top = min(SEL_TOPN, t // SEL_BLOCK)
    cmp_spec = pl.BlockSpec((1, nc, LANES), lambda i, qi: (i, 0, 0))
    col = lambda g: pl.BlockSpec((1, t, LANES), lambda i, qi: (i, 0, g))
    return pl.pallas_call(
        functools.partial(_nsa_prompt_kernel, tq=tq, tk=tk, n_top=n_top),
        grid=(b, t // tq),
        in_specs=[pl.BlockSpec((1, tq, NSA_HEADS * LANES), lambda i, qi: (i, qi, 0)),
                  pl.BlockSpec((1, tq, LANES), lambda i, qi: (i, qi, 0)),
                  cmp_spec, cmp_spec, col(2), col(3), col(0), col(1),
                  _full(eblk.shape), _full(cover.shape)],
        out_specs=pl.BlockSpec((1, tq, NSA_Q_COLS), lambda i, qi: (i, qi, 0)),
        out_shape=jax.ShapeDtypeStruct((b, t, NSA_Q_COLS), MXU_DTYPE),
        compiler_params=_params("parallel", "arbitrary"),
        name="nsa_prompt",
    )(q_nsa, gates, cmp_k, cmp_v, nsa_bf, nsa_bf, win_bf, win_bf, eblk, cover)


def _sb_prompt_kernel(q_ref, k_ref, v_ref, o_ref, *, tq, tk):
    qi = pl.program_id(2)
    lo = _lane_lo()
    q = q_ref[0]
    zero_q = jnp.zeros_like(q)
    qh = (jnp.where(lo, q, zero_q), jnp.where(lo, zero_q, q))
    cum = _cumsum_matrix(tk)
    row = lax.broadcasted_iota(jnp.int32, (tq, tk), 0)
    col = lax.broadcasted_iota(jnp.int32, (tq, tk), 1)
    per_tile = tq // tk

    def blocks(kb, carry, keep):
        start = pl.multiple_of(kb * tk, tk)
        kt, vt = k_ref[0, pl.ds(start, tk), :], v_ref[0, pl.ds(start, tk), :]
        return tuple(_sb_block(qh[h], kt, vt, cum, carry[h], keep) for h in range(2))

    def alive(carry):
        return (jnp.max(jnp.maximum(carry[0][0], carry[1][0])) > SB_DEAD).astype(jnp.int32)

    zero = jnp.zeros((tq, LANES), _F32)
    carry = ((zero, zero), (zero, zero))
    for d in reversed(range(per_tile)):
        carry = blocks(qi * per_tile + d, carry, d * tk + col < row)

    def body(st):
        i, _, carry = st
        carry = blocks(qi * per_tile - 1 - i, carry, None)
        return i + 1, alive(carry), carry

    n_past = qi * per_tile
    _, _, carry = lax.while_loop(lambda st: (st[0] < n_past) & (st[1] > 0), body, (0, alive(carry), carry))
    o_ref[0] = jnp.where(lo, carry[0][1], carry[1][1]).astype(o_ref.dtype)


def _sb_prompt(q_sb, sb_bf, *, tq, tk):
    b, t, _ = q_sb.shape
    n_pair = SB_HEADS // 2
    return pl.pallas_call(
        functools.partial(_sb_prompt_kernel, tq=tq, tk=tk),
        grid=(b, n_pair, t // tq),
        in_specs=[pl.BlockSpec((1, tq, LANES), lambda i, p, qi: (i, qi, p)),
                  pl.BlockSpec((1, t, LANES), lambda i, p, qi: (i, 0, p)),
                  pl.BlockSpec((1, t, LANES), lambda i, p, qi: (i, 0, n_pair + p))],
        out_specs=pl.BlockSpec((1, tq, LANES), lambda i, p, qi: (i, qi, p)),
        out_shape=jax.ShapeDtypeStruct((b, t, SB_HEADS * HEAD_DIM), MXU_DTYPE),
        compiler_params=_params("parallel", "parallel", "arbitrary"),
        name="sb_prompt",
    )(q_sb, sb_bf, sb_bf)


def _moba_prompt_kernel(q_ref, km_ref, k_ref, v_ref, eblk_ref, o_ref, *, tq, tk):
    qi = pl.program_id(2)
    lo = _lane_lo()
    q = q_ref[0]
    scale = HEAD_DIM ** -0.5
    row = lax.broadcasted_iota(jnp.int32, (tq, tq), 0)
    col = lax.broadcasted_iota(jnp.int32, (tq, tq), 1)
    own0 = pl.multiple_of(qi * tq, tq)
    k_own, v_own = k_ref[0, pl.ds(own0, tq), :], _with_ones(v_ref[0, pl.ds(own0, tq), :])
    lhs, states = [], []
    for hh in range(2):
        qf = jnp.where(lo if hh == 0 else jnp.logical_not(lo), q, 0.0)
        gate = lax.dot_general(qf, km_ref[0], (((1,), (1,)), ((), ())), preferred_element_type=_F32,
                               precision=lax.Precision.HIGHEST)
        bias = _moba_pick(gate.T, qi, 0).T
        qh = (qf * scale).astype(MXU_DTYPE)
        lhs.append(jnp.concatenate([qh, bias.astype(MXU_DTYPE)], axis=1))
        z = jnp.where(col <= row, _dot_nt(qh, k_own), NEG)
        states.append(_online_update1(_online_init1(tq), z, v_own))

    def tile(j, states):
        start = pl.multiple_of(j * tk, tk)
        rhs = jnp.concatenate([k_ref[0, pl.ds(start, tk), :], eblk_ref[pl.ds(start, tk), :]], axis=1)
        v1 = _with_ones(v_ref[0, pl.ds(start, tk), :])
        return tuple(_online_update1(st, _dot_nt(l, rhs), v1) for l, st in zip(lhs, states))

    states = lax.fori_loop(0, (qi * tq + tk - 1) // tk, tile, tuple(states))
    o_ref[0] = jnp.where(lo, _finish1(states[0]), _finish1(states[1])).astype(o_ref.dtype)


def _moba_prompt(q_mb, kmean, mb_bf, eblk, *, tk):
    b, t, _ = q_mb.shape
    tq = MOBA_BLOCK
    n_pair = MOBA_HEADS // 2
    return pl.pallas_call(
        functools.partial(_moba_prompt_kernel, tq=tq, tk=tk),
        grid=(b, n_pair, t // tq),
        in_specs=[pl.BlockSpec((1, tq, LANES), lambda i, p, qi: (i, qi, p)),
                  pl.BlockSpec((1, LANES, LANES), lambda i, p, qi: (i, 0, p)),
                  pl.BlockSpec((1, t, LANES), lambda i, p, qi: (i, 0, p)),
                  pl.BlockSpec((1, t, LANES), lambda i, p, qi: (i, 0, n_pair + p)),
                  _full(eblk.shape)],
        out_specs=pl.BlockSpec((1, tq, LANES), lambda i, p, qi: (i, qi, p)),
        out_shape=jax.ShapeDtypeStruct((b, t, MOBA_HEADS * HEAD_DIM), MXU_DTYPE),
        compiler_params=_params("parallel", "parallel", "arbitrary"),
        name="moba_prompt",
    )(q_mb, kmean, mb_bf, mb_bf, eblk)


def _row_scores(q, keys, n):
    return [jnp.sum(q * keys[j:j + 1, :], axis=-1, keepdims=True) for j in range(n)]


def _new_key_softmax(q, keys, vals, n, visible):
    zs = [jnp.where(visible(j), z, NEG) for j, z in enumerate(_row_scores(q, keys, n))]
    m = functools.reduce(jnp.maximum, zs)
    l = jnp.zeros_like(m)
    acc = jnp.zeros((q.shape[0], vals.shape[1]), _F32)
    for j, z in enumerate(zs):
        p = jnp.exp(z - m)
        l = l + p
        acc = acc + p * vals[j:j + 1, :]
    return m, l, acc


def _stack_heads(q, n_heads):
    head = lax.broadcasted_iota(jnp.int32, (1, q.shape[1]), 1) // HEAD_DIM
    return jnp.concatenate([jnp.where(head == h, q, jnp.zeros_like(q)) for h in range(n_heads)], axis=0)


def _unstack_heads(o, n_heads):
    rows = o.shape[0] // n_heads
    head = lax.broadcasted_iota(jnp.int32, (1, o.shape[1]), 1) // HEAD_DIM
    out = jnp.zeros((rows, o.shape[1]), _F32)
    for h in range(n_heads):
        out = jnp.where(head == h, o[h * rows:(h + 1) * rows], out)
    return out


def _side_by_side(pages, f0, f1):
    return jnp.concatenate([p[0, 0, f0:f1, :] for p in pages], axis=1).astype(MXU_DTYPE)


def _key_block_onehot(n_keys, first_key, block):
    key_blk = (first_key + lax.broadcasted_iota(jnp.int32, (LANES, n_keys), 1)) // block
    return jnp.where(lax.broadcasted_iota(jnp.int32, (LANES, n_keys), 0) == key_blk, -NEG, 0.0).astype(MXU_DTYPE)


def _state_refs_store(m_ref, acc_ref, state):
    m_ref[...] = jnp.broadcast_to(state[0], m_ref.shape)
    acc_ref[...] = state[1]


def _page_specs(layer, n, block, row_block, page_of):
    def spec(gp):
        return pl.BlockSpec(block, lambda b, s, pt: (layer, pt[b, page_of(s, gp)], row_block, 0))
    return [spec(gp) for gp in range(n)]


def _sample_call(kernel_fn, name, page_table, n_steps, pre_args, pre_specs, page_arrays, page_specs, out_block,
                 out_shape, scratch):
    grid_spec = pltpu.PrefetchScalarGridSpec(
        num_scalar_prefetch=1,
        grid=(page_table.shape[0], n_steps),
        in_specs=pre_specs + page_specs,
        out_specs=out_block,
        scratch_shapes=scratch,
    )
    return pl.pallas_call(kernel_fn, grid_spec=grid_spec, out_shape=out_shape,
                          compiler_params=_params("arbitrary", "arbitrary"), name=name,
                          )(page_table, *pre_args, *page_arrays)


def _seq_block(shape):
    zeros = (0,) * (len(shape) - 1)
    return pl.BlockSpec((1,) + tuple(shape[1:]), lambda b, s, pt: (b,) + zeros)


def _const_block(shape):
    zeros = (0,) * len(shape)
    return pl.BlockSpec(tuple(shape), lambda b, s, pt: zeros)


def _compress_sample(cache, layer, page_table, cw):
    b, n_pages = page_table.shape
    page = cache.shape[3]
    nc = n_pages * page // CMP_STRIDE
    g = PAGES_PER_STEP
    wk, wv, pe2, w2p = cw
    out_spec = pl.BlockSpec((1, nc, LANES), lambda i, s, pt: (i, 0, 0))
    return _sample_call(
        functools.partial(_compress_sample_kernel, n_pages_step=g), "nsa_compress_sample", page_table, n_pages // g,
        [wk, wv, pe2, w2p], [_const_block(wk.shape), _const_block(wv.shape), _const_block(pe2.shape),
                             _const_block(w2p.shape)],
        [cache] * g, _page_specs(layer, g, (1, 1, 2 * LANES, page), 0, lambda s, gp: s * g + gp),
        [out_spec, out_spec], _compress_out(b, nc),
        [pltpu.VMEM((nc, CMP_STRIDE * LANES), _F32)] * 2 + [pltpu.VMEM((2, page, LANES), _F32)])


def _nsa_sample_kernel(pt_ref, q_ref, g_ref, ck_ref, cv_ref, new_ref, neww_ref, win_ref, cover_ref, *refs,
                       n_pages_step, n_new, past_len, n_top):
    del pt_ref
    pages = refs[:n_pages_step]
    o_ref, m_ref, acc_ref, bias_ref, ocmp_ref, owin_ref = refs[n_pages_step:]
    s = pl.program_id(1)
    tq = SAMPLE_ROWS
    rows = NSA_HEADS * tq
    qs = _stack_nsa_queries(q_ref[0])
    qi_r = lax.broadcasted_iota(jnp.int32, (rows, 1), 0) % tq
    qpos_r = past_len + qi_r

    @pl.when(s == 0)
    def _():
        qf = qs.astype(_F32)
        o_cmp, psums = _cmp_branch(qs, ck_ref[0], cv_ref[0], qpos_r, tq)
        ocmp_ref[...] = o_cmp
        qp = past_len + lax.broadcasted_iota(jnp.int32, (tq, 1), 0)
        biases = [_nsa_select(_dot_split(psum, cover_ref[...]), qp, past_len // SEL_BLOCK, n_top - 1, 1)
                  for psum in psums]
        bias_ref[...] = jnp.concatenate([b for b in biases for _ in range(NSA_GROUP)], axis=0)
        visible = lambda j: j <= qi_r
        new = new_ref[0].astype(_F32)
        state = _seed_state(*_new_key_softmax(qf, new[:, 2 * LANES:3 * LANES], new[:, 3 * LANES:], n_new, visible))
        _state_refs_store(m_ref, acc_ref, state)
        wnew = neww_ref[0].astype(_F32)
        wstate = _seed_state(*_new_key_softmax(qf, wnew[:, :LANES], wnew[:, LANES:], n_new, visible))
        w = win_ref[0, 0]
        n_win = w.shape[1]
        dist = qpos_r - (past_len - n_win + lax.broadcasted_iota(jnp.int32, (1, n_win), 1))
        z = jnp.where((dist >= 0) & (dist <= WINDOW), _dot(qs, w[:LANES].astype(MXU_DTYPE)), NEG)
        owin_ref[...] = _finish1(_online_update1(wstate, z, _with_ones(w[LANES:].astype(MXU_DTYPE), 0), nt=True))

    n_keys = n_pages_step * pages[0].shape[3]
    lhs = jnp.concatenate([qs, bias_ref[...].astype(MXU_DTYPE)], axis=1)
    rhs = jnp.concatenate([_side_by_side(pages, 0, LANES), _key_block_onehot(n_keys, s * n_keys, SEL_BLOCK)], axis=0)
    state = (m_ref[:, 0:1], acc_ref[...])
    state = _online_update1(state, _dot(lhs, rhs), _with_ones(_side_by_side(pages, LANES, 2 * LANES), 0), nt=True)
    _state_refs_store(m_ref, acc_ref, state)

    @pl.when(s == pl.num_programs(1) - 1)
    def _():
        o_sel = _finish1((m_ref[:, 0:1], acc_ref[...]))
        o_ref[0] = _combine_nsa(g_ref[0], ocmp_ref[...], o_sel, owin_ref[...], tq).astype(o_ref.dtype)


def _nsa_sample(q, gates, cmp_k, cmp_v, new_bf, new_win_bf, cache, cache_win, layer, page_table, cover, n_new):
    b, n_pages = page_table.shape
    page = cache.shape[3]
    past_len = n_pages * page
    g = PAGES_PER_STEP
    rows = NSA_HEADS * SAMPLE_ROWS
    n_top = min(SEL_TOPN, past_len // SEL_BLOCK + 1)
    win_spec = pl.BlockSpec((1, 1) + cache_win.shape[2:], lambda i, s, pt: (layer, i, 0, 0))
    out_shape = (b, SAMPLE_ROWS, NSA_Q_COLS)
    vmem = lambda w: pltpu.VMEM((rows, w), _F32)
    return _sample_call(
        functools.partial(_nsa_sample_kernel, n_pages_step=g, n_new=n_new, past_len=past_len, n_top=n_top),
        "nsa_sample", page_table, n_pages // g,
        [q, gates, cmp_k, cmp_v, new_bf, new_win_bf, cache_win, cover],
        [_seq_block(q.shape), _seq_block(gates.shape), _seq_block(cmp_k.shape), _seq_block(cmp_v.shape),
         _seq_block(new_bf.shape), _seq_block(new_win_bf.shape), win_spec, _const_block(cover.shape)],
        [cache] * g, _page_specs(layer, g, (1, 1, 2 * LANES, page), 1, lambda s, gp: s * g + gp),
        _seq_block(out_shape), jax.ShapeDtypeStruct(out_shape, MXU_DTYPE),
        [vmem(LANES), vmem(2 * LANES), vmem(LANES), vmem(LANES), vmem(LANES)])


def _sb_sample_kernel(pt_ref, q_ref, new_ref, *refs, n_pages_step, n_new):
    del pt_ref
    pages = refs[:n_pages_step]
    o_ref, c_ref, acc_ref = refs[n_pages_step:]
    s = pl.program_id(1)
    width = SB_HEADS * HEAD_DIM
    qs = _stack_heads(q_ref[0], SB_HEADS)
    rows = qs.shape[0]
    qi_r = lax.broadcasted_iota(jnp.int32, (rows, 1), 0) % SAMPLE_ROWS

    @pl.when(s == 0)
    def _():
        qf = qs.astype(_F32)
        new = new_ref[0].astype(_F32)
        zs = _row_scores(qf, new[:, :width], n_new)
        c = jnp.zeros((rows, 1), _F32)
        o = jnp.zeros((rows, width), _F32)
        for j in reversed(range(n_new)):
            seen = j < qi_r
            lk = jnp.where(seen, -_softplus(zs[j]), 0.0)
            a = jnp.where(seen, jnp.exp(lk + zs[j] + c), 0.0)
            o = o + a * new[j:j + 1, width:]
            c = c + lk
        c_ref[...] = jnp.broadcast_to(c, c_ref.shape)
        acc_ref[...] = o

    page_keys = pages[0].shape[3]
    cum = _cumsum_matrix(page_keys)
    z = _dot(qs, _side_by_side(pages, 0, width))
    lk = -_softplus(z)
    c = c_ref[...]
    later = []
    for gp in range(n_pages_step):
        r = _dot_split(lk[:, gp * page_keys:(gp + 1) * page_keys], cum)
        later.append(r[:, :page_keys] + c)
        c = c + r[:, page_keys:]
    a = jnp.exp(lk + z + jnp.concatenate(later, axis=1))
    acc_ref[...] += _dot_nt(a.astype(MXU_DTYPE), _side_by_side(pages, width, 2 * width))
    c_ref[...] = c

    @pl.when(s == pl.num_programs(1) - 1)
    def _():
        o_ref[0] = _unstack_heads(acc_ref[...], SB_HEADS).astype(o_ref.dtype)


def _sb_sample(q, new_bf, cache, layer, page_table, n_new):
    b, n_pages = page_table.shape
    page = cache.shape[3]
    g = PAGES_PER_STEP
    rows = SB_HEADS * SAMPLE_ROWS
    width = SB_HEADS * HEAD_DIM
    return _sample_call(
        functools.partial(_sb_sample_kernel, n_pages_step=g, n_new=n_new),
        "sb_sample", page_table, n_pages // g,
        [q, new_bf], [_seq_block(q.shape), _seq_block(new_bf.shape)],
        [cache] * g, _page_specs(layer, g, (1, 1, 2 * width, page), 0, lambda s, gp: n_pages - 1 - (s * g + gp)),
        _seq_block(q.shape), jax.ShapeDtypeStruct(q.shape, MXU_DTYPE),
        [pltpu.VMEM((rows, LANES), _F32), pltpu.VMEM((rows, width), _F32)])


def _moba_sample_kernel(pt_ref, q_ref, new_ref, *refs, n_pages_step, n_new, past_len):
    del pt_ref
    g = n_pages_step
    kpages, pages = refs[:g], refs[g:2 * g]
    o_ref, m_ref, acc_ref, bias_ref, km_ref = refs[2 * g:]
    s = pl.program_id(1)
    n_half = pl.num_programs(1) // 2
    width = MOBA_HEADS * HEAD_DIM
    page_keys = pages[0].shape[3]
    n_keys = g * page_keys
    qf = _stack_heads(q_ref[0], MOBA_HEADS)
    rows = qf.shape[0]
    qs = (qf * HEAD_DIM ** -0.5).astype(MXU_DTYPE)
    qi_r = lax.broadcasted_iota(jnp.int32, (rows, 1), 0) % SAMPLE_ROWS
    lane = lax.broadcasted_iota(jnp.int32, (1, LANES), 1)

    @pl.when(s == 0)
    def _():
        km_ref[...] = jnp.zeros_like(km_ref)

    @pl.when(s < n_half)
    def _():
        km = km_ref[...]
        for gp, page in enumerate(kpages):
            blk = (s * g + gp) * page_keys // MOBA_BLOCK
            part = jnp.sum(page[0, 0], axis=1, keepdims=True) * (1.0 / MOBA_BLOCK)
            km = km + jnp.where(lane == blk, part, 0.0)
        km_ref[...] = km

    @pl.when(s == n_half)
    def _():
        gate = lax.dot_general(qf, km_ref[...], (((1,), (0,)), ((), ())), preferred_element_type=_F32,
                               precision=lax.Precision.HIGHEST)
        bias_ref[...] = _moba_pick(gate, past_len // MOBA_BLOCK, 1)
        new = new_ref[0].astype(_F32)
        m, l, acc = _new_key_softmax(qs.astype(_F32), new[:, :width], new[:, width:], n_new, lambda j: j <= qi_r)
        _state_refs_store(m_ref, acc_ref, _seed_state(m, l, acc))

    @pl.when(s >= n_half)
    def _():
        lhs = jnp.concatenate([qs, bias_ref[...].astype(MXU_DTYPE)], axis=1)
        rhs = jnp.concatenate([_side_by_side(pages, 0, width),
                               _key_block_onehot(n_keys, (s - n_half) * n_keys, MOBA_BLOCK)], axis=0)
        state = (m_ref[:, 0:1], acc_ref[...])
        v1 = _with_ones(_side_by_side(pages, width, 2 * width), 0)
        _state_refs_store(m_ref, acc_ref, _online_update1(state, _dot(lhs, rhs), v1, nt=True))

    @pl.when(s == pl.num_programs(1) - 1)
    def _():
        o_ref[0] = _unstack_heads(_finish1((m_ref[:, 0:1], acc_ref[...])), MOBA_HEADS).astype(o_ref.dtype)


def _moba_sample(q, new_bf, cache, layer, page_table, n_new):
    b, n_pages = page_table.shape
    page = cache.shape[3]
    g = PAGES_PER_STEP
    n_half = n_pages // g
    rows = MOBA_HEADS * SAMPLE_ROWS
    width = MOBA_HEADS * HEAD_DIM
    out_shape = (b, SAMPLE_ROWS, width)
    first = lambda s, gp: jnp.minimum(s, n_half - 1) * g + gp
    second = lambda s, gp: jnp.maximum(s - n_half, 0) * g + gp
    return _sample_call(
        functools.partial(_moba_sample_kernel, n_pages_step=g, n_new=n_new, past_len=n_pages * page),
        "moba_sample", page_table, 2 * n_half,
        [q, new_bf], [_seq_block(q.shape), _seq_block(new_bf.shape)],
        [cache] * (2 * g),
        _page_specs(layer, g, (1, 1, width, page), 0, first) + _page_specs(layer, g, (1, 1, 2 * width, page), 0, second),
        _seq_block(out_shape), jax.ShapeDtypeStruct(out_shape, MXU_DTYPE),
        [pltpu.VMEM((rows, LANES), _F32), pltpu.VMEM((rows, width + LANES), _F32), pltpu.VMEM((rows, LANES), _F32),
         pltpu.VMEM((width, LANES), _F32)])


def _rope_tables(pos):
    half = HEAD_DIM // 2
    inv = ROPE_THETA ** (-jnp.arange(half, dtype=_F32) / half)
    ang = pos.astype(_F32)[:, None] * inv[None, :]
    cos, sin = jnp.cos(ang), jnp.sin(ang)
    zero = jnp.zeros_like(sin)
    reps = LANES // HEAD_DIM
    return (jnp.tile(cos, (1, 2 * reps)), jnp.tile(jnp.concatenate([-sin, zero], axis=1), (1, reps)),
            jnp.tile(jnp.concatenate([zero, sin], axis=1), (1, reps)))


def _pack_w_in(w_in):
    d = w_in.shape[0]
    o1 = NSA_Q_COLS
    o2 = o1 + NSA_KV_COLS
    o3 = o2 + NSA_G_COLS
    o4 = o3 + SB_COLS
    q, kv, g, sb, mb = w_in[:, :o1], w_in[:, o1:o2], w_in[:, o2:o3], w_in[:, o3:o4], w_in[:, o4:]
    zero = jnp.zeros((d, HEAD_DIM), w_in.dtype)
    q_groups = []
    for hh in range(NSA_HEADS):
        w = q[:, hh * HEAD_DIM:(hh + 1) * HEAD_DIM]
        q_groups += [w, zero] if hh // NSA_GROUP == 0 else [zero, w]
    gate = jnp.pad(g, ((0, 0), (0, LANES - NSA_G_COLS)))
    return jnp.concatenate(q_groups + [kv, sb, mb, gate], axis=1).astype(MXU_DTYPE)


def _block_tables(t):
    key = jnp.arange(t)[:, None]
    j = jnp.arange(LANES)[None, :]
    eblk = jnp.where(key // SEL_BLOCK == j, -NEG, 0.0).astype(MXU_DTYPE)
    eblk_mb = jnp.where(key // MOBA_BLOCK == j, -NEG, 0.0).astype(MXU_DTYPE)
    i = jnp.arange(t // CMP_STRIDE)[:, None]
    ratio = SEL_BLOCK // CMP_STRIDE
    cover = ((i <= ratio * j + ratio - 1) & (i >= ratio * j - (CMP_LEN // CMP_STRIDE - 1))).astype(MXU_DTYPE)
    return eblk, eblk_mb, cover


def _layer_weights(l, norm_g, ffn_w_gate, ffn_w_up, ffn_w_down, w_in, nsa_cmp_pe, nsa_cmp_w1, nsa_cmp_w2,
                   w_branch_nsa, w_branch_sb, w_branch_moba, w_merge_gate, w_out, w_ple_proj, w_ple_gate):
    c = lambda w: w.astype(MXU_DTYPE)
    d, f = ffn_w_gate.shape[2:]
    tf = _ff_tile(f)
    cols = lambda w: c(w).reshape(d, f // tf, tf).transpose(1, 0, 2)
    return dict(
        g=[norm_g[l, i][None, :] for i in range(norm_g.shape[1])],
        ffn=[(cols(ffn_w_gate[l, i]), cols(ffn_w_up[l, i]), c(ffn_w_down[l, i]).reshape(f // tf, tf, d))
             for i in range(2)],
        w_all=_pack_w_in(w_in[l]),
        cw=_compress_weights(nsa_cmp_pe[l], nsa_cmp_w1[l], nsa_cmp_w2[l]),
        merge=(c(w_branch_nsa[l]), c(w_branch_sb[l]), c(w_branch_moba[l]), c(w_merge_gate[l]), c(w_out[l])),
        ple=(c(w_ple_gate[l]), c(w_ple_proj[l])),
    )


def _token_tile(n, cap):
    tm = cap
    while n % tm:
        tm //= 2
    return tm


def _ff_tile(f, cap=512):
    best = LANES
    for k in range(1, f // LANES + 1):
        if f % (k * LANES) == 0 and k * LANES <= cap:
            best = k * LANES
    return best


def _prompt_layer(h, p_l, lw, tabs, consts, b, t, depth, layer, carried):
    n, d = h.shape
    tm = _token_tile(t, 512)
    tm_ffn = _token_tile(t, 1024)
    tk = 8 * LANES
    g = lw["g"]
    h = _ffn(h, g[0], g[1], *lw["ffn"][0], tm=tm_ffn)
    per_seq = t // tm
    pr = _inproj(h, g[2], lw["w_all"], tabs, lambda i: (i % per_seq, 0), tm=tm, stack=(depth, layer, b, t),
                 carried=carried)
    stacked = tuple(pr.pop(k) for k in _INPROJ_STATE)
    kmean = pr.pop("kmean").reshape(b, per_seq, SUBLANES, -1)[:, :, :tm // MOBA_BLOCK].reshape(b, t // MOBA_BLOCK, -1)
    kmean = jnp.pad(kmean, ((0, 0), (0, LANES - kmean.shape[1]), (0, 0)))
    pr = {k: v.reshape(b, t, v.shape[1]) for k, v in pr.items()}
    eblk, eblk_mb, cover = consts
    cmp_k, cmp_v = _compress_prompt(pr["cmp_rows"], lw["cw"])
    o_nsa = _nsa_prompt(pr["q_nsa"], pr["gates"], cmp_k, cmp_v, pr["nsa_bf"], pr["win_bf"], eblk, cover,
                        tq=LANES, tk=tk)
    o_sb = _sb_prompt(pr["q_sb"], pr["sb_bf"], tq=2 * LANES, tk=LANES)
    o_mb = _moba_prompt(pr["q_mb"], kmean, pr["mb_bf"], eblk_mb, tk=tk)
    flat = lambda o: o.reshape(n, o.shape[2])
    h = _merge(h, g[2], g[3], flat(o_nsa), flat(o_sb), flat(o_mb), *lw["merge"], tm=tm)
    h = _ffn(h, g[4], g[5], *lw["ffn"][1], ple=(p_l, *lw["ple"], g[6]), tm=tm_ffn)
    return h, stacked, pr["win"]


def _sample_layer(h, p_l, lw, tabs, cover, caches, layer, page_table, n_new):
    n, d = h.shape
    b = page_table.shape[0]
    tm = _token_tile(n, 512)
    g = lw["g"]
    cache_nsa, cache_sb, cache_mb, cache_win = caches
    h = _ffn(h, g[0], g[1], *lw["ffn"][0], tm=tm)
    pr = _inproj(h, g[2], lw["w_all"], tabs, lambda i: (i, 0), tm=tm)
    pr = {k: v.reshape(b, SAMPLE_ROWS, v.shape[1]) for k, v in pr.items()}
    cmp_k, cmp_v = _compress_sample(cache_nsa, layer, page_table, lw["cw"])
    o_nsa = _nsa_sample(pr["q_nsa"], pr["gates"], cmp_k, cmp_v, pr["nsa_bf"], pr["win_bf"], cache_nsa, cache_win,
                        layer, page_table, cover, n_new)
    o_sb = _sb_sample(pr["q_sb"], pr["sb_bf"], cache_sb, layer, page_table, n_new)
    o_mb = _moba_sample(pr["q_mb"], pr["mb_bf"], cache_mb, layer, page_table, n_new)
    flat = lambda o: o.reshape(n, o.shape[2])
    h = _merge(h, g[2], g[3], flat(o_nsa), flat(o_sb), flat(o_mb), *lw["merge"], tm=tm)
    h = _ffn(h, g[4], g[5], *lw["ffn"][1], ple=(p_l, *lw["ple"], g[6]), tm=tm)
    return h, tuple(pr[k][:, :n_new] for k in ("nsa", "sb", "mb", "win"))


def kernel(x_prompt, x_sample, p_prompt, p_sample, cache_nsa, cache_sb, cache_moba, cache_win, page_table,
           norm_g, ffn_w_gate, ffn_w_up, ffn_w_down, w_in, nsa_cmp_pe, nsa_cmp_w1, nsa_cmp_w2,
           w_branch_nsa, w_branch_sb, w_branch_moba, w_merge_gate, w_out, w_ple_proj, w_ple_gate):
    depth = norm_g.shape[0]
    b, t, d = x_prompt.shape
    bs, ts, _ = x_sample.shape
    n_pages = page_table.shape[1]
    page = cache_nsa.shape[2]
    past_len = n_pages * page
    assert ts <= SAMPLE_ROWS and t % (8 * LANES) == 0 and t >= WINDOW + LANES
    assert t // SEL_BLOCK <= LANES and past_len // SEL_BLOCK <= LANES
    assert n_pages % PAGES_PER_STEP == 0 and MOBA_BLOCK % page == 0 and cache_win.shape[2] == WINDOW
    weights = (norm_g, ffn_w_gate, ffn_w_up, ffn_w_down, w_in, nsa_cmp_pe, nsa_cmp_w1, nsa_cmp_w2,
               w_branch_nsa, w_branch_sb, w_branch_moba, w_merge_gate, w_out, w_ple_proj, w_ple_gate)
    tabs_p = _rope_tables(jnp.arange(t, dtype=jnp.int32))
    tabs_s = _rope_tables(past_len + jnp.arange(bs * SAMPLE_ROWS, dtype=jnp.int32) % SAMPLE_ROWS)
    consts = _block_tables(t)
    cover_s = _block_tables(past_len)[2]
    pad_rows = lambda x: jnp.pad(x, ((0, 0), (0, SAMPLE_ROWS - ts), (0, 0))).reshape(bs * SAMPLE_ROWS, -1)
    feature_major = lambda c: jnp.transpose(c, (0, 1, 3, 4, 5, 2)).reshape(c.shape[0], c.shape[1], -1, c.shape[2])
    caches = tuple(feature_major(c) for c in (cache_nsa, cache_sb, cache_moba, cache_win))
    h_p = x_prompt.reshape(b * t, d)
    h_s = pad_rows(x_sample)
    stacked_p, win_p, st_s = (), [], []
    keep = min(WINDOW, t)
    for l in range(depth):
        lw = _layer_weights(l, *weights)
        h_p, stacked_p, win = _prompt_layer(h_p, p_prompt[l].reshape(b * t, -1), lw, tabs_p, consts, b, t, depth, l,
                                            stacked_p)
        win_p.append(win[:, t - keep:])
        h_s, rows = _sample_layer(h_s, pad_rows(p_sample[l]), lw, tabs_s, cover_s, caches, l, page_table, ts)
        win_fm = jnp.concatenate([caches[3][l][:, :, ts:], jnp.swapaxes(rows[3], 1, 2)], axis=2)
        st_s.append(rows[:3] + (jnp.swapaxes(win_fm, 1, 2),))
    heads = ((4, NSA_KV_HEADS), (2, SB_HEADS), (2, MOBA_HEADS))
    out = [h_p.reshape(b, t, d), h_s.reshape(bs, SAMPLE_ROWS, d)[:, :ts]]
    for i, (parts, nh) in enumerate(heads):
        out.append(jnp.swapaxes(stacked_p[i], 2, 3).reshape(depth, b, t, parts, nh, HEAD_DIM))
        out.append(jnp.stack([s[i] for s in st_s]).reshape(depth, bs, ts, parts, nh, HEAD_DIM))
    out.append(jnp.stack(win_p).reshape(depth, b, keep, 2, NSA_KV_HEADS, HEAD_DIM))
    out.append(jnp.stack([s[3] for s in st_s]).reshape(depth, bs, -1, 2, NSA_KV_HEADS, HEAD_DIM))
    return tuple(out)
```

```python
import functools

import jax
import jax.numpy as jnp
from jax import lax
from jax.experimental import pallas as pl
from jax.experimental.pallas import tpu as pltpu

HEAD_DIM = 64
NSA_HEADS = 8
NSA_KV_HEADS = 2
NSA_GROUP = NSA_HEADS // NSA_KV_HEADS
SB_HEADS = 4
MOBA_HEADS = 4
CMP_LEN = 32
CMP_STRIDE = 16
SEL_BLOCK = 64
SEL_TOPN = 16
WINDOW = 512
MOBA_BLOCK = 256
MOBA_TOPK = 3
ROPE_THETA = 10000.0
RMS_EPS = 1e-6
NEG = -1e30
FORCE_BONUS = 1e4
SB_DEAD = -104.0

LANES = 128
SUBLANES = 8
MXU_DTYPE = jnp.bfloat16
VMEM_LIMIT = 56 * 1024 * 1024
SAMPLE_ROWS = 8
PAGES_PER_STEP = 16
NSA_ROW_CHUNKS = 4

NSA_Q_COLS = NSA_HEADS * HEAD_DIM
NSA_KV_COLS = 6 * NSA_KV_HEADS * HEAD_DIM
NSA_G_COLS = 3 * NSA_HEADS
SB_COLS = 3 * SB_HEADS * HEAD_DIM
MOBA_COLS = 3 * MOBA_HEADS * HEAD_DIM

_F32 = jnp.float32


def _dot(a, b):
    return jnp.dot(a, b, preferred_element_type=_F32)


def _dot_nt(a, b):
    return lax.dot_general(a, b, (((1,), (1,)), ((), ())), preferred_element_type=_F32)


def _dot_split(x, w):
    hi = x.astype(MXU_DTYPE)
    mid = (x - hi.astype(_F32)).astype(MXU_DTYPE)
    return _dot(hi, w) + _dot(mid, w)


def _rms(x, g):
    return x * lax.rsqrt(jnp.mean(x * x, axis=-1, keepdims=True) + RMS_EPS) * g


def _params(*sem):
    return pltpu.CompilerParams(dimension_semantics=sem, vmem_limit_bytes=VMEM_LIMIT)


def _full(shape):
    n = len(shape)
    return pl.BlockSpec(shape, lambda *_: (0,) * n)


def _ffn_kernel(*refs, with_ple):
    if with_ple:
        (h_ref, gpre_ref, gpost_ref, wg_ref, wu_ref, wd_ref,
         p_ref, wpg_ref, wpp_ref, gple_ref, o_ref, xn_ref, acc_ref) = refs
    else:
        h_ref, gpre_ref, gpost_ref, wg_ref, wu_ref, wd_ref, o_ref, xn_ref, acc_ref = refs
    xn_ref[...] = _rms(h_ref[...], gpre_ref[...]).astype(MXU_DTYPE)
    acc_ref[...] = jnp.zeros_like(acc_ref)

    def chunk(j, carry):
        xn = xn_ref[...]
        g = _dot(xn, wg_ref[j])
        u = _dot(xn, wu_ref[j])
        a = (g * jax.nn.sigmoid(g)) * u
        acc_ref[...] += _dot(a.astype(MXU_DTYPE), wd_ref[j])
        return carry

    lax.fori_loop(0, wg_ref.shape[0], chunk, 0)
    h = h_ref[...] + 0.5 * _rms(acc_ref[...], gpost_ref[...])
    if with_ple:
        gate = jax.nn.sigmoid(_dot(h.astype(MXU_DTYPE), wpg_ref[...]))
        ple = gate * _dot(p_ref[...].astype(MXU_DTYPE), wpp_ref[...])
        h = h + _rms(ple, gple_ref[...])
    o_ref[...] = h


def _resident(shape):
    n = len(shape)
    return pl.BlockSpec(shape, lambda *_: (0,) * n, pipeline_mode=pl.Buffered(1))


def _ffn(h, g_pre, g_post, wg, wu, wd, ple=None, *, tm):
    n, d = h.shape
    row = lambda i: (i, 0)
    in_specs = [pl.BlockSpec((tm, d), row), _full((1, d)), _full((1, d)),
                _resident(wg.shape), _resident(wu.shape), _resident(wd.shape)]
    args = [h, g_pre, g_post, wg, wu, wd]
    if ple is not None:
        p, wpg, wpp, g_ple = ple
        in_specs += [pl.BlockSpec((tm, p.shape[1]), row), _resident(wpg.shape), _resident(wpp.shape), _full((1, d))]
        args += [p, wpg, wpp, g_ple]
    return pl.pallas_call(
        functools.partial(_ffn_kernel, with_ple=ple is not None),
        grid=(n // tm,),
        in_specs=in_specs,
        out_specs=pl.BlockSpec((tm, d), row),
        out_shape=jax.ShapeDtypeStruct((n, d), _F32),
        scratch_shapes=[pltpu.VMEM((tm, d), MXU_DTYPE), pltpu.VMEM((tm, d), _F32)],
        compiler_params=_params("parallel"),
        name="ffn_ple" if ple is not None else "ffn",
    )(*args)


_G_QNSA = 0
_G_KV = 8
_G_SB = 14
_G_MB = 20
_G_GATE = 26
_ROPE_GROUPS = tuple(range(8)) + (8, 10, 12, 20, 21, 22, 23)
_INPROJ_OUT = (("q_nsa", _G_QNSA, 8, True, None), ("nsa", _G_KV, 4, False, _F32), ("win", _G_KV + 4, 2, False, _F32),
               ("q_sb", _G_SB, 2, True, None), ("sb", _G_SB + 2, 4, False, _F32),
               ("q_mb", _G_MB, 2, False, _F32), ("mb", _G_MB + 2, 4, False, _F32))
_INPROJ_NARROW = ("nsa", "win", "sb", "mb")


_INPROJ_STATE = ("nsa", "sb", "mb")


def _inproj_kernel(h_ref, g_ref, w_ref, cos_ref, sa_ref, sb_ref, *refs, n_carried, feature_major):
    out_refs = refs[n_carried:]
    n_stacked = feature_major
    xn = _rms(h_ref[...], g_ref[...]).astype(MXU_DTYPE)
    cos, sa, sb = cos_ref[...], sa_ref[...], sb_ref[...]
    scale = HEAD_DIM ** -0.5
    tm = xn.shape[0]

    pairs = {}

    def group(gi):
        if gi not in pairs:
            g0 = gi - gi % 2
            wide = _dot(xn, w_ref[:, g0 * LANES:min(g0 + 2, _G_GATE + 1) * LANES])
            for k in range(wide.shape[1] // LANES):
                pairs[g0 + k] = wide[:, k * LANES:(k + 1) * LANES]
        y = pairs.pop(gi)
        if gi in _ROPE_GROUPS:
            y = y * cos + pltpu.roll(y, LANES - HEAD_DIM // 2, 1) * sa + pltpu.roll(y, HEAD_DIM // 2, 1) * sb
        return y

    outs = out_refs[:len(_INPROJ_OUT)]
    narrow = dict(zip(_INPROJ_NARROW, out_refs[len(_INPROJ_OUT):]))
    gate_ref = out_refs[len(_INPROJ_OUT) + len(_INPROJ_NARROW)]
    means = []
    for (name, g0, n_groups, scaled, _), o_ref in zip(_INPROJ_OUT, outs):
        for i in range(n_groups):
            y = group(g0 + i)
            if scaled:
                y = y * scale
            if n_stacked and name in _INPROJ_STATE:
                o_ref[0, 0, i * LANES:(i + 1) * LANES, :] = y.T
            else:
                o_ref[:, i * LANES:(i + 1) * LANES] = y.astype(o_ref.dtype)
            if name in narrow:
                narrow[name][:, i * LANES:(i + 1) * LANES] = y.astype(MXU_DTYPE)
            if n_stacked and name == "nsa" and i < 2:
                out_refs[-2][:, i * LANES:(i + 1) * LANES] = y
            if n_stacked and name == "mb" and i < 2:
                means.append([jnp.mean(y[r:r + MOBA_BLOCK], axis=0, keepdims=True) for r in range(0, tm, MOBA_BLOCK)])
    gate_ref[...] = jax.nn.sigmoid(group(_G_GATE))
    if n_stacked:
        rows = [jnp.concatenate([means[0][r], means[1][r]], axis=1) for r in range(tm // MOBA_BLOCK)]
        rows.append(jnp.zeros((SUBLANES - len(rows), 2 * LANES), _F32))
        out_refs[-1][...] = jnp.concatenate(rows, axis=0)


def _inproj(h, g, w_all, tabs, tab_map, *, tm, stack=None, carried=()):
    n, d = h.shape
    row = lambda i: (i, 0)
    tab_spec = pl.BlockSpec((tm, LANES), tab_map)
    sizes = {o[0]: o[2] * LANES for o in _INPROJ_OUT}
    names = [o[0] for o in _INPROJ_OUT] + [k + "_bf" for k in _INPROJ_NARROW] + ["gates"]
    shapes = [jax.ShapeDtypeStruct((n, o[2] * LANES), o[4] or MXU_DTYPE) for o in _INPROJ_OUT]
    shapes += [jax.ShapeDtypeStruct((n, sizes[k]), MXU_DTYPE) for k in _INPROJ_NARROW]
    shapes.append(jax.ShapeDtypeStruct((n, LANES), _F32))
    specs = [pl.BlockSpec((tm, s.shape[1]), row) for s in shapes]
    aliases = {}
    n_fixed_inputs = 6
    if stack is not None:
        depth, layer, b, t = stack
        per_seq = t // tm
        assert tm % MOBA_BLOCK == 0 and tm // MOBA_BLOCK <= SUBLANES and len(carried) in (0, len(_INPROJ_STATE))
        for k, name in enumerate(_INPROJ_STATE):
            idx = names.index(name)
            shapes[idx] = jax.ShapeDtypeStruct((depth, b, sizes[name], t), _F32)
            specs[idx] = pl.BlockSpec((1, 1, sizes[name], tm), lambda i: (layer, i // per_seq, 0, i % per_seq))
            if carried:
                aliases[n_fixed_inputs + k] = idx
        names += ["cmp_rows", "kmean"]
        shapes += [jax.ShapeDtypeStruct((n, 2 * LANES), _F32),
                   jax.ShapeDtypeStruct((n // tm * SUBLANES, 2 * LANES), _F32)]
        specs += [pl.BlockSpec((tm, 2 * LANES), row), pl.BlockSpec((SUBLANES, 2 * LANES), row)]
    outs = pl.pallas_call(
        functools.partial(_inproj_kernel, n_carried=len(carried), feature_major=stack is not None),
        grid=(n // tm,),
        in_specs=[pl.BlockSpec((tm, d), row), _full((1, d)), _full(w_all.shape), tab_spec, tab_spec, tab_spec]
        + [pl.BlockSpec(memory_space=pl.ANY)] * len(carried),
        out_specs=specs,
        out_shape=shapes,
        input_output_aliases=aliases,
        compiler_params=_params("parallel"),
        name="inproj",
    )(h, g, w_all, *tabs, *carried)
    return dict(zip(names, outs))


def _merge_kernel(h_ref, gpre_ref, gpost_ref, onsa_ref, osb_ref, omb_ref,
                  wbn_ref, wbs_ref, wbm_ref, wgate_ref, wout_ref, o_ref):
    d = h_ref.shape[1]
    h = h_ref[...]
    u = _rms(h, gpre_ref[...]).astype(MXU_DTYPE)
    branches = (_dot(onsa_ref[...], wbn_ref[...]), _dot(osb_ref[...], wbs_ref[...]), _dot(omb_ref[...], wbm_ref[...]))
    mix = None
    for c, br in enumerate(branches):
        term = jax.nn.sigmoid(_dot(u, wgate_ref[:, c * d:(c + 1) * d])) * br
        mix = term if mix is None else mix + term
    y = _dot(mix.astype(MXU_DTYPE), wout_ref[...])
    o_ref[...] = h + _rms(y, gpost_ref[...])


def _merge(h, g_pre, g_post, o_nsa, o_sb, o_mb, wbn, wbs, wbm, wgate, wout, *, tm):
    n, d = h.shape
    row = lambda i: (i, 0)
    return pl.pallas_call(
        _merge_kernel,
        grid=(n // tm,),
        in_specs=[pl.BlockSpec((tm, d), row), _full((1, d)), _full((1, d)),
                  pl.BlockSpec((tm, o_nsa.shape[1]), row), pl.BlockSpec((tm, o_sb.shape[1]), row),
                  pl.BlockSpec((tm, o_mb.shape[1]), row),
                  _full(wbn.shape), _full(wbs.shape), _full(wbm.shape), _full(wgate.shape), _full(wout.shape)],
        out_specs=pl.BlockSpec((tm, d), row),
        out_shape=jax.ShapeDtypeStruct((n, d), _F32),
        compiler_params=_params("parallel"),
        name="merge",
    )(h, g_pre, g_post, o_nsa, o_sb, o_mb, wbn, wbs, wbm, wgate, wout)


def _lane_lo(shape=(1, LANES)):
    return (lax.broadcasted_iota(jnp.int32, shape, len(shape) - 1) % LANES) < HEAD_DIM


def _softplus(z):
    return jnp.maximum(z, 0.0) + jnp.log1p(jnp.exp(-jnp.abs(z)))


def _cumsum_matrix(tk):
    j = lax.broadcasted_iota(jnp.int32, (tk, 2 * tk), 0)
    s = lax.broadcasted_iota(jnp.int32, (tk, 2 * tk), 1)
    return jnp.where((j > s) | (s >= tk), 1.0, 0.0).astype(MXU_DTYPE)


def _online_update1(state, z, v1, nt=False):
    m, acc = state
    m_new = jnp.maximum(m, jnp.max(z, axis=-1, keepdims=True))
    p = jnp.exp(z - m_new).astype(MXU_DTYPE)
    return m_new, jnp.exp(m - m_new) * acc + (_dot_nt(p, v1) if nt else _dot(p, v1))


def _online_init1(rows, width=LANES):
    return jnp.full((rows, 1), NEG, _F32), jnp.zeros((rows, width + LANES), _F32)


def _finish1(state):
    _, acc = state
    width = acc.shape[1] - LANES
    den = 1.0 / acc[:, width:]
    return acc[:, :width] * jnp.concatenate([den] * (width // LANES), axis=1)


def _with_ones(v, axis=1):
    shape = (v.shape[0], LANES) if axis == 1 else (LANES, v.shape[1])
    return jnp.concatenate([v, jnp.ones(shape, v.dtype)], axis=axis)


def _seed_state(m, l, acc):
    return m, jnp.concatenate([acc, jnp.broadcast_to(l, (l.shape[0], LANES))], axis=1)


def _topk_mask(val, k, axis):
    idx = lax.broadcasted_iota(jnp.int32, val.shape, axis).astype(_F32)
    sel = jnp.zeros(val.shape, _F32)
    for _ in range(k):
        m = jnp.max(val, axis=axis, keepdims=True)
        first = jnp.min(jnp.where(val == m, idx, 1e9), axis=axis, keepdims=True)
        hit = idx == first
        sel = jnp.where(hit, 1.0, sel)
        val = jnp.where(hit, -jnp.inf, val)
    return sel


def _stack_nsa_queries(q):
    return jnp.concatenate([q[:, hh * LANES:(hh + 1) * LANES] for hh in range(NSA_HEADS)], axis=0)


def _cmp_branch(qs, kc, vc, qpos, tq):
    nc = kc.shape[0]
    s = _dot_nt(qs, kc)
    cend = lax.broadcasted_iota(jnp.int32, (1, nc), 1) * CMP_STRIDE + (CMP_LEN - 1)
    vis = cend <= qpos
    sm = jnp.where(vis, s, NEG)
    m = jnp.max(sm, axis=-1, keepdims=True)
    e = jnp.where(vis, jnp.exp(sm - m), 0.0)
    l = jnp.sum(e, axis=-1, keepdims=True)
    p = e * (1.0 / jnp.where(l > 0.0, l, 1.0))
    o_cmp = _dot(p.astype(MXU_DTYPE), vc)
    psums = []
    for kv in range(NSA_KV_HEADS):
        blocks = [p[(kv * NSA_GROUP + g) * tq:(kv * NSA_GROUP + g + 1) * tq] for g in range(NSA_GROUP)]
        psums.append(functools.reduce(lambda a, b: a + b, blocks))
    return o_cmp, psums


def _nsa_select(imp, qp, n_blocks, n_top, axis):
    jblk = lax.broadcasted_iota(jnp.int32, imp.shape, axis)
    cur = qp // SEL_BLOCK
    forced = (jblk == 0) | (jblk == cur) | (jblk == cur - 1)
    valid = (jblk * SEL_BLOCK <= qp) & (jblk < n_blocks)
    val = jnp.where(valid, imp + jnp.where(forced, FORCE_BONUS, 0.0), NEG)
    return jnp.where(valid, _topk_mask(val, n_top, axis), 0.0) - 1.0


def _nsa_lhs(qs, biases):
    bias = jnp.concatenate([b for b in biases for _ in range(NSA_GROUP)], axis=0)
    return jnp.concatenate([qs, bias.astype(qs.dtype)], axis=1)


def _combine_nsa(gates, o_cmp, o_sel, o_win, tq):
    lo = _lane_lo()
    heads = []
    for hh in range(NSA_HEADS):
        r = slice(hh * tq, (hh + 1) * tq)
        o = (gates[:, 3 * hh:3 * hh + 1] * o_cmp[r] + gates[:, 3 * hh + 1:3 * hh + 2] * o_sel[r]
             + gates[:, 3 * hh + 2:3 * hh + 3] * o_win[r])
        if hh % 2 != hh // NSA_GROUP:
            o = pltpu.roll(o, HEAD_DIM, 1)
        heads.append(o)
    return jnp.concatenate([jnp.where(lo, heads[2 * i], heads[2 * i + 1]) for i in range(NSA_HEADS // 2)], axis=1)


def _moba_pick(s, own, axis):
    past = lax.broadcasted_iota(jnp.int32, s.shape, axis) < own
    return jnp.where(past, _topk_mask(jnp.where(past, s, NEG), MOBA_TOPK, axis), 0.0) - 1.0


def _sb_block(qh, kt, vt, cum, carry, keep):
    c, o = carry
    tk = kt.shape[0]
    z = _dot_nt(qh, kt)
    lk = -_softplus(z)
    if keep is not None:
        lk = jnp.where(keep, lk, 0.0)
    r = _dot_split(lk, cum)
    a = jnp.exp(lk + z + r[:, :tk] + c)
    if keep is not None:
        a = jnp.where(keep, a, 0.0)
    return c + r[:, tk:], o + _dot(a.astype(MXU_DTYPE), vt)


def _stage_chunks(src_ref, lead, n_chunks, dst_ref, c0):
    for r in range(CMP_STRIDE):
        x = src_ref[lead + (pl.ds(r, n_chunks, stride=CMP_STRIDE), slice(None))]
        dst_ref[pl.ds(c0, n_chunks), r * LANES:(r + 1) * LANES] = x


def _compress_finish(xk_ref, xv_ref, wk_ref, wv_ref, pe_ref, w2_ref, ok_ref, ov_ref):
    nc = xk_ref.shape[0]
    for x_ref, w_ref, t, o_ref in ((xk_ref, wk_ref, 0, ok_ref), (xv_ref, wv_ref, 1, ov_ref)):
        ab = _dot(x_ref[...].astype(MXU_DTYPE), w_ref[...])
        cst = _dot(pe_ref[t].astype(MXU_DTYPE), w_ref[...])
        for h in range(NSA_KV_HEADS):
            a = ab[:, h * 256:h * 256 + LANES]
            b = ab[:, h * 256 + LANES:(h + 1) * 256]
            c = cst[0:1, h * 256:h * 256 + LANES] + cst[1:2, h * 256 + LANES:(h + 1) * 256]
            hid = jax.nn.gelu(a + pltpu.roll(b, nc - 1, 0) + c)
            part = _dot(hid.astype(MXU_DTYPE), w2_ref[t, h])
            out = part if h == 0 else out + part
        o_ref[0] = out.astype(o_ref.dtype)


def _compress_prompt_kernel(rk_ref, rv_ref, wk_ref, wv_ref, pe_ref, w2_ref, ok_ref, ov_ref, xk_ref, xv_ref):
    n_chunks = rk_ref.shape[1] // CMP_STRIDE
    _stage_chunks(rk_ref, (0,), n_chunks, xk_ref, 0)
    _stage_chunks(rv_ref, (0,), n_chunks, xv_ref, 0)
    _compress_finish(xk_ref, xv_ref, wk_ref, wv_ref, pe_ref, w2_ref, ok_ref, ov_ref)


def _compress_sample_kernel(pt_ref, wk_ref, wv_ref, pe_ref, w2_ref, *refs, n_pages_step):
    del pt_ref
    pages = refs[:n_pages_step]
    ok_ref, ov_ref, xk_ref, xv_ref, rows_ref = refs[n_pages_step:]
    s = pl.program_id(1)
    per_page = pages[0].shape[3] // CMP_STRIDE
    for gp, page in enumerate(pages):
        x = page[0, 0]
        rows_ref[0] = x[:LANES].T
        rows_ref[1] = x[LANES:].T
        c0 = pl.multiple_of((s * n_pages_step + gp) * per_page, per_page)
        _stage_chunks(rows_ref, (0,), per_page, xk_ref, c0)
        _stage_chunks(rows_ref, (1,), per_page, xv_ref, c0)

    @pl.when(s == pl.num_programs(1) - 1)
    def _():
        _compress_finish(xk_ref, xv_ref, wk_ref, wv_ref, pe_ref, w2_ref, ok_ref, ov_ref)


def _compress_weights(pe, w1, w2):
    eye = jnp.eye(NSA_KV_HEADS, dtype=w1.dtype)
    big, pes = [], []
    for t in range(2):
        w1r = w1[t].reshape(2, CMP_STRIDE, HEAD_DIM, w1.shape[-1])
        wb = jnp.einsum("hH,ardn->rhdHan", eye, w1r)
        big.append(wb.reshape(CMP_STRIDE * NSA_KV_HEADS * HEAD_DIM, -1).astype(MXU_DTYPE))
        per = pe[t].reshape(2, CMP_STRIDE, 1, HEAD_DIM)
        rows = jnp.broadcast_to(per, (2, CMP_STRIDE, NSA_KV_HEADS, HEAD_DIM)).reshape(2, -1)
        pes.append(jnp.concatenate([rows, jnp.zeros((SUBLANES - 2, rows.shape[1]), rows.dtype)], axis=0))
    zero = jnp.zeros_like(w2)
    w2p = jnp.stack([jnp.concatenate([w2, zero], axis=-1), jnp.concatenate([zero, w2], axis=-1)], axis=1)
    return big[0], big[1], jnp.stack(pes), w2p.astype(MXU_DTYPE)


def _compress_out(b, nc):
    shape = jax.ShapeDtypeStruct((b, nc, LANES), MXU_DTYPE)
    return [shape, shape]


def _compress_prompt(nsa_rows, cw):
    b, t, _ = nsa_rows.shape
    nc = t // CMP_STRIDE
    wk, wv, pe2, w2p = cw
    out_spec = pl.BlockSpec((1, nc, LANES), lambda i: (i, 0, 0))
    return pl.pallas_call(
        _compress_prompt_kernel,
        grid=(b,),
        in_specs=[pl.BlockSpec((1, t, LANES), lambda i: (i, 0, 0)),
                  pl.BlockSpec((1, t, LANES), lambda i: (i, 0, 1)),
                  _full(wk.shape), _full(wv.shape), _full(pe2.shape), _full(w2p.shape)],
        out_specs=[out_spec, out_spec],
        out_shape=_compress_out(b, nc),
        scratch_shapes=[pltpu.VMEM((nc, CMP_STRIDE * LANES), _F32)] * 2,
        compiler_params=_params("arbitrary"),
        name="nsa_compress_prompt",
    )(nsa_rows, nsa_rows, wk, wv, pe2, w2p)


def _nsa_prompt_kernel(q_ref, g_ref, ck_ref, cv_ref, sk_ref, sv_ref, wk_ref, wv_ref, eblk_ref, cover_ref,
                       o_ref, *, tq, tk, n_top):
    qi = pl.program_id(1)
    t0 = qi * tq
    rows = NSA_HEADS * tq
    n_blocks = sk_ref.shape[1] // SEL_BLOCK
    qs = _stack_nsa_queries(q_ref[0])
    qpos_r = t0 + lax.broadcasted_iota(jnp.int32, (rows, 1), 0) % tq

    o_cmp, psums = _cmp_branch(qs, ck_ref[0], cv_ref[0], qpos_r, tq)
    qp_t = t0 + lax.broadcasted_iota(jnp.int32, (1, tq), 1)
    biases = []
    for psum in psums:
        imp_t = _dot_split(psum, cover_ref[...]).T
        biases.append(_nsa_select(imp_t, qp_t, n_blocks, n_top, 0).T)
    lhs = _nsa_lhs(qs, biases)

    kcol = lax.broadcasted_iota(jnp.int32, (1, tk), 1)

    chunk = rows // NSA_ROW_CHUNKS
    lhs_c = [lhs[c * chunk:(c + 1) * chunk] for c in range(NSA_ROW_CHUNKS)]
    qpos_c = [qpos_r[c * chunk:(c + 1) * chunk] for c in range(NSA_ROW_CHUNKS)]

    def sel_tile(kt, states, causal):
        start = pl.multiple_of(kt * tk, tk)
        rhs = jnp.concatenate([sk_ref[0, pl.ds(start, tk), :], eblk_ref[pl.ds(start, tk), :]], axis=1)
        v1 = _with_ones(sv_ref[0, pl.ds(start, tk), :])
        out = []
        for c in range(NSA_ROW_CHUNKS):
            z = _dot_nt(lhs_c[c], rhs)
            if causal:
                z = jnp.where(start + kcol <= qpos_c[c], z, NEG)
            out.append(_online_update1(states[c], z, v1))
        return tuple(out)

    n_full = t0 // tk
    init = tuple(_online_init1(chunk) for _ in range(NSA_ROW_CHUNKS))
    states = lax.fori_loop(0, n_full, lambda kt, st: sel_tile(kt, st, False), init)
    o_sel = jnp.concatenate([_finish1(st) for st in sel_tile(n_full, states, True)], axis=0)

    n_band = WINDOW + tq
    ws = pl.multiple_of(jnp.maximum(t0 - WINDOW, 0), tq)
    z = _dot_nt(qs, wk_ref[0, pl.ds(ws, n_band), :])
    dist = qpos_r - (ws + lax.broadcasted_iota(jnp.int32, (1, n_band), 1))
    z = jnp.where((dist >= 0) & (dist <= WINDOW), z, NEG)
    p = jnp.exp(z - jnp.max(z, axis=-1, keepdims=True))
    o_win = _dot(p.astype(MXU_DTYPE), wv_ref[0, pl.ds(ws, n_band), :]) * (1.0 / jnp.sum(p, axis=-1, keepdims=True))

    o_ref[0] = _combine_nsa(g_ref[0], o_cmp, o_sel, o_win, tq).astype(o_ref.dtype)


def _nsa_prompt(q_nsa, gates, cmp_k, cmp_v, nsa_bf, win_bf, eblk, cover, *, tq, tk):
    b, t, _ = q_nsa.shape
    nc = cmp_k.shape[1]
    n_top = min(SEL_TOPN, t // SEL_BLOCK)
    cmp_spec = pl.BlockSpec((1, nc, LANES), lambda i, qi: (i, 0, 0))
    col = lambda g: pl.BlockSpec((1, t, LANES), lambda i, qi: (i, 0, g))
    return pl.pallas_call(
        functools.partial(_nsa_prompt_kernel, tq=tq, tk=tk, n_top=n_top),
        grid=(b, t // tq),
        in_specs=[pl.BlockSpec((1, tq, NSA_HEADS * LANES), lambda i, qi: (i, qi, 0)),
                  pl.BlockSpec((1, tq, LANES), lambda i, qi: (i, qi, 0)),
                  cmp_spec, cmp_spec, col(2), col(3), col(0), col(1),
                  _full(eblk.shape), _full(cover.shape)],
        out_specs=pl.BlockSpec((1, tq, NSA_Q_COLS), lambda i, qi: (i, qi, 0)),
        out_shape=jax.ShapeDtypeStruct((b, t, NSA_Q_COLS), MXU_DTYPE),
        compiler_params=_params("parallel", "arbitrary"),
        name="nsa_prompt",
    )(q_nsa, gates, cmp_k, cmp_v, nsa_bf, nsa_bf, win_bf, win_bf, eblk, cover)


def _sb_prompt_kernel(q_ref, k_ref, v_ref, o_ref, *, tq, tk):
    qi = pl.program_id(2)
    lo = _lane_lo()
    q = q_ref[0]
    zero_q = jnp.zeros_like(q)
    qh = (jnp.where(lo, q, zero_q), jnp.where(lo, zero_q, q))
    cum = _cumsum_matrix(tk)
    row = lax.broadcasted_iota(jnp.int32, (tq, tk), 0)
    col = lax.broadcasted_iota(jnp.int32, (tq, tk), 1)
    per_tile = tq // tk

    def blocks(kb, carry, keep):
        start = pl.multiple_of(kb * tk, tk)
        kt, vt = k_ref[0, pl.ds(start, tk), :], v_ref[0, pl.ds(start, tk), :]
        return tuple(_sb_block(qh[h], kt, vt, cum, carry[h], keep) for h in range(2))

    def alive(carry):
        return (jnp.max(jnp.maximum(carry[0][0], carry[1][0])) > SB_DEAD).astype(jnp.int32)

    zero = jnp.zeros((tq, LANES), _F32)
    carry = ((zero, zero), (zero, zero))
    for d in reversed(range(per_tile)):
        carry = blocks(qi * per_tile + d, carry, d * tk + col < row)

    def body(st):
        i, _, carry = st
        for d in range(per_tile):
            carry = blocks((qi - 1 - i) * per_tile + per_tile - 1 - d, carry, None)
        return i + 1, alive(carry), carry

    _, _, carry = lax.while_loop(lambda st: (st[0] < qi) & (st[1] > 0), body, (0, alive(carry), carry))
    o_ref[0] = jnp.where(lo, carry[0][1], carry[1][1]).astype(o_ref.dtype)


def _sb_prompt(q_sb, sb_bf, *, tq, tk):
    b, t, _ = q_sb.shape
    n_pair = SB_HEADS // 2
    return pl.pallas_call(
        functools.partial(_sb_prompt_kernel, tq=tq, tk=tk),
        grid=(b, n_pair, t // tq),
        in_specs=[pl.BlockSpec((1, tq, LANES), lambda i, p, qi: (i, qi, p)),
                  pl.BlockSpec((1, t, LANES), lambda i, p, qi: (i, 0, p)),
                  pl.BlockSpec((1, t, LANES), lambda i, p, qi: (i, 0, n_pair + p))],
        out_specs=pl.BlockSpec((1, tq, LANES), lambda i, p, qi: (i, qi, p)),
        out_shape=jax.ShapeDtypeStruct((b, t, SB_HEADS * HEAD_DIM), MXU_DTYPE),
        compiler_params=_params("parallel", "parallel", "arbitrary"),
        name="sb_prompt",
    )(q_sb, sb_bf, sb_bf)


def _moba_prompt_kernel(q_ref, km_ref, k_ref, v_ref, eblk_ref, o_ref, *, tq, tk):
    qi = pl.program_id(2)
    lo = _lane_lo()
    q = q_ref[0]
    scale = HEAD_DIM ** -0.5
    row = lax.broadcasted_iota(jnp.int32, (tq, tq), 0)
    col = lax.broadcasted_iota(jnp.int32, (tq, tq), 1)
    own0 = pl.multiple_of(qi * tq, tq)
    k_own, v_own = k_ref[0, pl.ds(own0, tq), :], _with_ones(v_ref[0, pl.ds(own0, tq), :])
    lhs, states = [], []
    for hh in range(2):
        qf = jnp.where(lo if hh == 0 else jnp.logical_not(lo), q, 0.0)
        gate = lax.dot_general(qf, km_ref[0], (((1,), (1,)), ((), ())), preferred_element_type=_F32,
                               precision=lax.Precision.HIGHEST)
        bias = _moba_pick(gate.T, qi, 0).T
        qh = (qf * scale).astype(MXU_DTYPE)
        lhs.append(jnp.concatenate([qh, bias.astype(MXU_DTYPE)], axis=1))
        z = jnp.where(col <= row, _dot_nt(qh, k_own), NEG)
        states.append(_online_update1(_online_init1(tq), z, v_own))

    def tile(j, states):
        start = pl.multiple_of(j * tk, tk)
        rhs = jnp.concatenate([k_ref[0, pl.ds(start, tk), :], eblk_ref[pl.ds(start, tk), :]], axis=1)
        v1 = _with_ones(v_ref[0, pl.ds(start, tk), :])
        return tuple(_online_update1(st, _dot_nt(l, rhs), v1) for l, st in zip(lhs, states))

    states = lax.fori_loop(0, (qi * tq + tk - 1) // tk, tile, tuple(states))
    o_ref[0] = jnp.where(lo, _finish1(states[0]), _finish1(states[1])).astype(o_ref.dtype)


def _moba_prompt(q_mb, kmean, mb_bf, eblk, *, tk):
    b, t, _ = q_mb.shape
    tq = MOBA_BLOCK
    n_pair = MOBA_HEADS // 2
    return pl.pallas_call(
        functools.partial(_moba_prompt_kernel, tq=tq, tk=tk),
        grid=(b, n_pair, t // tq),
        in_specs=[pl.BlockSpec((1, tq, LANES), lambda i, p, qi: (i, qi, p)),
                  pl.BlockSpec((1, LANES, LANES), lambda i, p, qi: (i, 0, p)),
                  pl.BlockSpec((1, t, LANES), lambda i, p, qi: (i, 0, p)),
                  pl.BlockSpec((1, t, LANES), lambda i, p, qi: (i, 0, n_pair + p)),
                  _full(eblk.shape)],
        out_specs=pl.BlockSpec((1, tq, LANES), lambda i, p, qi: (i, qi, p)),
        out_shape=jax.ShapeDtypeStruct((b, t, MOBA_HEADS * HEAD_DIM), MXU_DTYPE),
        compiler_params=_params("parallel", "parallel", "arbitrary"),
        name="moba_prompt",
    )(q_mb, kmean, mb_bf, mb_bf, eblk)


def _row_scores(q, keys, n):
    return [jnp.sum(q * keys[j:j + 1, :], axis=-1, keepdims=True) for j in range(n)]


def _new_key_softmax(q, keys, vals, n, visible):
    zs = [jnp.where(visible(j), z, NEG) for j, z in enumerate(_row_scores(q, keys, n))]
    m = functools.reduce(jnp.maximum, zs)
    l = jnp.zeros_like(m)
    acc = jnp.zeros((q.shape[0], vals.shape[1]), _F32)
    for j, z in enumerate(zs):
        p = jnp.exp(z - m)
        l = l + p
        acc = acc + p * vals[j:j + 1, :]
    return m, l, acc


def _stack_heads(q, n_heads):
    head = lax.broadcasted_iota(jnp.int32, (1, q.shape[1]), 1) // HEAD_DIM
    return jnp.concatenate([jnp.where(head == h, q, jnp.zeros_like(q)) for h in range(n_heads)], axis=0)


def _unstack_heads(o, n_heads):
    rows = o.shape[0] // n_heads
    head = lax.broadcasted_iota(jnp.int32, (1, o.shape[1]), 1) // HEAD_DIM
    out = jnp.zeros((rows, o.shape[1]), _F32)
    for h in range(n_heads):
        out = jnp.where(head == h, o[h * rows:(h + 1) * rows], out)
    return out


def _side_by_side(pages, f0, f1):
    return jnp.concatenate([p[0, 0, f0:f1, :] for p in pages], axis=1).astype(MXU_DTYPE)


def _key_block_onehot(n_keys, first_key, block):
    key_blk = (first_key + lax.broadcasted_iota(jnp.int32, (LANES, n_keys), 1)) // block
    return jnp.where(lax.broadcasted_iota(jnp.int32, (LANES, n_keys), 0) == key_blk, -NEG, 0.0).astype(MXU_DTYPE)


def _state_refs_store(m_ref, acc_ref, state):
    m_ref[...] = jnp.broadcast_to(state[0], m_ref.shape)
    acc_ref[...] = state[1]


def _page_specs(layer, n, block, row_block, page_of):
    def spec(gp):
        return pl.BlockSpec(block, lambda b, s, pt: (layer, pt[b, page_of(s, gp)], row_block, 0))
    return [spec(gp) for gp in range(n)]


def _sample_call(kernel_fn, name, page_table, n_steps, pre_args, pre_specs, page_arrays, page_specs, out_block,
                 out_shape, scratch):
    grid_spec = pltpu.PrefetchScalarGridSpec(
        num_scalar_prefetch=1,
        grid=(page_table.shape[0], n_steps),
        in_specs=pre_specs + page_specs,
        out_specs=out_block,
        scratch_shapes=scratch,
    )
    return pl.pallas_call(kernel_fn, grid_spec=grid_spec, out_shape=out_shape,
                          compiler_params=_params("arbitrary", "arbitrary"), name=name,
                          )(page_table, *pre_args, *page_arrays)


def _seq_block(shape):
    zeros = (0,) * (len(shape) - 1)
    return pl.BlockSpec((1,) + tuple(shape[1:]), lambda b, s, pt: (b,) + zeros)


def _const_block(shape):
    zeros = (0,) * len(shape)
    return pl.BlockSpec(tuple(shape), lambda b, s, pt: zeros)


def _compress_sample(cache, layer, page_table, cw):
    b, n_pages = page_table.shape
    page = cache.shape[3]
    nc = n_pages * page // CMP_STRIDE
    g = PAGES_PER_STEP
    wk, wv, pe2, w2p = cw
    out_spec = pl.BlockSpec((1, nc, LANES), lambda i, s, pt: (i, 0, 0))
    return _sample_call(
        functools.partial(_compress_sample_kernel, n_pages_step=g), "nsa_compress_sample", page_table, n_pages // g,
        [wk, wv, pe2, w2p], [_const_block(wk.shape), _const_block(wv.shape), _const_block(pe2.shape),
                             _const_block(w2p.shape)],
        [cache] * g, _page_specs(layer, g, (1, 1, 2 * LANES, page), 0, lambda s, gp: s * g + gp),
        [out_spec, out_spec], _compress_out(b, nc),
        [pltpu.VMEM((nc, CMP_STRIDE * LANES), _F32)] * 2 + [pltpu.VMEM((2, page, LANES), _F32)])


def _nsa_sample_kernel(pt_ref, q_ref, g_ref, ck_ref, cv_ref, new_ref, neww_ref, win_ref, cover_ref, *refs,
                       n_pages_step, n_new, past_len, n_top):
    del pt_ref
    pages = refs[:n_pages_step]
    o_ref, m_ref, acc_ref, bias_ref, ocmp_ref, owin_ref = refs[n_pages_step:]
    s = pl.program_id(1)
    tq = SAMPLE_ROWS
    rows = NSA_HEADS * tq
    qs = _stack_nsa_queries(q_ref[0])
    qi_r = lax.broadcasted_iota(jnp.int32, (rows, 1), 0) % tq
    qpos_r = past_len + qi_r

    @pl.when(s == 0)
    def _():
        qf = qs.astype(_F32)
        o_cmp, psums = _cmp_branch(qs, ck_ref[0], cv_ref[0], qpos_r, tq)
        ocmp_ref[...] = o_cmp
        qp = past_len + lax.broadcasted_iota(jnp.int32, (tq, 1), 0)
        biases = [_nsa_select(_dot_split(psum, cover_ref[...]), qp, past_len // SEL_BLOCK, n_top - 1, 1)
                  for psum in psums]
        bias_ref[...] = jnp.concatenate([b for b in biases for _ in range(NSA_GROUP)], axis=0)
        visible = lambda j: j <= qi_r
        new = new_ref[0].astype(_F32)
        state = _seed_state(*_new_key_softmax(qf, new[:, 2 * LANES:3 * LANES], new[:, 3 * LANES:], n_new, visible))
        _state_refs_store(m_ref, acc_ref, state)
        wnew = neww_ref[0].astype(_F32)
        wstate = _seed_state(*_new_key_softmax(qf, wnew[:, :LANES], wnew[:, LANES:], n_new, visible))
        w = win_ref[0, 0]
        n_win = w.shape[1]
        dist = qpos_r - (past_len - n_win + lax.broadcasted_iota(jnp.int32, (1, n_win), 1))
        z = jnp.where((dist >= 0) & (dist <= WINDOW), _dot(qs, w[:LANES].astype(MXU_DTYPE)), NEG)
        owin_ref[...] = _finish1(_online_update1(wstate, z, _with_ones(w[LANES:].astype(MXU_DTYPE), 0), nt=True))

    n_keys = n_pages_step * pages[0].shape[3]
    lhs = jnp.concatenate([qs, bias_ref[...].astype(MXU_DTYPE)], axis=1)
    rhs = jnp.concatenate([_side_by_side(pages, 0, LANES), _key_block_onehot(n_keys, s * n_keys, SEL_BLOCK)], axis=0)
    state = (m_ref[:, 0:1], acc_ref[...])
    state = _online_update1(state, _dot(lhs, rhs), _with_ones(_side_by_side(pages, LANES, 2 * LANES), 0), nt=True)
    _state_refs_store(m_ref, acc_ref, state)

    @pl.when(s == pl.num_programs(1) - 1)
    def _():
        o_sel = _finish1((m_ref[:, 0:1], acc_ref[...]))
        o_ref[0] = _combine_nsa(g_ref[0], ocmp_ref[...], o_sel, owin_ref[...], tq).astype(o_ref.dtype)


def _nsa_sample(q, gates, cmp_k, cmp_v, new_bf, new_win_bf, cache, cache_win, layer, page_table, cover, n_new):
    b, n_pages = page_table.shape
    page = cache.shape[3]
    past_len = n_pages * page
    g = PAGES_PER_STEP
    rows = NSA_HEADS * SAMPLE_ROWS
    n_top = min(SEL_TOPN, past_len // SEL_BLOCK + 1)
    win_spec = pl.BlockSpec((1, 1) + cache_win.shape[2:], lambda i, s, pt: (layer, i, 0, 0))
    out_shape = (b, SAMPLE_ROWS, NSA_Q_COLS)
    vmem = lambda w: pltpu.VMEM((rows, w), _F32)
    return _sample_call(
        functools.partial(_nsa_sample_kernel, n_pages_step=g, n_new=n_new, past_len=past_len, n_top=n_top),
        "nsa_sample", page_table, n_pages // g,
        [q, gates, cmp_k, cmp_v, new_bf, new_win_bf, cache_win, cover],
        [_seq_block(q.shape), _seq_block(gates.shape), _seq_block(cmp_k.shape), _seq_block(cmp_v.shape),
         _seq_block(new_bf.shape), _seq_block(new_win_bf.shape), win_spec, _const_block(cover.shape)],
        [cache] * g, _page_specs(layer, g, (1, 1, 2 * LANES, page), 1, lambda s, gp: s * g + gp),
        _seq_block(out_shape), jax.ShapeDtypeStruct(out_shape, MXU_DTYPE),
        [vmem(LANES), vmem(2 * LANES), vmem(LANES), vmem(LANES), vmem(LANES)])


def _sb_sample_kernel(pt_ref, q_ref, new_ref, *refs, n_pages_step, n_new):
    del pt_ref
    pages = refs[:n_pages_step]
    o_ref, c_ref, acc_ref = refs[n_pages_step:]
    s = pl.program_id(1)
    width = SB_HEADS * HEAD_DIM
    qs = _stack_heads(q_ref[0], SB_HEADS)
    rows = qs.shape[0]
    qi_r = lax.broadcasted_iota(jnp.int32, (rows, 1), 0) % SAMPLE_ROWS

    @pl.when(s == 0)
    def _():
        qf = qs.astype(_F32)
        new = new_ref[0].astype(_F32)
        zs = _row_scores(qf, new[:, :width], n_new)
        c = jnp.zeros((rows, 1), _F32)
        o = jnp.zeros((rows, width), _F32)
        for j in reversed(range(n_new)):
            seen = j < qi_r
            lk = jnp.where(seen, -_softplus(zs[j]), 0.0)
            a = jnp.where(seen, jnp.exp(lk + zs[j] + c), 0.0)
            o = o + a * new[j:j + 1, width:]
            c = c + lk
        c_ref[...] = jnp.broadcast_to(c, c_ref.shape)
        acc_ref[...] = o

    page_keys = pages[0].shape[3]
    cum = _cumsum_matrix(page_keys)
    z = _dot(qs, _side_by_side(pages, 0, width))
    lk = -_softplus(z)
    c = c_ref[...]
    later = []
    for gp in range(n_pages_step):
        r = _dot_split(lk[:, gp * page_keys:(gp + 1) * page_keys], cum)
        later.append(r[:, :page_keys] + c)
        c = c + r[:, page_keys:]
    a = jnp.exp(lk + z + jnp.concatenate(later, axis=1))
    acc_ref[...] += _dot_nt(a.astype(MXU_DTYPE), _side_by_side(pages, width, 2 * width))
    c_ref[...] = c

    @pl.when(s == pl.num_programs(1) - 1)
    def _():
        o_ref[0] = _unstack_heads(acc_ref[...], SB_HEADS).astype(o_ref.dtype)


def _sb_sample(q, new_bf, cache, layer, page_table, n_new):
    b, n_pages = page_table.shape
    page = cache.shape[3]
    g = PAGES_PER_STEP
    rows = SB_HEADS * SAMPLE_ROWS
    width = SB_HEADS * HEAD_DIM
    return _sample_call(
        functools.partial(_sb_sample_kernel, n_pages_step=g, n_new=n_new),
        "sb_sample", page_table, n_pages // g,
        [q, new_bf], [_seq_block(q.shape), _seq_block(new_bf.shape)],
        [cache] * g, _page_specs(layer, g, (1, 1, 2 * width, page), 0, lambda s, gp: n_pages - 1 - (s * g + gp)),
        _seq_block(q.shape), jax.ShapeDtypeStruct(q.shape, MXU_DTYPE),
        [pltpu.VMEM((rows, LANES), _F32), pltpu.VMEM((rows, width), _F32)])


def _moba_sample_kernel(pt_ref, q_ref, new_ref, *refs, n_pages_step, n_new, past_len):
    del pt_ref
    pages = refs[:n_pages_step]
    o_ref, m_ref, part_ref, km_ref = refs[n_pages_step:]
    s = pl.program_id(1)
    width = MOBA_HEADS * HEAD_DIM
    page_keys = pages[0].shape[3]
    per_blk = MOBA_BLOCK // page_keys
    blk_step = n_pages_step // per_blk
    n_blk = past_len // MOBA_BLOCK
    qf = _stack_heads(q_ref[0], MOBA_HEADS)
    rows = qf.shape[0]
    qs = (qf * HEAD_DIM ** -0.5).astype(MXU_DTYPE)
    qi_r = lax.broadcasted_iota(jnp.int32, (rows, 1), 0) % SAMPLE_ROWS
    lane = lax.broadcasted_iota(jnp.int32, (1, LANES), 1)

    @pl.when(s == 0)
    def _():
        km_ref[...] = jnp.zeros_like(km_ref)

    km = km_ref[...]
    for jb in range(blk_step):
        blk = s * blk_step + jb
        mine = pages[jb * per_blk:(jb + 1) * per_blk]
        mean = sum(jnp.sum(p[0, 0, :width, :], axis=1, keepdims=True) for p in mine) * (1.0 / MOBA_BLOCK)
        km = km + jnp.where(lane == blk, mean, 0.0)
        z = _dot(qs, _side_by_side(mine, 0, width))
        m = jnp.max(z, axis=-1, keepdims=True)
        p = jnp.exp(z - m).astype(MXU_DTYPE)
        part_ref[blk] = _dot_nt(p, _with_ones(_side_by_side(mine, width, 2 * width), 0))
        m_ref[blk] = jnp.broadcast_to(m, (rows, LANES))
    km_ref[...] = km

    @pl.when(s == pl.num_programs(1) - 1)
    def _():
        gate = lax.dot_general(qf, km_ref[...], (((1,), (0,)), ((), ())), preferred_element_type=_F32,
                               precision=lax.Precision.HIGHEST)
        sel = _moba_pick(gate, n_blk, 1) + 1.0
        new = new_ref[0].astype(_F32)
        m0, l0, acc0 = _new_key_softmax(qs.astype(_F32), new[:, :width], new[:, width:], n_new, lambda j: j <= qi_r)
        m_all = jnp.full((rows, LANES), NEG, _F32)
        for j in range(n_blk):
            m_all = jnp.where(lane == j, m_ref[j], m_all)
        m_top = jnp.maximum(m0, jnp.max(jnp.where(sel > 0.0, m_all, NEG), axis=-1, keepdims=True))
        w = sel * jnp.exp(m_all - m_top)
        out = jnp.exp(m0 - m_top) * _seed_state(m0, l0, acc0)[1]
        for j in range(n_blk):
            out = out + w[:, j:j + 1] * part_ref[j]
        o_ref[0] = _unstack_heads(_finish1((m_top, out)), MOBA_HEADS).astype(o_ref.dtype)


def _moba_sample(q, new_bf, cache, layer, page_table, n_new):
    b, n_pages = page_table.shape
    page = cache.shape[3]
    g = PAGES_PER_STEP
    rows = MOBA_HEADS * SAMPLE_ROWS
    width = MOBA_HEADS * HEAD_DIM
    n_blk = n_pages * page // MOBA_BLOCK
    out_shape = (b, SAMPLE_ROWS, width)
    return _sample_call(
        functools.partial(_moba_sample_kernel, n_pages_step=g, n_new=n_new, past_len=n_pages * page),
        "moba_sample", page_table, n_pages // g,
        [q, new_bf], [_seq_block(q.shape), _seq_block(new_bf.shape)],
        [cache] * g, _page_specs(layer, g, (1, 1, 2 * width, page), 0, lambda s, gp: s * g + gp),
        _seq_block(out_shape), jax.ShapeDtypeStruct(out_shape, MXU_DTYPE),
        [pltpu.VMEM((n_blk, rows, LANES), _F32), pltpu.VMEM((n_blk, rows, width + LANES), _F32),
         pltpu.VMEM((width, LANES), _F32)])


def _rope_tables(pos):
    half = HEAD_DIM // 2
    inv = ROPE_THETA ** (-jnp.arange(half, dtype=_F32) / half)
    ang = pos.astype(_F32)[:, None] * inv[None, :]
    cos, sin = jnp.cos(ang), jnp.sin(ang)
    zero = jnp.zeros_like(sin)
    reps = LANES // HEAD_DIM
    return (jnp.tile(cos, (1, 2 * reps)), jnp.tile(jnp.concatenate([-sin, zero], axis=1), (1, reps)),
            jnp.tile(jnp.concatenate([zero, sin], axis=1), (1, reps)))


def _pack_w_in(w_in):
    d = w_in.shape[0]
    o1 = NSA_Q_COLS
    o2 = o1 + NSA_KV_COLS
    o3 = o2 + NSA_G_COLS
    o4 = o3 + SB_COLS
    q, kv, g, sb, mb = w_in[:, :o1], w_in[:, o1:o2], w_in[:, o2:o3], w_in[:, o3:o4], w_in[:, o4:]
    zero = jnp.zeros((d, HEAD_DIM), w_in.dtype)
    q_groups = []
    for hh in range(NSA_HEADS):
        w = q[:, hh * HEAD_DIM:(hh + 1) * HEAD_DIM]
        q_groups += [w, zero] if hh // NSA_GROUP == 0 else [zero, w]
    gate = jnp.pad(g, ((0, 0), (0, LANES - NSA_G_COLS)))
    return jnp.concatenate(q_groups + [kv, sb, mb, gate], axis=1).astype(MXU_DTYPE)


def _block_tables(t):
    key = jnp.arange(t)[:, None]
    j = jnp.arange(LANES)[None, :]
    eblk = jnp.where(key // SEL_BLOCK == j, -NEG, 0.0).astype(MXU_DTYPE)
    eblk_mb = jnp.where(key // MOBA_BLOCK == j, -NEG, 0.0).astype(MXU_DTYPE)
    i = jnp.arange(t // CMP_STRIDE)[:, None]
    ratio = SEL_BLOCK // CMP_STRIDE
    cover = ((i <= ratio * j + ratio - 1) & (i >= ratio * j - (CMP_LEN // CMP_STRIDE - 1))).astype(MXU_DTYPE)
    return eblk, eblk_mb, cover


def _layer_weights(l, norm_g, ffn_w_gate, ffn_w_up, ffn_w_down, w_in, nsa_cmp_pe, nsa_cmp_w1, nsa_cmp_w2,
                   w_branch_nsa, w_branch_sb, w_branch_moba, w_merge_gate, w_out, w_ple_proj, w_ple_gate):
    c = lambda w: w.astype(MXU_DTYPE)
    d, f = ffn_w_gate.shape[2:]
    tf = _ff_tile(f)
    cols = lambda w: c(w).reshape(d, f // tf, tf).transpose(1, 0, 2)
    return dict(
        g=[norm_g[l, i][None, :] for i in range(norm_g.shape[1])],
        ffn=[(cols(ffn_w_gate[l, i]), cols(ffn_w_up[l, i]), c(ffn_w_down[l, i]).reshape(f // tf, tf, d))
             for i in range(2)],
        w_all=_pack_w_in(w_in[l]),
        cw=_compress_weights(nsa_cmp_pe[l], nsa_cmp_w1[l], nsa_cmp_w2[l]),
        merge=(c(w_branch_nsa[l]), c(w_branch_sb[l]), c(w_branch_moba[l]), c(w_merge_gate[l]), c(w_out[l])),
        ple=(c(w_ple_gate[l]), c(w_ple_proj[l])),
    )


def _token_tile(n, cap):
    tm = cap
    while n % tm:
        tm //= 2
    return tm


def _ff_tile(f, cap=512):
    best = LANES
    for k in range(1, f // LANES + 1):
        if f % (k * LANES) == 0 and k * LANES <= cap:
            best = k * LANES
    return best


def _prompt_layer(h, p_l, lw, tabs, consts, b, t, depth, layer, carried):
    n, d = h.shape
    tm = _token_tile(t, 512)
    tm_ffn = _token_tile(t, 1024)
    tk = 8 * LANES
    g = lw["g"]
    h = _ffn(h, g[0], g[1], *lw["ffn"][0], tm=tm_ffn)
    per_seq = t // tm
    pr = _inproj(h, g[2], lw["w_all"], tabs, lambda i: (i % per_seq, 0), tm=tm, stack=(depth, layer, b, t),
                 carried=carried)
    stacked = tuple(pr.pop(k) for k in _INPROJ_STATE)
    kmean = pr.pop("kmean").reshape(b, per_seq, SUBLANES, -1)[:, :, :tm // MOBA_BLOCK].reshape(b, t // MOBA_BLOCK, -1)
    kmean = jnp.pad(kmean, ((0, 0), (0, LANES - kmean.shape[1]), (0, 0)))
    pr = {k: v.reshape(b, t, v.shape[1]) for k, v in pr.items()}
    eblk, eblk_mb, cover = consts
    cmp_k, cmp_v = _compress_prompt(pr["cmp_rows"], lw["cw"])
    o_nsa = _nsa_prompt(pr["q_nsa"], pr["gates"], cmp_k, cmp_v, pr["nsa_bf"], pr["win_bf"], eblk, cover,
                        tq=2 * LANES, tk=tk)
    o_sb = _sb_prompt(pr["q_sb"], pr["sb_bf"], tq=2 * LANES, tk=LANES)
    o_mb = _moba_prompt(pr["q_mb"], kmean, pr["mb_bf"], eblk_mb, tk=tk)
    flat = lambda o: o.reshape(n, o.shape[2])
    h = _merge(h, g[2], g[3], flat(o_nsa), flat(o_sb), flat(o_mb), *lw["merge"], tm=tm)
    h = _ffn(h, g[4], g[5], *lw["ffn"][1], ple=(p_l, *lw["ple"], g[6]), tm=tm_ffn)
    return h, stacked, pr["win"]


def _sample_layer(h, p_l, lw, tabs, cover, caches, layer, page_table, n_new):
    n, d = h.shape
    b = page_table.shape[0]
    tm = _token_tile(n, 512)
    g = lw["g"]
    cache_nsa, cache_sb, cache_mb, cache_win = caches
    h = _ffn(h, g[0], g[1], *lw["ffn"][0], tm=tm)
    pr = _inproj(h, g[2], lw["w_all"], tabs, lambda i: (i, 0), tm=tm)
    pr = {k: v.reshape(b, SAMPLE_ROWS, v.shape[1]) for k, v in pr.items()}
    cmp_k, cmp_v = _compress_sample(cache_nsa, layer, page_table, lw["cw"])
    o_nsa = _nsa_sample(pr["q_nsa"], pr["gates"], cmp_k, cmp_v, pr["nsa_bf"], pr["win_bf"], cache_nsa, cache_win,
                        layer, page_table, cover, n_new)
    o_sb = _sb_sample(pr["q_sb"], pr["sb_bf"], cache_sb, layer, page_table, n_new)
    o_mb = _moba_sample(pr["q_mb"], pr["mb_bf"], cache_mb, layer, page_table, n_new)
    flat = lambda o: o.reshape(n, o.shape[2])
    h = _merge(h, g[2], g[3], flat(o_nsa), flat(o_sb), flat(o_mb), *lw["merge"], tm=tm)
    h = _ffn(h, g[4], g[5], *lw["ffn"][1], ple=(p_l, *lw["ple"], g[6]), tm=tm)
    return h, tuple(pr[k][:, :n_new] for k in ("nsa", "sb", "mb", "win"))


def kernel(x_prompt, x_sample, p_prompt, p_sample, cache_nsa, cache_sb, cache_moba, cache_win, page_table,
           norm_g, ffn_w_gate, ffn_w_up, ffn_w_down, w_in, nsa_cmp_pe, nsa_cmp_w1, nsa_cmp_w2,
           w_branch_nsa, w_branch_sb, w_branch_moba, w_merge_gate, w_out, w_ple_proj, w_ple_gate):
    depth = norm_g.shape[0]
    b, t, d = x_prompt.shape
    bs, ts, _ = x_sample.shape
    n_pages = page_table.shape[1]
    page = cache_nsa.shape[2]
    past_len = n_pages * page
    assert ts <= SAMPLE_ROWS and t % (8 * LANES) == 0 and t >= WINDOW + LANES
    assert t // SEL_BLOCK <= LANES and past_len // SEL_BLOCK <= LANES
    assert n_pages % PAGES_PER_STEP == 0 and MOBA_BLOCK % page == 0 and cache_win.shape[2] == WINDOW
    weights = (norm_g, ffn_w_gate, ffn_w_up, ffn_w_down, w_in, nsa_cmp_pe, nsa_cmp_w1, nsa_cmp_w2,
               w_branch_nsa, w_branch_sb, w_branch_moba, w_merge_gate, w_out, w_ple_proj, w_ple_gate)
    tabs_p = _rope_tables(jnp.arange(t, dtype=jnp.int32))
    tabs_s = _rope_tables(past_len + jnp.arange(bs * SAMPLE_ROWS, dtype=jnp.int32) % SAMPLE_ROWS)
    consts = _block_tables(t)
    cover_s = _block_tables(past_len)[2]
    pad_rows = lambda x: jnp.pad(x, ((0, 0), (0, SAMPLE_ROWS - ts), (0, 0))).reshape(bs * SAMPLE_ROWS, -1)
    feature_major = lambda c: jnp.transpose(c, (0, 1, 3, 4, 5, 2)).reshape(c.shape[0], c.shape[1], -1, c.shape[2])
    caches = tuple(feature_major(c) for c in (cache_nsa, cache_sb, cache_moba, cache_win))
    h_p = x_prompt.reshape(b * t, d)
    h_s = pad_rows(x_sample)
    stacked_p, win_p, st_s = (), [], []
    keep = min(WINDOW, t)
    for l in range(depth):
        lw = _layer_weights(l, *weights)
        h_p, stacked_p, win = _prompt_layer(h_p, p_prompt[l].reshape(b * t, -1), lw, tabs_p, consts, b, t, depth, l,
                                            stacked_p)
        win_p.append(win[:, t - keep:])
        h_s, rows = _sample_layer(h_s, pad_rows(p_sample[l]), lw, tabs_s, cover_s, caches, l, page_table, ts)
        win_fm = jnp.concatenate([caches[3][l][:, :, ts:], jnp.swapaxes(rows[3], 1, 2)], axis=2)
        st_s.append(rows[:3] + (jnp.swapaxes(win_fm, 1, 2),))
    heads = ((4, NSA_KV_HEADS), (2, SB_HEADS), (2, MOBA_HEADS))
    out = [h_p.reshape(b, t, d), h_s.reshape(bs, SAMPLE_ROWS, d)[:, :ts]]
    for i, (parts, nh) in enumerate(heads):
        out.append(jnp.swapaxes(stacked_p[i], 2, 3).reshape(depth, b, t, parts, nh, HEAD_DIM))
        out.append(jnp.stack([s[i] for s in st_s]).reshape(depth, bs, ts, parts, nh, HEAD_DIM))
    out.append(jnp.stack(win_p).reshape(depth, b, keep, 2, NSA_KV_HEADS, HEAD_DIM))
    out.append(jnp.stack([s[3] for s in st_s]).reshape(depth, bs, -1, 2, NSA_KV_HEADS, HEAD_DIM))
    return tuple(out)
```

```python
import functools

import jax
import jax.numpy as jnp
from jax import lax
from jax.experimental import pallas as pl
from jax.experimental.pallas import tpu as pltpu

HEAD_DIM = 64
NSA_HEADS = 8
NSA_KV_HEADS = 2
NSA_GROUP = NSA_HEADS // NSA_KV_HEADS
SB_HEADS = 4
MOBA_HEADS = 4
CMP_LEN = 32
CMP_STRIDE = 16
SEL_BLOCK = 64
SEL_TOPN = 16
WINDOW = 512
MOBA_BLOCK = 256
MOBA_TOPK = 3
ROPE_THETA = 10000.0
RMS_EPS = 1e-6
NEG = -1e30
FORCE_BONUS = 1e4
SB_DEAD = -104.0

LANES = 128
SUBLANES = 8
MXU_DTYPE = jnp.bfloat16
VMEM_LIMIT = 56 * 1024 * 1024
SAMPLE_ROWS = 8
PAGES_PER_STEP = 32
NSA_ROW_CHUNKS = 4

NSA_Q_COLS = NSA_HEADS * HEAD_DIM
NSA_KV_COLS = 6 * NSA_KV_HEADS * HEAD_DIM
NSA_G_COLS = 3 * NSA_HEADS
SB_COLS = 3 * SB_HEADS * HEAD_DIM
MOBA_COLS = 3 * MOBA_HEADS * HEAD_DIM

_F32 = jnp.float32


def _dot(a, b):
    return jnp.dot(a, b, preferred_element_type=_F32)


def _dot_nt(a, b):
    return lax.dot_general(a, b, (((1,), (1,)), ((), ())), preferred_element_type=_F32)


def _dot_split(x, w):
    hi = x.astype(MXU_DTYPE)
    mid = (x - hi.astype(_F32)).astype(MXU_DTYPE)
    return _dot(hi, w) + _dot(mid, w)


def _rms(x, g):
    return x * lax.rsqrt(jnp.mean(x * x, axis=-1, keepdims=True) + RMS_EPS) * g


def _params(*sem):
    return pltpu.CompilerParams(dimension_semantics=sem, vmem_limit_bytes=VMEM_LIMIT)


def _full(shape):
    n = len(shape)
    return pl.BlockSpec(shape, lambda *_: (0,) * n)


def _ffn_kernel(*refs, with_ple):
    if with_ple:
        (h_ref, gpre_ref, gpost_ref, wg_ref, wu_ref, wd_ref,
         p_ref, wpg_ref, wpp_ref, gple_ref, o_ref, xn_ref, acc_ref) = refs
    else:
        h_ref, gpre_ref, gpost_ref, wg_ref, wu_ref, wd_ref, o_ref, xn_ref, acc_ref = refs
    xn_ref[...] = _rms(h_ref[...], gpre_ref[...]).astype(MXU_DTYPE)
    acc_ref[...] = jnp.zeros_like(acc_ref)

    def chunk(j, carry):
        xn = xn_ref[...]
        g = _dot(xn, wg_ref[j])
        u = _dot(xn, wu_ref[j])
        a = (g * jax.nn.sigmoid(g)) * u
        acc_ref[...] += _dot(a.astype(MXU_DTYPE), wd_ref[j])
        return carry

    lax.fori_loop(0, wg_ref.shape[0], chunk, 0)
    h = h_ref[...] + 0.5 * _rms(acc_ref[...], gpost_ref[...])
    if with_ple:
        gate = jax.nn.sigmoid(_dot(h.astype(MXU_DTYPE), wpg_ref[...]))
        ple = gate * _dot(p_ref[...].astype(MXU_DTYPE), wpp_ref[...])
        h = h + _rms(ple, gple_ref[...])
    o_ref[...] = h


def _resident(shape):
    n = len(shape)
    return pl.BlockSpec(shape, lambda *_: (0,) * n, pipeline_mode=pl.Buffered(1))


def _ffn(h, g_pre, g_post, wg, wu, wd, ple=None, *, tm):
    n, d = h.shape
    row = lambda i: (i, 0)
    in_specs = [pl.BlockSpec((tm, d), row), _full((1, d)), _full((1, d)),
                _resident(wg.shape), _resident(wu.shape), _resident(wd.shape)]
    args = [h, g_pre, g_post, wg, wu, wd]
    if ple is not None:
        p, wpg, wpp, g_ple = ple
        in_specs += [pl.BlockSpec((tm, p.shape[1]), row), _resident(wpg.shape), _resident(wpp.shape), _full((1, d))]
        args += [p, wpg, wpp, g_ple]
    return pl.pallas_call(
        functools.partial(_ffn_kernel, with_ple=ple is not None),
        grid=(n // tm,),
        in_specs=in_specs,
        out_specs=pl.BlockSpec((tm, d), row),
        out_shape=jax.ShapeDtypeStruct((n, d), _F32),
        scratch_shapes=[pltpu.VMEM((tm, d), MXU_DTYPE), pltpu.VMEM((tm, d), _F32)],
        compiler_params=_params("parallel"),
        name="ffn_ple" if ple is not None else "ffn",
    )(*args)


_G_QNSA = 0
_G_KV = 8
_G_SB = 14
_G_MB = 20
_G_GATE = 26
_ROPE_GROUPS = tuple(range(8)) + (8, 10, 12, 20, 21, 22, 23)
_INPROJ_OUT = (("q_nsa", _G_QNSA, 8, True, None), ("nsa", _G_KV, 4, False, _F32), ("win", _G_KV + 4, 2, False, _F32),
               ("q_sb", _G_SB, 2, True, None), ("sb", _G_SB + 2, 4, False, _F32),
               ("q_mb", _G_MB, 2, False, _F32), ("mb", _G_MB + 2, 4, False, _F32))
_INPROJ_NARROW = ("nsa", "win", "sb", "mb")


_INPROJ_STATE = ("nsa", "sb", "mb")


def _inproj_kernel(h_ref, g_ref, w_ref, cos_ref, sa_ref, sb_ref, *refs, n_carried, feature_major):
    out_refs = refs[n_carried:]
    n_stacked = feature_major
    xn = _rms(h_ref[...], g_ref[...]).astype(MXU_DTYPE)
    cos, sa, sb = cos_ref[...], sa_ref[...], sb_ref[...]
    scale = HEAD_DIM ** -0.5
    tm = xn.shape[0]

    pairs = {}

    def group(gi):
        if gi not in pairs:
            g0 = gi - gi % 2
            wide = _dot(xn, w_ref[:, g0 * LANES:min(g0 + 2, _G_GATE + 1) * LANES])
            for k in range(wide.shape[1] // LANES):
                pairs[g0 + k] = wide[:, k * LANES:(k + 1) * LANES]
        y = pairs.pop(gi)
        if gi in _ROPE_GROUPS:
            y = y * cos + pltpu.roll(y, LANES - HEAD_DIM // 2, 1) * sa + pltpu.roll(y, HEAD_DIM // 2, 1) * sb
        return y

    outs = out_refs[:len(_INPROJ_OUT)]
    narrow = dict(zip(_INPROJ_NARROW, out_refs[len(_INPROJ_OUT):]))
    gate_ref = out_refs[len(_INPROJ_OUT) + len(_INPROJ_NARROW)]
    means = []
    for (name, g0, n_groups, scaled, _), o_ref in zip(_INPROJ_OUT, outs):
        for i in range(n_groups):
            y = group(g0 + i)
            if scaled:
                y = y * scale
            if n_stacked and name in _INPROJ_STATE:
                o_ref[0, 0, i * LANES:(i + 1) * LANES, :] = y.T
            else:
                o_ref[:, i * LANES:(i + 1) * LANES] = y.astype(o_ref.dtype)
            if name in narrow:
                narrow[name][:, i * LANES:(i + 1) * LANES] = y.astype(MXU_DTYPE)
            if n_stacked and name == "nsa" and i < 2:
                out_refs[-2][:, i * LANES:(i + 1) * LANES] = y
            if n_stacked and name == "mb" and i < 2:
                means.append([jnp.mean(y[r:r + MOBA_BLOCK], axis=0, keepdims=True) for r in range(0, tm, MOBA_BLOCK)])
    gate_ref[...] = jax.nn.sigmoid(group(_G_GATE))
    if n_stacked:
        rows = [jnp.concatenate([means[0][r], means[1][r]], axis=1) for r in range(tm // MOBA_BLOCK)]
        rows.append(jnp.zeros((SUBLANES - len(rows), 2 * LANES), _F32))
        out_refs[-1][...] = jnp.concatenate(rows, axis=0)


def _inproj(h, g, w_all, tabs, tab_map, *, tm, stack=None, carried=()):
    n, d = h.shape
    row = lambda i: (i, 0)
    tab_spec = pl.BlockSpec((tm, LANES), tab_map)
    sizes = {o[0]: o[2] * LANES for o in _INPROJ_OUT}
    names = [o[0] for o in _INPROJ_OUT] + [k + "_bf" for k in _INPROJ_NARROW] + ["gates"]
    shapes = [jax.ShapeDtypeStruct((n, o[2] * LANES), o[4] or MXU_DTYPE) for o in _INPROJ_OUT]
    shapes += [jax.ShapeDtypeStruct((n, sizes[k]), MXU_DTYPE) for k in _INPROJ_NARROW]
    shapes.append(jax.ShapeDtypeStruct((n, LANES), _F32))
    specs = [pl.BlockSpec((tm, s.shape[1]), row) for s in shapes]
    aliases = {}
    n_fixed_inputs = 6
    if stack is not None:
        depth, layer, b, t = stack
        per_seq = t // tm
        assert tm % MOBA_BLOCK == 0 and tm // MOBA_BLOCK <= SUBLANES and len(carried) in (0, len(_INPROJ_STATE))
        for k, name in enumerate(_INPROJ_STATE):
            idx = names.index(name)
            shapes[idx] = jax.ShapeDtypeStruct((depth, b, sizes[name], t), _F32)
            specs[idx] = pl.BlockSpec((1, 1, sizes[name], tm), lambda i: (layer, i // per_seq, 0, i % per_seq))
            if carried:
                aliases[n_fixed_inputs + k] = idx
        names += ["cmp_rows", "kmean"]
        shapes += [jax.ShapeDtypeStruct((n, 2 * LANES), _F32),
                   jax.ShapeDtypeStruct((n // tm * SUBLANES, 2 * LANES), _F32)]
        specs += [pl.BlockSpec((tm, 2 * LANES), row), pl.BlockSpec((SUBLANES, 2 * LANES), row)]
    outs = pl.pallas_call(
        functools.partial(_inproj_kernel, n_carried=len(carried), feature_major=stack is not None),
        grid=(n // tm,),
        in_specs=[pl.BlockSpec((tm, d), row), _full((1, d)), _full(w_all.shape), tab_spec, tab_spec, tab_spec]
        + [pl.BlockSpec(memory_space=pl.ANY)] * len(carried),
        out_specs=specs,
        out_shape=shapes,
        input_output_aliases=aliases,
        compiler_params=_params("parallel"),
        name="inproj",
    )(h, g, w_all, *tabs, *carried)
    return dict(zip(names, outs))


def _merge_kernel(h_ref, gpre_ref, gpost_ref, onsa_ref, osb_ref, omb_ref,
                  wbn_ref, wbs_ref, wbm_ref, wgate_ref, wout_ref, o_ref):
    d = h_ref.shape[1]
    h = h_ref[...]
    u = _rms(h, gpre_ref[...]).astype(MXU_DTYPE)
    branches = (_dot(onsa_ref[...], wbn_ref[...]), _dot(osb_ref[...], wbs_ref[...]), _dot(omb_ref[...], wbm_ref[...]))
    mix = None
    for c, br in enumerate(branches):
        term = jax.nn.sigmoid(_dot(u, wgate_ref[:, c * d:(c + 1) * d])) * br
        mix = term if mix is None else mix + term
    y = _dot(mix.astype(MXU_DTYPE), wout_ref[...])
    o_ref[...] = h + _rms(y, gpost_ref[...])


def _merge(h, g_pre, g_post, o_nsa, o_sb, o_mb, wbn, wbs, wbm, wgate, wout, *, tm):
    n, d = h.shape
    row = lambda i: (i, 0)
    return pl.pallas_call(
        _merge_kernel,
        grid=(n // tm,),
        in_specs=[pl.BlockSpec((tm, d), row), _full((1, d)), _full((1, d)),
                  pl.BlockSpec((tm, o_nsa.shape[1]), row), pl.BlockSpec((tm, o_sb.shape[1]), row),
                  pl.BlockSpec((tm, o_mb.shape[1]), row),
                  _full(wbn.shape), _full(wbs.shape), _full(wbm.shape), _full(wgate.shape), _full(wout.shape)],
        out_specs=pl.BlockSpec((tm, d), row),
        out_shape=jax.ShapeDtypeStruct((n, d), _F32),
        compiler_params=_params("parallel"),
        name="merge",
    )(h, g_pre, g_post, o_nsa, o_sb, o_mb, wbn, wbs, wbm, wgate, wout)


def _lane_lo(shape=(1, LANES)):
    return (lax.broadcasted_iota(jnp.int32, shape, len(shape) - 1) % LANES) < HEAD_DIM


def _softplus(z):
    return jnp.maximum(z, 0.0) + jnp.log1p(jnp.exp(-jnp.abs(z)))


def _cumsum_matrix(tk):
    j = lax.broadcasted_iota(jnp.int32, (tk, 2 * tk), 0)
    s = lax.broadcasted_iota(jnp.int32, (tk, 2 * tk), 1)
    return jnp.where((j > s) | (s >= tk), 1.0, 0.0).astype(MXU_DTYPE)


def _online_update1(state, z, v1, nt=False):
    m, acc = state
    m_new = jnp.maximum(m, jnp.max(z, axis=-1, keepdims=True))
    p = jnp.exp(z - m_new).astype(MXU_DTYPE)
    return m_new, jnp.exp(m - m_new) * acc + (_dot_nt(p, v1) if nt else _dot(p, v1))


def _online_init1(rows, width=LANES):
    return jnp.full((rows, 1), NEG, _F32), jnp.zeros((rows, width + LANES), _F32)


def _finish1(state):
    _, acc = state
    width = acc.shape[1] - LANES
    den = 1.0 / acc[:, width:]
    return acc[:, :width] * jnp.concatenate([den] * (width // LANES), axis=1)


def _with_ones(v, axis=1):
    shape = (v.shape[0], LANES) if axis == 1 else (LANES, v.shape[1])
    return jnp.concatenate([v, jnp.ones(shape, v.dtype)], axis=axis)


def _seed_state(m, l, acc):
    return m, jnp.concatenate([acc, jnp.broadcast_to(l, (l.shape[0], LANES))], axis=1)


def _topk_mask(val, k, axis):
    idx = lax.broadcasted_iota(jnp.int32, val.shape, axis).astype(_F32)
    sel = jnp.zeros(val.shape, _F32)
    for _ in range(k):
        m = jnp.max(val, axis=axis, keepdims=True)
        first = jnp.min(jnp.where(val == m, idx, 1e9), axis=axis, keepdims=True)
        hit = idx == first
        sel = jnp.where(hit, 1.0, sel)
        val = jnp.where(hit, -jnp.inf, val)
    return sel


def _stack_nsa_queries(q):
    return jnp.concatenate([q[:, hh * LANES:(hh + 1) * LANES] for hh in range(NSA_HEADS)], axis=0)


def _cmp_branch(qs, kc, vc, qpos, tq):
    nc = kc.shape[0]
    s = _dot_nt(qs, kc)
    cend = lax.broadcasted_iota(jnp.int32, (1, nc), 1) * CMP_STRIDE + (CMP_LEN - 1)
    vis = cend <= qpos
    sm = jnp.where(vis, s, NEG)
    m = jnp.max(sm, axis=-1, keepdims=True)
    e = jnp.where(vis, jnp.exp(sm - m), 0.0)
    l = jnp.sum(e, axis=-1, keepdims=True)
    p = e * (1.0 / jnp.where(l > 0.0, l, 1.0))
    o_cmp = _dot(p.astype(MXU_DTYPE), vc)
    psums = []
    for kv in range(NSA_KV_HEADS):
        blocks = [p[(kv * NSA_GROUP + g) * tq:(kv * NSA_GROUP + g + 1) * tq] for g in range(NSA_GROUP)]
        psums.append(functools.reduce(lambda a, b: a + b, blocks))
    return o_cmp, psums


def _nsa_select(imp, qp, n_blocks, n_top, axis):
    jblk = lax.broadcasted_iota(jnp.int32, imp.shape, axis)
    cur = qp // SEL_BLOCK
    forced = (jblk == 0) | (jblk == cur) | (jblk == cur - 1)
    valid = (jblk * SEL_BLOCK <= qp) & (jblk < n_blocks)
    val = jnp.where(valid, imp + jnp.where(forced, FORCE_BONUS, 0.0), NEG)
    return jnp.where(valid, _topk_mask(val, n_top, axis), 0.0) - 1.0


def _nsa_lhs(qs, biases):
    bias = jnp.concatenate([b for b in biases for _ in range(NSA_GROUP)], axis=0)
    return jnp.concatenate([qs, bias.astype(qs.dtype)], axis=1)


def _combine_nsa(gates, o_cmp, o_sel, o_win, tq):
    lo = _lane_lo()
    heads = []
    for hh in range(NSA_HEADS):
        r = slice(hh * tq, (hh + 1) * tq)
        o = (gates[:, 3 * hh:3 * hh + 1] * o_cmp[r] + gates[:, 3 * hh + 1:3 * hh + 2] * o_sel[r]
             + gates[:, 3 * hh + 2:3 * hh + 3] * o_win[r])
        if hh % 2 != hh // NSA_GROUP:
            o = pltpu.roll(o, HEAD_DIM, 1)
        heads.append(o)
    return jnp.concatenate([jnp.where(lo, heads[2 * i], heads[2 * i + 1]) for i in range(NSA_HEADS // 2)], axis=1)


def _moba_pick(s, own, axis):
    past = lax.broadcasted_iota(jnp.int32, s.shape, axis) < own
    return jnp.where(past, _topk_mask(jnp.where(past, s, NEG), MOBA_TOPK, axis), 0.0) - 1.0


def _sb_block(qh, kt, vt, cum, carry, keep):
    c, o = carry
    tk = kt.shape[0]
    z = _dot_nt(qh, kt)
    lk = -_softplus(z)
    if keep is not None:
        lk = jnp.where(keep, lk, 0.0)
    r = _dot_split(lk, cum)
    a = jnp.exp(lk + z + r[:, :tk] + c)
    if keep is not None:
        a = jnp.where(keep, a, 0.0)
    return c + r[:, tk:], o + _dot(a.astype(MXU_DTYPE), vt)


def _stage_chunks(src_ref, lead, n_chunks, dst_ref, c0):
    for r in range(CMP_STRIDE):
        x = src_ref[lead + (pl.ds(r, n_chunks, stride=CMP_STRIDE), slice(None))]
        dst_ref[pl.ds(c0, n_chunks), r * LANES:(r + 1) * LANES] = x


def _compress_finish(xk_ref, xv_ref, wk_ref, wv_ref, pe_ref, w2_ref, ok_ref, ov_ref):
    nc = xk_ref.shape[0]
    for x_ref, w_ref, t, o_ref in ((xk_ref, wk_ref, 0, ok_ref), (xv_ref, wv_ref, 1, ov_ref)):
        ab = _dot(x_ref[...].astype(MXU_DTYPE), w_ref[...])
        cst = _dot(pe_ref[t].astype(MXU_DTYPE), w_ref[...])
        for h in range(NSA_KV_HEADS):
            a = ab[:, h * 256:h * 256 + LANES]
            b = ab[:, h * 256 + LANES:(h + 1) * 256]
            c = cst[0:1, h * 256:h * 256 + LANES] + cst[1:2, h * 256 + LANES:(h + 1) * 256]
            hid = jax.nn.gelu(a + pltpu.roll(b, nc - 1, 0) + c)
            part = _dot(hid.astype(MXU_DTYPE), w2_ref[t, h])
            out = part if h == 0 else out + part
        o_ref[0] = out.astype(o_ref.dtype)


def _compress_prompt_kernel(rk_ref, rv_ref, wk_ref, wv_ref, pe_ref, w2_ref, ok_ref, ov_ref, xk_ref, xv_ref):
    n_chunks = rk_ref.shape[1] // CMP_STRIDE
    _stage_chunks(rk_ref, (0,), n_chunks, xk_ref, 0)
    _stage_chunks(rv_ref, (0,), n_chunks, xv_ref, 0)
    _compress_finish(xk_ref, xv_ref, wk_ref, wv_ref, pe_ref, w2_ref, ok_ref, ov_ref)


def _compress_sample_kernel(pt_ref, wk_ref, wv_ref, pe_ref, w2_ref, *refs, n_pages_step):
    del pt_ref
    pages = refs[:n_pages_step]
    ok_ref, ov_ref, xk_ref, xv_ref, rows_ref = refs[n_pages_step:]
    s = pl.program_id(1)
    per_page = pages[0].shape[3] // CMP_STRIDE
    for gp, page in enumerate(pages):
        x = page[0, 0]
        rows_ref[0] = x[:LANES].T
        rows_ref[1] = x[LANES:].T
        c0 = pl.multiple_of((s * n_pages_step + gp) * per_page, per_page)
        _stage_chunks(rows_ref, (0,), per_page, xk_ref, c0)
        _stage_chunks(rows_ref, (1,), per_page, xv_ref, c0)

    @pl.when(s == pl.num_programs(1) - 1)
    def _():
        _compress_finish(xk_ref, xv_ref, wk_ref, wv_ref, pe_ref, w2_ref, ok_ref, ov_ref)


def _compress_weights(pe, w1, w2):
    eye = jnp.eye(NSA_KV_HEADS, dtype=w1.dtype)
    big, pes = [], []
    for t in range(2):
        w1r = w1[t].reshape(2, CMP_STRIDE, HEAD_DIM, w1.shape[-1])
        wb = jnp.einsum("hH,ardn->rhdHan", eye, w1r)
        big.append(wb.reshape(CMP_STRIDE * NSA_KV_HEADS * HEAD_DIM, -1).astype(MXU_DTYPE))
        per = pe[t].reshape(2, CMP_STRIDE, 1, HEAD_DIM)
        rows = jnp.broadcast_to(per, (2, CMP_STRIDE, NSA_KV_HEADS, HEAD_DIM)).reshape(2, -1)
        pes.append(jnp.concatenate([rows, jnp.zeros((SUBLANES - 2, rows.shape[1]), rows.dtype)], axis=0))
    zero = jnp.zeros_like(w2)
    w2p = jnp.stack([jnp.concatenate([w2, zero], axis=-1), jnp.concatenate([zero, w2], axis=-1)], axis=1)
    return big[0], big[1], jnp.stack(pes), w2p.astype(MXU_DTYPE)


def _compress_out(b, nc):
    shape = jax.ShapeDtypeStruct((b, nc, LANES), MXU_DTYPE)
    return [shape, shape]


def _compress_prompt(nsa_rows, cw):
    b, t, _ = nsa_rows.shape
    nc = t // CMP_STRIDE
    wk, wv, pe2, w2p = cw
    out_spec = pl.BlockSpec((1, nc, LANES), lambda i: (i, 0, 0))
    return pl.pallas_call(
        _compress_prompt_kernel,
        grid=(b,),
        in_specs=[pl.BlockSpec((1, t, LANES), lambda i: (i, 0, 0)),
                  pl.BlockSpec((1, t, LANES), lambda i: (i, 0, 1)),
                  _full(wk.shape), _full(wv.shape), _full(pe2.shape), _full(w2p.shape)],
        out_specs=[out_spec, out_spec],
        out_shape=_compress_out(b, nc),
        scratch_shapes=[pltpu.VMEM((nc, CMP_STRIDE * LANES), _F32)] * 2,
        compiler_params=_params("arbitrary"),
        name="nsa_compress_prompt",
    )(nsa_rows, nsa_rows, wk, wv, pe2, w2p)


def _nsa_prompt_kernel(q_ref, g_ref, ck_ref, cv_ref, sk_ref, sv_ref, wk_ref, wv_ref, eblk_ref, cover_ref,
                       o_ref, *, tq, tk, n_top):
    qi = pl.program_id(1)
    t0 = qi * tq
    rows = NSA_HEADS * tq
    n_blocks = sk_ref.shape[1] // SEL_BLOCK
    qs = _stack_nsa_queries(q_ref[0])
    qpos_r = t0 + lax.broadcasted_iota(jnp.int32, (rows, 1), 0) % tq

    o_cmp, psums = _cmp_branch(qs, ck_ref[0], cv_ref[0], qpos_r, tq)
    qp_t = t0 + lax.broadcasted_iota(jnp.int32, (1, tq), 1)
    biases = []
    for psum in psums:
        imp_t = _dot_split(psum, cover_ref[...]).T
        biases.append(_nsa_select(imp_t, qp_t, n_blocks, n_top, 0).T)
    lhs = _nsa_lhs(qs, biases)

    kcol = lax.broadcasted_iota(jnp.int32, (1, tk), 1)

    chunk = rows // NSA_ROW_CHUNKS
    lhs_c = [lhs[c * chunk:(c + 1) * chunk] for c in range(NSA_ROW_CHUNKS)]
    qpos_c = [qpos_r[c * chunk:(c + 1) * chunk] for c in range(NSA_ROW_CHUNKS)]

    def sel_tile(kt, states, causal):
        start = pl.multiple_of(kt * tk, tk)
        rhs = jnp.concatenate([sk_ref[0, pl.ds(start, tk), :], eblk_ref[pl.ds(start, tk), :]], axis=1)
        v1 = _with_ones(sv_ref[0, pl.ds(start, tk), :])
        out = []
        for c in range(NSA_ROW_CHUNKS):
            z = _dot_nt(lhs_c[c], rhs)
            if causal:
                z = jnp.where(start + kcol <= qpos_c[c], z, NEG)
            out.append(_online_update1(states[c], z, v1))
        return tuple(out)

    n_full = t0 // tk
    init = tuple(_online_init1(chunk) for _ in range(NSA_ROW_CHUNKS))
    states = lax.fori_loop(0, n_full, lambda kt, st: sel_tile(kt, st, False), init)
    o_sel = jnp.concatenate([_finish1(st) for st in sel_tile(n_full, states, True)], axis=0)

    n_band = WINDOW + tq
    ws = pl.multiple_of(jnp.maximum(t0 - WINDOW, 0), tq)
    wk, wv1 = wk_ref[0, pl.ds(ws, n_band), :], _with_ones(wv_ref[0, pl.ds(ws, n_band), :])
    kpos = ws + lax.broadcasted_iota(jnp.int32, (1, n_band), 1)
    o_win = []
    for c in range(NSA_ROW_CHUNKS):
        dist = qpos_c[c] - kpos
        z = jnp.where((dist >= 0) & (dist <= WINDOW), _dot_nt(lhs_c[c][:, :LANES], wk), NEG)
        o_win.append(_finish1(_online_update1(_online_init1(chunk), z, wv1)))
    o_win = jnp.concatenate(o_win, axis=0)

    o_ref[0] = _combine_nsa(g_ref[0], o_cmp, o_sel, o_win, tq).astype(o_ref.dtype)


def _nsa_prompt(q_nsa, gates, cmp_k, cmp_v, nsa_bf, win_bf, eblk, cover, *, tq, tk):
    b, t, _ = q_nsa.shape
    nc = cmp_k.shape[1]
    n_top = min(SEL_TOPN, t // SEL_BLOCK)
    cmp_spec = pl.BlockSpec((1, nc, LANES), lambda i, qi: (i, 0, 0))
    col = lambda g: pl.BlockSpec((1, t, LANES), lambda i, qi: (i, 0, g))
    return pl.pallas_call(
        functools.partial(_nsa_prompt_kernel, tq=tq, tk=tk, n_top=n_top),
        grid=(b, t // tq),
        in_specs=[pl.BlockSpec((1, tq, NSA_HEADS * LANES), lambda i, qi: (i, qi, 0)),
                  pl.BlockSpec((1, tq, LANES), lambda i, qi: (i, qi, 0)),
                  cmp_spec, cmp_spec, col(2), col(3), col(0), col(1),
                  _full(eblk.shape), _full(cover.shape)],
        out_specs=pl.BlockSpec((1, tq, NSA_Q_COLS), lambda i, qi: (i, qi, 0)),
        out_shape=jax.ShapeDtypeStruct((b, t, NSA_Q_COLS), MXU_DTYPE),
        compiler_params=_params("parallel", "arbitrary"),
        name="nsa_prompt",
    )(q_nsa, gates, cmp_k, cmp_v, nsa_bf, nsa_bf, win_bf, win_bf, eblk, cover)


def _sb_prompt_kernel(q_ref, k_ref, v_ref, o_ref, *, tq, tk):
    qi = pl.program_id(1)
    lo = _lane_lo()
    n_pair = q_ref.shape[2] // LANES
    qh = []
    for p in range(n_pair):
        q = q_ref[0, :, p * LANES:(p + 1) * LANES]
        zero_q = jnp.zeros_like(q)
        qh += [jnp.where(lo, q, zero_q), jnp.where(lo, zero_q, q)]
    cum = _cumsum_matrix(tk)
    row = lax.broadcasted_iota(jnp.int32, (tq, tk), 0)
    col = lax.broadcasted_iota(jnp.int32, (tq, tk), 1)
    per_tile = tq // tk

    def blocks(kb, carry, keep):
        start = pl.multiple_of(kb * tk, tk)
        out = []
        for p in range(n_pair):
            kt = k_ref[0, pl.ds(start, tk), p * LANES:(p + 1) * LANES]
            vt = v_ref[0, pl.ds(start, tk), p * LANES:(p + 1) * LANES]
            out += [_sb_block(qh[2 * p + h], kt, vt, cum, carry[2 * p + h], keep) for h in range(2)]
        return tuple(out)

    def alive(carry):
        c_max = functools.reduce(jnp.maximum, [c for c, _ in carry])
        return (jnp.max(c_max) > SB_DEAD).astype(jnp.int32)

    zero = jnp.zeros((tq, LANES), _F32)
    carry = ((zero, zero),) * (2 * n_pair)
    for d in reversed(range(per_tile)):
        carry = blocks(qi * per_tile + d, carry, d * tk + col < row)

    def body(st):
        i, _, carry = st
        for d in range(per_tile):
            carry = blocks((qi - 1 - i) * per_tile + per_tile - 1 - d, carry, None)
        return i + 1, alive(carry), carry

    _, _, carry = lax.while_loop(lambda st: (st[0] < qi) & (st[1] > 0), body, (0, alive(carry), carry))
    o_ref[0] = jnp.concatenate([jnp.where(lo, carry[2 * p][1], carry[2 * p + 1][1]) for p in range(n_pair)],
                               axis=1).astype(o_ref.dtype)


def _sb_prompt(q_sb, sb_bf, *, tq, tk):
    b, t, width = q_sb.shape
    return pl.pallas_call(
        functools.partial(_sb_prompt_kernel, tq=tq, tk=tk),
        grid=(b, t // tq),
        in_specs=[pl.BlockSpec((1, tq, width), lambda i, qi: (i, qi, 0)),
                  pl.BlockSpec((1, t, width), lambda i, qi: (i, 0, 0)),
                  pl.BlockSpec((1, t, width), lambda i, qi: (i, 0, 1))],
        out_specs=pl.BlockSpec((1, tq, width), lambda i, qi: (i, qi, 0)),
        out_shape=jax.ShapeDtypeStruct((b, t, width), MXU_DTYPE),
        compiler_params=_params("parallel", "arbitrary"),
        name="sb_prompt",
    )(q_sb, sb_bf, sb_bf)


def _moba_prompt_kernel(q_ref, km_ref, k_ref, v_ref, eblk_ref, o_ref, *, tq, tk):
    qi = pl.program_id(2)
    lo = _lane_lo()
    q = q_ref[0]
    scale = HEAD_DIM ** -0.5
    row = lax.broadcasted_iota(jnp.int32, (tq, tq), 0)
    col = lax.broadcasted_iota(jnp.int32, (tq, tq), 1)
    own0 = pl.multiple_of(qi * tq, tq)
    k_own, v_own = k_ref[0, pl.ds(own0, tq), :], _with_ones(v_ref[0, pl.ds(own0, tq), :])
    lhs, states = [], []
    for hh in range(2):
        qf = jnp.where(lo if hh == 0 else jnp.logical_not(lo), q, 0.0)
        gate = lax.dot_general(qf, km_ref[0], (((1,), (1,)), ((), ())), preferred_element_type=_F32,
                               precision=lax.Precision.HIGHEST)
        bias = _moba_pick(gate.T, qi, 0).T
        qh = (qf * scale).astype(MXU_DTYPE)
        lhs.append(jnp.concatenate([qh, bias.astype(MXU_DTYPE)], axis=1))
        z = jnp.where(col <= row, _dot_nt(qh, k_own), NEG)
        states.append(_online_update1(_online_init1(tq), z, v_own))

    def tile(j, states):
        start = pl.multiple_of(j * tk, tk)
        rhs = jnp.concatenate([k_ref[0, pl.ds(start, tk), :], eblk_ref[pl.ds(start, tk), :]], axis=1)
        v1 = _with_ones(v_ref[0, pl.ds(start, tk), :])
        return tuple(_online_update1(st, _dot_nt(l, rhs), v1) for l, st in zip(lhs, states))

    states = lax.fori_loop(0, (qi * tq + tk - 1) // tk, tile, tuple(states))
    o_ref[0] = jnp.where(lo, _finish1(states[0]), _finish1(states[1])).astype(o_ref.dtype)


def _moba_prompt(q_mb, kmean, mb_bf, eblk, *, tk):
    b, t, _ = q_mb.shape
    tq = MOBA_BLOCK
    n_pair = MOBA_HEADS // 2
    return pl.pallas_call(
        functools.partial(_moba_prompt_kernel, tq=tq, tk=tk),
        grid=(b, n_pair, t // tq),
        in_specs=[pl.BlockSpec((1, tq, LANES), lambda i, p, qi: (i, qi, p)),
                  pl.BlockSpec((1, LANES, LANES), lambda i, p, qi: (i, 0, p)),
                  pl.BlockSpec((1, t, LANES), lambda i, p, qi: (i, 0, p)),
                  pl.BlockSpec((1, t, LANES), lambda i, p, qi: (i, 0, n_pair + p)),
                  _full(eblk.shape)],
        out_specs=pl.BlockSpec((1, tq, LANES), lambda i, p, qi: (i, qi, p)),
        out_shape=jax.ShapeDtypeStruct((b, t, MOBA_HEADS * HEAD_DIM), MXU_DTYPE),
        compiler_params=_params("parallel", "parallel", "arbitrary"),
        name="moba_prompt",
    )(q_mb, kmean, mb_bf, mb_bf, eblk)


def _row_scores(q, keys, n):
    return [jnp.sum(q * keys[j:j + 1, :], axis=-1, keepdims=True) for j in range(n)]


def _new_key_softmax(q, keys, vals, n, visible):
    zs = [jnp.where(visible(j), z, NEG) for j, z in enumerate(_row_scores(q, keys, n))]
    m = functools.reduce(jnp.maximum, zs)
    l = jnp.zeros_like(m)
    acc = jnp.zeros((q.shape[0], vals.shape[1]), _F32)
    for j, z in enumerate(zs):
        p = jnp.exp(z - m)
        l = l + p
        acc = acc + p * vals[j:j + 1, :]
    return m, l, acc


def _stack_heads(q, n_heads):
    head = lax.broadcasted_iota(jnp.int32, (1, q.shape[1]), 1) // HEAD_DIM
    return jnp.concatenate([jnp.where(head == h, q, jnp.zeros_like(q)) for h in range(n_heads)], axis=0)


def _unstack_heads(o, n_heads):
    rows = o.shape[0] // n_heads
    head = lax.broadcasted_iota(jnp.int32, (1, o.shape[1]), 1) // HEAD_DIM
    out = jnp.zeros((rows, o.shape[1]), _F32)
    for h in range(n_heads):
        out = jnp.where(head == h, o[h * rows:(h + 1) * rows], out)
    return out


def _side_by_side(pages, f0, f1):
    return jnp.concatenate([p[0, 0, f0:f1, :] for p in pages], axis=1).astype(MXU_DTYPE)


def _key_block_onehot(n_keys, first_key, block):
    key_blk = (first_key + lax.broadcasted_iota(jnp.int32, (LANES, n_keys), 1)) // block
    return jnp.where(lax.broadcasted_iota(jnp.int32, (LANES, n_keys), 0) == key_blk, -NEG, 0.0).astype(MXU_DTYPE)


def _state_refs_store(m_ref, acc_ref, state):
    m_ref[...] = jnp.broadcast_to(state[0], m_ref.shape)
    acc_ref[...] = state[1]


def _page_specs(layer, n, block, row_block, page_of):
    def spec(gp):
        return pl.BlockSpec(block, lambda b, s, pt: (layer, pt[b, page_of(s, gp)], row_block, 0))
    return [spec(gp) for gp in range(n)]


def _sample_call(kernel_fn, name, page_table, n_steps, pre_args, pre_specs, page_arrays, page_specs, out_block,
                 out_shape, scratch):
    grid_spec = pltpu.PrefetchScalarGridSpec(
        num_scalar_prefetch=1,
        grid=(page_table.shape[0], n_steps),
        in_specs=pre_specs + page_specs,
        out_specs=out_block,
        scratch_shapes=scratch,
    )
    return pl.pallas_call(kernel_fn, grid_spec=grid_spec, out_shape=out_shape,
                          compiler_params=_params("arbitrary", "arbitrary"), name=name,
                          )(page_table, *pre_args, *page_arrays)


def _seq_block(shape):
    zeros = (0,) * (len(shape) - 1)
    return pl.BlockSpec((1,) + tuple(shape[1:]), lambda b, s, pt: (b,) + zeros)


def _const_block(shape):
    zeros = (0,) * len(shape)
    return pl.BlockSpec(tuple(shape), lambda b, s, pt: zeros)


def _compress_sample(cache, layer, page_table, cw):
    b, n_pages = page_table.shape
    page = cache.shape[3]
    nc = n_pages * page // CMP_STRIDE
    g = PAGES_PER_STEP
    wk, wv, pe2, w2p = cw
    out_spec = pl.BlockSpec((1, nc, LANES), lambda i, s, pt: (i, 0, 0))
    return _sample_call(
        functools.partial(_compress_sample_kernel, n_pages_step=g), "nsa_compress_sample", page_table, n_pages // g,
        [wk, wv, pe2, w2p], [_const_block(wk.shape), _const_block(wv.shape), _const_block(pe2.shape),
                             _const_block(w2p.shape)],
        [cache] * g, _page_specs(layer, g, (1, 1, 2 * LANES, page), 0, lambda s, gp: s * g + gp),
        [out_spec, out_spec], _compress_out(b, nc),
        [pltpu.VMEM((nc, CMP_STRIDE * LANES), _F32)] * 2 + [pltpu.VMEM((2, page, LANES), _F32)])


def _nsa_sample_kernel(pt_ref, q_ref, g_ref, ck_ref, cv_ref, new_ref, neww_ref, win_ref, cover_ref, *refs,
                       n_pages_step, n_new, past_len, n_top):
    del pt_ref
    pages = refs[:n_pages_step]
    o_ref, m_ref, acc_ref, bias_ref, ocmp_ref, owin_ref = refs[n_pages_step:]
    s = pl.program_id(1)
    tq = SAMPLE_ROWS
    rows = NSA_HEADS * tq
    qs = _stack_nsa_queries(q_ref[0])
    qi_r = lax.broadcasted_iota(jnp.int32, (rows, 1), 0) % tq
    qpos_r = past_len + qi_r

    @pl.when(s == 0)
    def _():
        qf = qs.astype(_F32)
        o_cmp, psums = _cmp_branch(qs, ck_ref[0], cv_ref[0], qpos_r, tq)
        ocmp_ref[...] = o_cmp
        qp = past_len + lax.broadcasted_iota(jnp.int32, (tq, 1), 0)
        biases = [_nsa_select(_dot_split(psum, cover_ref[...]), qp, past_len // SEL_BLOCK, n_top - 1, 1)
                  for psum in psums]
        bias_ref[...] = jnp.concatenate([b for b in biases for _ in range(NSA_GROUP)], axis=0)
        visible = lambda j: j <= qi_r
        new = new_ref[0].astype(_F32)
        state = _seed_state(*_new_key_softmax(qf, new[:, 2 * LANES:3 * LANES], new[:, 3 * LANES:], n_new, visible))
        _state_refs_store(m_ref, acc_ref, state)
        wnew = neww_ref[0].astype(_F32)
        wstate = _seed_state(*_new_key_softmax(qf, wnew[:, :LANES], wnew[:, LANES:], n_new, visible))
        w = win_ref[0, 0]
        n_win = w.shape[1]
        dist = qpos_r - (past_len - n_win + lax.broadcasted_iota(jnp.int32, (1, n_win), 1))
        z = jnp.where((dist >= 0) & (dist <= WINDOW), _dot(qs, w[:LANES].astype(MXU_DTYPE)), NEG)
        owin_ref[...] = _finish1(_online_update1(wstate, z, _with_ones(w[LANES:].astype(MXU_DTYPE), 0), nt=True))

    n_keys = n_pages_step * pages[0].shape[3]
    lhs = jnp.concatenate([qs, bias_ref[...].astype(MXU_DTYPE)], axis=1)
    rhs = jnp.concatenate([_side_by_side(pages, 0, LANES), _key_block_onehot(n_keys, s * n_keys, SEL_BLOCK)], axis=0)
    state = (m_ref[:, 0:1], acc_ref[...])
    state = _online_update1(state, _dot(lhs, rhs), _with_ones(_side_by_side(pages, LANES, 2 * LANES), 0), nt=True)
    _state_refs_store(m_ref, acc_ref, state)

    @pl.when(s == pl.num_programs(1) - 1)
    def _():
        o_sel = _finish1((m_ref[:, 0:1], acc_ref[...]))
        o_ref[0] = _combine_nsa(g_ref[0], ocmp_ref[...], o_sel, owin_ref[...], tq).astype(o_ref.dtype)


def _nsa_sample(q, gates, cmp_k, cmp_v, new_bf, new_win_bf, cache, cache_win, layer, page_table, cover, n_new):
    b, n_pages = page_table.shape
    page = cache.shape[3]
    past_len = n_pages * page
    g = PAGES_PER_STEP
    rows = NSA_HEADS * SAMPLE_ROWS
    n_top = min(SEL_TOPN, past_len // SEL_BLOCK + 1)
    win_spec = pl.BlockSpec((1, 1) + cache_win.shape[2:], lambda i, s, pt: (layer, i, 0, 0))
    out_shape = (b, SAMPLE_ROWS, NSA_Q_COLS)
    vmem = lambda w: pltpu.VMEM((rows, w), _F32)
    return _sample_call(
        functools.partial(_nsa_sample_kernel, n_pages_step=g, n_new=n_new, past_len=past_len, n_top=n_top),
        "nsa_sample", page_table, n_pages // g,
        [q, gates, cmp_k, cmp_v, new_bf, new_win_bf, cache_win, cover],
        [_seq_block(q.shape), _seq_block(gates.shape), _seq_block(cmp_k.shape), _seq_block(cmp_v.shape),
         _seq_block(new_bf.shape), _seq_block(new_win_bf.shape), win_spec, _const_block(cover.shape)],
        [cache] * g, _page_specs(layer, g, (1, 1, 2 * LANES, page), 1, lambda s, gp: s * g + gp),
        _seq_block(out_shape), jax.ShapeDtypeStruct(out_shape, MXU_DTYPE),
        [vmem(LANES), vmem(2 * LANES), vmem(LANES), vmem(LANES), vmem(LANES)])


def _sb_sample_kernel(pt_ref, q_ref, new_ref, *refs, n_pages_step, n_new):
    del pt_ref
    pages = refs[:n_pages_step]
    o_ref, c_ref, acc_ref = refs[n_pages_step:]
    s = pl.program_id(1)
    width = SB_HEADS * HEAD_DIM
    qs = _stack_heads(q_ref[0], SB_HEADS)
    rows = qs.shape[0]
    qi_r = lax.broadcasted_iota(jnp.int32, (rows, 1), 0) % SAMPLE_ROWS

    @pl.when(s == 0)
    def _():
        qf = qs.astype(_F32)
        new = new_ref[0].astype(_F32)
        zs = _row_scores(qf, new[:, :width], n_new)
        c = jnp.zeros((rows, 1), _F32)
        o = jnp.zeros((rows, width), _F32)
        for j in reversed(range(n_new)):
            seen = j < qi_r
            lk = jnp.where(seen, -_softplus(zs[j]), 0.0)
            a = jnp.where(seen, jnp.exp(lk + zs[j] + c), 0.0)
            o = o + a * new[j:j + 1, width:]
            c = c + lk
        c_ref[...] = jnp.broadcast_to(c, c_ref.shape)
        acc_ref[...] = o

    page_keys = pages[0].shape[3]
    cum = _cumsum_matrix(page_keys)
    z = _dot(qs, _side_by_side(pages, 0, width))
    lk = -_softplus(z)
    c = c_ref[...]
    later = []
    for gp in range(n_pages_step):
        r = _dot_split(lk[:, gp * page_keys:(gp + 1) * page_keys], cum)
        later.append(r[:, :page_keys] + c)
        c = c + r[:, page_keys:]
    a = jnp.exp(lk + z + jnp.concatenate(later, axis=1))
    acc_ref[...] += _dot_nt(a.astype(MXU_DTYPE), _side_by_side(pages, width, 2 * width))
    c_ref[...] = c

    @pl.when(s == pl.num_programs(1) - 1)
    def _():
        o_ref[0] = _unstack_heads(acc_ref[...], SB_HEADS).astype(o_ref.dtype)


def _sb_sample(q, new_bf, cache, layer, page_table, n_new):
    b, n_pages = page_table.shape
    page = cache.shape[3]
    g = PAGES_PER_STEP
    rows = SB_HEADS * SAMPLE_ROWS
    width = SB_HEADS * HEAD_DIM
    return _sample_call(
        functools.partial(_sb_sample_kernel, n_pages_step=g, n_new=n_new),
        "sb_sample", page_table, n_pages // g,
        [q, new_bf], [_seq_block(q.shape), _seq_block(new_bf.shape)],
        [cache] * g, _page_specs(layer, g, (1, 1, 2 * width, page), 0, lambda s, gp: n_pages - 1 - (s * g + gp)),
        _seq_block(q.shape), jax.ShapeDtypeStruct(q.shape, MXU_DTYPE),
        [pltpu.VMEM((rows, LANES), _F32), pltpu.VMEM((rows, width), _F32)])


def _moba_sample_kernel(pt_ref, q_ref, new_ref, *refs, n_pages_step, n_new, past_len):
    del pt_ref
    pages = refs[:n_pages_step]
    o_ref, m_ref, part_ref, km_ref = refs[n_pages_step:]
    s = pl.program_id(1)
    width = MOBA_HEADS * HEAD_DIM
    page_keys = pages[0].shape[3]
    per_blk = MOBA_BLOCK // page_keys
    blk_step = n_pages_step // per_blk
    n_blk = past_len // MOBA_BLOCK
    qf = _stack_heads(q_ref[0], MOBA_HEADS)
    rows = qf.shape[0]
    qs = (qf * HEAD_DIM ** -0.5).astype(MXU_DTYPE)
    qi_r = lax.broadcasted_iota(jnp.int32, (rows, 1), 0) % SAMPLE_ROWS
    lane = lax.broadcasted_iota(jnp.int32, (1, LANES), 1)

    @pl.when(s == 0)
    def _():
        km_ref[...] = jnp.zeros_like(km_ref)

    km = km_ref[...]
    for jb in range(blk_step):
        blk = s * blk_step + jb
        mine = pages[jb * per_blk:(jb + 1) * per_blk]
        mean = sum(jnp.sum(p[0, 0, :width, :], axis=1, keepdims=True) for p in mine) * (1.0 / MOBA_BLOCK)
        km = km + jnp.where(lane == blk, mean, 0.0)
        z = _dot(qs, _side_by_side(mine, 0, width))
        m = jnp.max(z, axis=-1, keepdims=True)
        p = jnp.exp(z - m).astype(MXU_DTYPE)
        part_ref[blk] = _dot_nt(p, _with_ones(_side_by_side(mine, width, 2 * width), 0))
        m_ref[blk] = jnp.broadcast_to(m, (rows, LANES))
    km_ref[...] = km

    @pl.when(s == pl.num_programs(1) - 1)
    def _():
        gate = lax.dot_general(qf, km_ref[...], (((1,), (0,)), ((), ())), preferred_element_type=_F32,
                               precision=lax.Precision.HIGHEST)
        sel = _moba_pick(gate, n_blk, 1) + 1.0
        new = new_ref[0].astype(_F32)
        m0, l0, acc0 = _new_key_softmax(qs.astype(_F32), new[:, :width], new[:, width:], n_new, lambda j: j <= qi_r)
        m_all = jnp.full((rows, LANES), NEG, _F32)
        for j in range(n_blk):
            m_all = jnp.where(lane == j, m_ref[j], m_all)
        m_top = jnp.maximum(m0, jnp.max(jnp.where(sel > 0.0, m_all, NEG), axis=-1, keepdims=True))
        w = sel * jnp.exp(m_all - m_top)
        out = jnp.exp(m0 - m_top) * _seed_state(m0, l0, acc0)[1]
        for j in range(n_blk):
            out = out + w[:, j:j + 1] * part_ref[j]
        o_ref[0] = _unstack_heads(_finish1((m_top, out)), MOBA_HEADS).astype(o_ref.dtype)


def _moba_sample(q, new_bf, cache, layer, page_table, n_new):
    b, n_pages = page_table.shape
    page = cache.shape[3]
    g = PAGES_PER_STEP
    rows = MOBA_HEADS * SAMPLE_ROWS
    width = MOBA_HEADS * HEAD_DIM
    n_blk = n_pages * page // MOBA_BLOCK
    out_shape = (b, SAMPLE_ROWS, width)
    return _sample_call(
        functools.partial(_moba_sample_kernel, n_pages_step=g, n_new=n_new, past_len=n_pages * page),
        "moba_sample", page_table, n_pages // g,
        [q, new_bf], [_seq_block(q.shape), _seq_block(new_bf.shape)],
        [cache] * g, _page_specs(layer, g, (1, 1, 2 * width, page), 0, lambda s, gp: s * g + gp),
        _seq_block(out_shape), jax.ShapeDtypeStruct(out_shape, MXU_DTYPE),
        [pltpu.VMEM((n_blk, rows, LANES), _F32), pltpu.VMEM((n_blk, rows, width + LANES), _F32),
         pltpu.VMEM((width, LANES), _F32)])


def _rope_tables(pos):
    half = HEAD_DIM // 2
    inv = ROPE_THETA ** (-jnp.arange(half, dtype=_F32) / half)
    ang = pos.astype(_F32)[:, None] * inv[None, :]
    cos, sin = jnp.cos(ang), jnp.sin(ang)
    zero = jnp.zeros_like(sin)
    reps = LANES // HEAD_DIM
    return (jnp.tile(cos, (1, 2 * reps)), jnp.tile(jnp.concatenate([-sin, zero], axis=1), (1, reps)),
            jnp.tile(jnp.concatenate([zero, sin], axis=1), (1, reps)))


def _pack_w_in(w_in):
    d = w_in.shape[0]
    o1 = NSA_Q_COLS
    o2 = o1 + NSA_KV_COLS
    o3 = o2 + NSA_G_COLS
    o4 = o3 + SB_COLS
    q, kv, g, sb, mb = w_in[:, :o1], w_in[:, o1:o2], w_in[:, o2:o3], w_in[:, o3:o4], w_in[:, o4:]
    zero = jnp.zeros((d, HEAD_DIM), w_in.dtype)
    q_groups = []
    for hh in range(NSA_HEADS):
        w = q[:, hh * HEAD_DIM:(hh + 1) * HEAD_DIM]
        q_groups += [w, zero] if hh // NSA_GROUP == 0 else [zero, w]
    gate = jnp.pad(g, ((0, 0), (0, LANES - NSA_G_COLS)))
    return jnp.concatenate(q_groups + [kv, sb, mb, gate], axis=1).astype(MXU_DTYPE)


def _block_tables(t):
    key = jnp.arange(t)[:, None]
    j = jnp.arange(LANES)[None, :]
    eblk = jnp.where(key // SEL_BLOCK == j, -NEG, 0.0).astype(MXU_DTYPE)
    eblk_mb = jnp.where(key // MOBA_BLOCK == j, -NEG, 0.0).astype(MXU_DTYPE)
    i = jnp.arange(t // CMP_STRIDE)[:, None]
    ratio = SEL_BLOCK // CMP_STRIDE
    cover = ((i <= ratio * j + ratio - 1) & (i >= ratio * j - (CMP_LEN // CMP_STRIDE - 1))).astype(MXU_DTYPE)
    return eblk, eblk_mb, cover


def _layer_weights(l, norm_g, ffn_w_gate, ffn_w_up, ffn_w_down, w_in, nsa_cmp_pe, nsa_cmp_w1, nsa_cmp_w2,
                   w_branch_nsa, w_branch_sb, w_branch_moba, w_merge_gate, w_out, w_ple_proj, w_ple_gate):
    c = lambda w: w.astype(MXU_DTYPE)
    d, f = ffn_w_gate.shape[2:]
    tf = _ff_tile(f)
    cols = lambda w: c(w).reshape(d, f // tf, tf).transpose(1, 0, 2)
    return dict(
        g=[norm_g[l, i][None, :] for i in range(norm_g.shape[1])],
        ffn=[(cols(ffn_w_gate[l, i]), cols(ffn_w_up[l, i]), c(ffn_w_down[l, i]).reshape(f // tf, tf, d))
             for i in range(2)],
        w_all=_pack_w_in(w_in[l]),
        cw=_compress_weights(nsa_cmp_pe[l], nsa_cmp_w1[l], nsa_cmp_w2[l]),
        merge=(c(w_branch_nsa[l]), c(w_branch_sb[l]), c(w_branch_moba[l]), c(w_merge_gate[l]), c(w_out[l])),
        ple=(c(w_ple_gate[l]), c(w_ple_proj[l])),
    )


def _token_tile(n, cap):
    tm = cap
    while n % tm:
        tm //= 2
    return tm


def _ff_tile(f, cap=512):
    best = LANES
    for k in range(1, f // LANES + 1):
        if f % (k * LANES) == 0 and k * LANES <= cap:
            best = k * LANES
    return best


def _prompt_layer(h, p_l, lw, tabs, consts, b, t, depth, layer, carried):
    n, d = h.shape
    tm = _token_tile(t, 512)
    tm_ffn = _token_tile(t, 1024)
    tk = 8 * LANES
    g = lw["g"]
    h = _ffn(h, g[0], g[1], *lw["ffn"][0], tm=tm_ffn)
    per_seq = t // tm
    pr = _inproj(h, g[2], lw["w_all"], tabs, lambda i: (i % per_seq, 0), tm=tm, stack=(depth, layer, b, t),
                 carried=carried)
    stacked = tuple(pr.pop(k) for k in _INPROJ_STATE)
    kmean = pr.pop("kmean").reshape(b, per_seq, SUBLANES, -1)[:, :, :tm // MOBA_BLOCK].reshape(b, t // MOBA_BLOCK, -1)
    kmean = jnp.pad(kmean, ((0, 0), (0, LANES - kmean.shape[1]), (0, 0)))
    pr = {k: v.reshape(b, t, v.shape[1]) for k, v in pr.items()}
    eblk, eblk_mb, cover = consts
    cmp_k, cmp_v = _compress_prompt(pr["cmp_rows"], lw["cw"])
    o_nsa = _nsa_prompt(pr["q_nsa"], pr["gates"], cmp_k, cmp_v, pr["nsa_bf"], pr["win_bf"], eblk, cover,
                        tq=2 * LANES, tk=tk)
    o_sb = _sb_prompt(pr["q_sb"], pr["sb_bf"], tq=2 * LANES, tk=LANES)
    o_mb = _moba_prompt(pr["q_mb"], kmean, pr["mb_bf"], eblk_mb, tk=tk)
    flat = lambda o: o.reshape(n, o.shape[2])
    h = _merge(h, g[2], g[3], flat(o_nsa), flat(o_sb), flat(o_mb), *lw["merge"], tm=tm)
    h = _ffn(h, g[4], g[5], *lw["ffn"][1], ple=(p_l, *lw["ple"], g[6]), tm=tm_ffn)
    return h, stacked, pr["win"]


def _sample_layer(h, p_l, lw, tabs, cover, caches, layer, page_table, n_new):
    n, d = h.shape
    b = page_table.shape[0]
    tm = _token_tile(n, 512)
    g = lw["g"]
    cache_nsa, cache_sb, cache_mb, cache_win = caches
    h = _ffn(h, g[0], g[1], *lw["ffn"][0], tm=tm)
    pr = _inproj(h, g[2], lw["w_all"], tabs, lambda i: (i, 0), tm=tm)
    pr = {k: v.reshape(b, SAMPLE_ROWS, v.shape[1]) for k, v in pr.items()}
    cmp_k, cmp_v = _compress_sample(cache_nsa, layer, page_table, lw["cw"])
    o_nsa = _nsa_sample(pr["q_nsa"], pr["gates"], cmp_k, cmp_v, pr["nsa_bf"], pr["win_bf"], cache_nsa, cache_win,
                        layer, page_table, cover, n_new)
    o_sb = _sb_sample(pr["q_sb"], pr["sb_bf"], cache_sb, layer, page_table, n_new)
    o_mb = _moba_sample(pr["q_mb"], pr["mb_bf"], cache_mb, layer, page_table, n_new)
    flat = lambda o: o.reshape(n, o.shape[2])
    h = _merge(h, g[2], g[3], flat(o_nsa), flat(o_sb), flat(o_mb), *lw["merge"], tm=tm)
    h = _ffn(h, g[4], g[5], *lw["ffn"][1], ple=(p_l, *lw["ple"], g[6]), tm=tm)
    return h, tuple(pr[k][:, :n_new] for k in ("nsa", "sb", "mb", "win"))


def kernel(x_prompt, x_sample, p_prompt, p_sample, cache_nsa, cache_sb, cache_moba, cache_win, page_table,
           norm_g, ffn_w_gate, ffn_w_up, ffn_w_down, w_in, nsa_cmp_pe, nsa_cmp_w1, nsa_cmp_w2,
           w_branch_nsa, w_branch_sb, w_branch_moba, w_merge_gate, w_out, w_ple_proj, w_ple_gate):
    depth = norm_g.shape[0]
    b, t, d = x_prompt.shape
    bs, ts, _ = x_sample.shape
    n_pages = page_table.shape[1]
    page = cache_nsa.shape[2]
    past_len = n_pages * page
    assert ts <= SAMPLE_ROWS and t % (8 * LANES) == 0 and t >= WINDOW + LANES
    assert t // SEL_BLOCK <= LANES and past_len // SEL_BLOCK <= LANES
    assert n_pages % PAGES_PER_STEP == 0 and MOBA_BLOCK % page == 0 and cache_win.shape[2] == WINDOW
    weights = (norm_g, ffn_w_gate, ffn_w_up, ffn_w_down, w_in, nsa_cmp_pe, nsa_cmp_w1, nsa_cmp_w2,
               w_branch_nsa, w_branch_sb, w_branch_moba, w_merge_gate, w_out, w_ple_proj, w_ple_gate)
    tabs_p = _rope_tables(jnp.arange(t, dtype=jnp.int32))
    tabs_s = _rope_tables(past_len + jnp.arange(bs * SAMPLE_ROWS, dtype=jnp.int32) % SAMPLE_ROWS)
    consts = _block_tables(t)
    cover_s = _block_tables(past_len)[2]
    pad_rows = lambda x: jnp.pad(x, ((0, 0), (0, SAMPLE_ROWS - ts), (0, 0))).reshape(bs * SAMPLE_ROWS, -1)
    feature_major = lambda c: jnp.transpose(c, (0, 1, 3, 4, 5, 2)).reshape(c.shape[0], c.shape[1], -1, c.shape[2])
    caches = tuple(feature_major(c) for c in (cache_nsa, cache_sb, cache_moba, cache_win))
    h_p = x_prompt.reshape(b * t, d)
    h_s = pad_rows(x_sample)
    stacked_p, win_p, st_s = (), [], []
    keep = min(WINDOW, t)
    for l in range(depth):
        lw = _layer_weights(l, *weights)
        h_p, stacked_p, win = _prompt_layer(h_p, p_prompt[l].reshape(b * t, -1), lw, tabs_p, consts, b, t, depth, l,
                                            stacked_p)
        win_p.append(win[:, t - keep:])
        h_s, rows = _sample_layer(h_s, pad_rows(p_sample[l]), lw, tabs_s, cover_s, caches, l, page_table, ts)
        win_fm = jnp.concatenate([caches[3][l][:, :, ts:], jnp.swapaxes(rows[3], 1, 2)], axis=2)
        st_s.append(rows[:3] + (jnp.swapaxes(win_fm, 1, 2),))
    heads = ((4, NSA_KV_HEADS), (2, SB_HEADS), (2, MOBA_HEADS))
    out = [h_p.reshape(b, t, d), h_s.reshape(bs, SAMPLE_ROWS, d)[:, :ts]]
    for i, (parts, nh) in enumerate(heads):
        out.append(jnp.swapaxes(stacked_p[i], 2, 3).reshape(depth, b, t, parts, nh, HEAD_DIM))
        out.append(jnp.stack([s[i] for s in st_s]).reshape(depth, bs, ts, parts, nh, HEAD_DIM))
    out.append(jnp.stack(win_p).reshape(depth, b, keep, 2, NSA_KV_HEADS, HEAD_DIM))
    out.append(jnp.stack([s[3] for s in st_s]).reshape(depth, bs, -1, 2, NSA_KV_HEADS, HEAD_DIM))
    return tuple(out)
```

```python
import functools

import jax
import jax.numpy as jnp
from jax import lax
from jax.experimental import pallas as pl
from jax.experimental.pallas import tpu as pltpu

HEAD_DIM = 64
NSA_HEADS = 8
NSA_KV_HEADS = 2
NSA_GROUP = NSA_HEADS // NSA_KV_HEADS
SB_HEADS = 4
MOBA_HEADS = 4
CMP_LEN = 32
CMP_STRIDE = 16
SEL_BLOCK = 64
SEL_TOPN = 16
WINDOW = 512
MOBA_BLOCK = 256
MOBA_TOPK = 3
ROPE_THETA = 10000.0
RMS_EPS = 1e-6
NEG = -1e30
FORCE_BONUS = 1e4
SB_DEAD = -104.0

LANES = 128
SUBLANES = 8
MXU_DTYPE = jnp.bfloat16
VMEM_LIMIT = 56 * 1024 * 1024
SAMPLE_ROWS = 8
PAGES_PER_STEP = 32
NSA_ROW_CHUNKS = 4

NSA_Q_COLS = NSA_HEADS * HEAD_DIM
NSA_KV_COLS = 6 * NSA_KV_HEADS * HEAD_DIM
NSA_G_COLS = 3 * NSA_HEADS
SB_COLS = 3 * SB_HEADS * HEAD_DIM
MOBA_COLS = 3 * MOBA_HEADS * HEAD_DIM

_F32 = jnp.float32


def _dot(a, b):
    return jnp.dot(a, b, preferred_element_type=_F32)


def _dot_nt(a, b):
    return lax.dot_general(a, b, (((1,), (1,)), ((), ())), preferred_element_type=_F32)


def _dot_split(x, w):
    hi = x.astype(MXU_DTYPE)
    mid = (x - hi.astype(_F32)).astype(MXU_DTYPE)
    return _dot(hi, w) + _dot(mid, w)


def _rms(x, g):
    return x * lax.rsqrt(jnp.mean(x * x, axis=-1, keepdims=True) + RMS_EPS) * g


def _params(*sem):
    return pltpu.CompilerParams(dimension_semantics=sem, vmem_limit_bytes=VMEM_LIMIT)


def _full(shape):
    n = len(shape)
    return pl.BlockSpec(shape, lambda *_: (0,) * n)


def _ffn_kernel(*refs, with_ple):
    if with_ple:
        (h_ref, gpre_ref, gpost_ref, wg_ref, wu_ref, wd_ref,
         p_ref, wpg_ref, wpp_ref, gple_ref, o_ref, xn_ref, acc_ref) = refs
    else:
        h_ref, gpre_ref, gpost_ref, wg_ref, wu_ref, wd_ref, o_ref, xn_ref, acc_ref = refs
    xn_ref[...] = _rms(h_ref[...], gpre_ref[...]).astype(MXU_DTYPE)
    acc_ref[...] = jnp.zeros_like(acc_ref)

    def chunk(j, carry):
        xn = xn_ref[...]
        g = _dot(xn, wg_ref[j])
        u = _dot(xn, wu_ref[j])
        a = (g * jax.nn.sigmoid(g)) * u
        acc_ref[...] += _dot(a.astype(MXU_DTYPE), wd_ref[j])
        return carry

    lax.fori_loop(0, wg_ref.shape[0], chunk, 0)
    h = h_ref[...] + 0.5 * _rms(acc_ref[...], gpost_ref[...])
    if with_ple:
        gate = jax.nn.sigmoid(_dot(h.astype(MXU_DTYPE), wpg_ref[...]))
        ple = gate * _dot(p_ref[...].astype(MXU_DTYPE), wpp_ref[...])
        h = h + _rms(ple, gple_ref[...])
    o_ref[...] = h


def _resident(shape):
    n = len(shape)
    return pl.BlockSpec(shape, lambda *_: (0,) * n, pipeline_mode=pl.Buffered(1))


def _ffn(h, g_pre, g_post, wg, wu, wd, ple=None, *, tm):
    n, d = h.shape
    row = lambda i: (i, 0)
    in_specs = [pl.BlockSpec((tm, d), row), _full((1, d)), _full((1, d)),
                _resident(wg.shape), _resident(wu.shape), _resident(wd.shape)]
    args = [h, g_pre, g_post, wg, wu, wd]
    if ple is not None:
        p, wpg, wpp, g_ple = ple
        in_specs += [pl.BlockSpec((tm, p.shape[1]), row), _resident(wpg.shape), _resident(wpp.shape), _full((1, d))]
        args += [p, wpg, wpp, g_ple]
    return pl.pallas_call(
        functools.partial(_ffn_kernel, with_ple=ple is not None),
        grid=(n // tm,),
        in_specs=in_specs,
        out_specs=pl.BlockSpec((tm, d), row),
        out_shape=jax.ShapeDtypeStruct((n, d), _F32),
        scratch_shapes=[pltpu.VMEM((tm, d), MXU_DTYPE), pltpu.VMEM((tm, d), _F32)],
        compiler_params=_params("parallel"),
        name="ffn_ple" if ple is not None else "ffn",
    )(*args)


_G_QNSA = 0
_G_KV = 8
_G_SB = 14
_G_MB = 20
_G_GATE = 26
_ROPE_GROUPS = tuple(range(8)) + (8, 10, 12, 20, 21, 22, 23)
_INPROJ_OUT = (("q_nsa", _G_QNSA, 8, True, None), ("nsa", _G_KV, 4, False, _F32), ("win", _G_KV + 4, 2, False, _F32),
               ("q_sb", _G_SB, 2, True, None), ("sb", _G_SB + 2, 4, False, _F32),
               ("q_mb", _G_MB, 2, False, _F32), ("mb", _G_MB + 2, 4, False, _F32))
_INPROJ_NARROW = ("nsa", "win", "sb", "mb")


_INPROJ_STATE = ("nsa", "sb", "mb")


def _inproj_kernel(h_ref, g_ref, w_ref, cos_ref, sa_ref, sb_ref, *refs, n_carried, feature_major):
    out_refs = refs[n_carried:]
    n_stacked = feature_major
    xn = _rms(h_ref[...], g_ref[...]).astype(MXU_DTYPE)
    cos, sa, sb = cos_ref[...], sa_ref[...], sb_ref[...]
    scale = HEAD_DIM ** -0.5
    tm = xn.shape[0]

    pairs = {}

    def group(gi):
        if gi not in pairs:
            g0 = gi - gi % 2
            wide = _dot(xn, w_ref[:, g0 * LANES:min(g0 + 2, _G_GATE + 1) * LANES])
            for k in range(wide.shape[1] // LANES):
                pairs[g0 + k] = wide[:, k * LANES:(k + 1) * LANES]
        y = pairs.pop(gi)
        if gi in _ROPE_GROUPS:
            y = y * cos + pltpu.roll(y, LANES - HEAD_DIM // 2, 1) * sa + pltpu.roll(y, HEAD_DIM // 2, 1) * sb
        return y

    outs = out_refs[:len(_INPROJ_OUT)]
    narrow = dict(zip(_INPROJ_NARROW, out_refs[len(_INPROJ_OUT):]))
    gate_ref = out_refs[len(_INPROJ_OUT) + len(_INPROJ_NARROW)]
    means = []
    for (name, g0, n_groups, scaled, _), o_ref in zip(_INPROJ_OUT, outs):
        for i in range(n_groups):
            y = group(g0 + i)
            if scaled:
                y = y * scale
            if n_stacked and name in _INPROJ_STATE:
                o_ref[0, 0, i * LANES:(i + 1) * LANES, :] = y.T
            else:
                o_ref[:, i * LANES:(i + 1) * LANES] = y.astype(o_ref.dtype)
            if name in narrow:
                narrow[name][:, i * LANES:(i + 1) * LANES] = y.astype(MXU_DTYPE)
            if n_stacked and name == "nsa" and i < 2:
                out_refs[-2][:, i * LANES:(i + 1) * LANES] = y
            if n_stacked and name == "mb" and i < 2:
                means.append([jnp.mean(y[r:r + MOBA_BLOCK], axis=0, keepdims=True) for r in range(0, tm, MOBA_BLOCK)])
    gate_ref[...] = jax.nn.sigmoid(group(_G_GATE))
    if n_stacked:
        rows = [jnp.concatenate([means[0][r], means[1][r]], axis=1) for r in range(tm // MOBA_BLOCK)]
        rows.append(jnp.zeros((SUBLANES - len(rows), 2 * LANES), _F32))
        out_refs[-1][...] = jnp.concatenate(rows, axis=0)


def _inproj(h, g, w_all, tabs, tab_map, *, tm, stack=None, carried=()):
    n, d = h.shape
    row = lambda i: (i, 0)
    tab_spec = pl.BlockSpec((tm, LANES), tab_map)
    sizes = {o[0]: o[2] * LANES for o in _INPROJ_OUT}
    names = [o[0] for o in _INPROJ_OUT] + [k + "_bf" for k in _INPROJ_NARROW] + ["gates"]
    shapes = [jax.ShapeDtypeStruct((n, o[2] * LANES), o[4] or MXU_DTYPE) for o in _INPROJ_OUT]
    shapes += [jax.ShapeDtypeStruct((n, sizes[k]), MXU_DTYPE) for k in _INPROJ_NARROW]
    shapes.append(jax.ShapeDtypeStruct((n, LANES), _F32))
    specs = [pl.BlockSpec((tm, s.shape[1]), row) for s in shapes]
    aliases = {}
    n_fixed_inputs = 6
    if stack is not None:
        depth, layer, b, t = stack
        per_seq = t // tm
        assert tm % MOBA_BLOCK == 0 and tm // MOBA_BLOCK <= SUBLANES and len(carried) in (0, len(_INPROJ_STATE))
        for k, name in enumerate(_INPROJ_STATE):
            idx = names.index(name)
            shapes[idx] = jax.ShapeDtypeStruct((depth, b, sizes[name], t), _F32)
            specs[idx] = pl.BlockSpec((1, 1, sizes[name], tm), lambda i: (layer, i // per_seq, 0, i % per_seq))
            if carried:
                aliases[n_fixed_inputs + k] = idx
        names += ["cmp_rows", "kmean"]
        shapes += [jax.ShapeDtypeStruct((n, 2 * LANES), _F32),
                   jax.ShapeDtypeStruct((n // tm * SUBLANES, 2 * LANES), _F32)]
        specs += [pl.BlockSpec((tm, 2 * LANES), row), pl.BlockSpec((SUBLANES, 2 * LANES), row)]
    outs = pl.pallas_call(
        functools.partial(_inproj_kernel, n_carried=len(carried), feature_major=stack is not None),
        grid=(n // tm,),
        in_specs=[pl.BlockSpec((tm, d), row), _full((1, d)), _full(w_all.shape), tab_spec, tab_spec, tab_spec]
        + [pl.BlockSpec(memory_space=pl.ANY)] * len(carried),
        out_specs=specs,
        out_shape=shapes,
        input_output_aliases=aliases,
        compiler_params=_params("parallel"),
        name="inproj",
    )(h, g, w_all, *tabs, *carried)
    return dict(zip(names, outs))


def _merge_kernel(h_ref, gpre_ref, gpost_ref, onsa_ref, osb_ref, omb_ref,
                  wbn_ref, wbs_ref, wbm_ref, wgate_ref, wout_ref, o_ref):
    d = h_ref.shape[1]
    h = h_ref[...]
    u = _rms(h, gpre_ref[...]).astype(MXU_DTYPE)
    branches = (_dot(onsa_ref[...], wbn_ref[...]), _dot(osb_ref[...], wbs_ref[...]), _dot(omb_ref[...], wbm_ref[...]))
    mix = None
    for c, br in enumerate(branches):
        term = jax.nn.sigmoid(_dot(u, wgate_ref[:, c * d:(c + 1) * d])) * br
        mix = term if mix is None else mix + term
    y = _dot(mix.astype(MXU_DTYPE), wout_ref[...])
    o_ref[...] = h + _rms(y, gpost_ref[...])


def _merge(h, g_pre, g_post, o_nsa, o_sb, o_mb, wbn, wbs, wbm, wgate, wout, *, tm):
    n, d = h.shape
    row = lambda i: (i, 0)
    return pl.pallas_call(
        _merge_kernel,
        grid=(n // tm,),
        in_specs=[pl.BlockSpec((tm, d), row), _full((1, d)), _full((1, d)),
                  pl.BlockSpec((tm, o_nsa.shape[1]), row), pl.BlockSpec((tm, o_sb.shape[1]), row),
                  pl.BlockSpec((tm, o_mb.shape[1]), row),
                  _full(wbn.shape), _full(wbs.shape), _full(wbm.shape), _full(wgate.shape), _full(wout.shape)],
        out_specs=pl.BlockSpec((tm, d), row),
        out_shape=jax.ShapeDtypeStruct((n, d), _F32),
        compiler_params=_params("parallel"),
        name="merge",
    )(h, g_pre, g_post, o_nsa, o_sb, o_mb, wbn, wbs, wbm, wgate, wout)


def _lane_lo(shape=(1, LANES)):
    return (lax.broadcasted_iota(jnp.int32, shape, len(shape) - 1) % LANES) < HEAD_DIM


def _softplus(z):
    return jnp.maximum(z, 0.0) + jnp.log1p(jnp.exp(-jnp.abs(z)))


def _cumsum_matrix(tk):
    j = lax.broadcasted_iota(jnp.int32, (tk, 2 * tk), 0)
    s = lax.broadcasted_iota(jnp.int32, (tk, 2 * tk), 1)
    return jnp.where((j > s) | (s >= tk), 1.0, 0.0).astype(MXU_DTYPE)


def _online_update1(state, z, v1, nt=False):
    m, acc = state
    m_new = jnp.maximum(m, jnp.max(z, axis=-1, keepdims=True))
    p = jnp.exp(z - m_new).astype(MXU_DTYPE)
    return m_new, jnp.exp(m - m_new) * acc + (_dot_nt(p, v1) if nt else _dot(p, v1))


def _online_init1(rows, width=LANES):
    return jnp.full((rows, 1), NEG, _F32), jnp.zeros((rows, width + LANES), _F32)


def _finish1(state):
    _, acc = state
    width = acc.shape[1] - LANES
    den = 1.0 / acc[:, width:]
    return acc[:, :width] * jnp.concatenate([den] * (width // LANES), axis=1)


def _with_ones(v, axis=1):
    shape = (v.shape[0], LANES) if axis == 1 else (LANES, v.shape[1])
    return jnp.concatenate([v, jnp.ones(shape, v.dtype)], axis=axis)


def _seed_state(m, l, acc):
    return m, jnp.concatenate([acc, jnp.broadcast_to(l, (l.shape[0], LANES))], axis=1)


def _topk_mask(val, k, axis):
    idx = lax.broadcasted_iota(jnp.int32, val.shape, axis).astype(_F32)
    sel = jnp.zeros(val.shape, _F32)
    for _ in range(k):
        m = jnp.max(val, axis=axis, keepdims=True)
        first = jnp.min(jnp.where(val == m, idx, 1e9), axis=axis, keepdims=True)
        hit = idx == first
        sel = jnp.where(hit, 1.0, sel)
        val = jnp.where(hit, -jnp.inf, val)
    return sel


def _stack_nsa_queries(q):
    return jnp.concatenate([q[:, hh * LANES:(hh + 1) * LANES] for hh in range(NSA_HEADS)], axis=0)


def _cmp_branch(qs, kc, vc, qpos, tq):
    nc = kc.shape[0]
    s = _dot_nt(qs, kc)
    cend = lax.broadcasted_iota(jnp.int32, (1, nc), 1) * CMP_STRIDE + (CMP_LEN - 1)
    vis = cend <= qpos
    sm = jnp.where(vis, s, NEG)
    m = jnp.max(sm, axis=-1, keepdims=True)
    e = jnp.where(vis, jnp.exp(sm - m), 0.0)
    l = jnp.sum(e, axis=-1, keepdims=True)
    p = e * (1.0 / jnp.where(l > 0.0, l, 1.0))
    o_cmp = _dot(p.astype(MXU_DTYPE), vc)
    psums = []
    for kv in range(NSA_KV_HEADS):
        blocks = [p[(kv * NSA_GROUP + g) * tq:(kv * NSA_GROUP + g + 1) * tq] for g in range(NSA_GROUP)]
        psums.append(functools.reduce(lambda a, b: a + b, blocks))
    return o_cmp, psums


def _nsa_select(imp, qp, n_blocks, n_top, axis):
    jblk = lax.broadcasted_iota(jnp.int32, imp.shape, axis)
    cur = qp // SEL_BLOCK
    forced = (jblk == 0) | (jblk == cur) | (jblk == cur - 1)
    valid = (jblk * SEL_BLOCK <= qp) & (jblk < n_blocks)
    val = jnp.where(valid, imp + jnp.where(forced, FORCE_BONUS, 0.0), NEG)
    return jnp.where(valid, _topk_mask(val, n_top, axis), 0.0) - 1.0


def _nsa_lhs(qs, biases):
    bias = jnp.concatenate([b for b in biases for _ in range(NSA_GROUP)], axis=0)
    return jnp.concatenate([qs, bias.astype(qs.dtype)], axis=1)


def _combine_nsa(gates, o_cmp, o_sel, o_win, tq):
    lo = _lane_lo()
    heads = []
    for hh in range(NSA_HEADS):
        r = slice(hh * tq, (hh + 1) * tq)
        o = (gates[:, 3 * hh:3 * hh + 1] * o_cmp[r] + gates[:, 3 * hh + 1:3 * hh + 2] * o_sel[r]
             + gates[:, 3 * hh + 2:3 * hh + 3] * o_win[r])
        if hh % 2 != hh // NSA_GROUP:
            o = pltpu.roll(o, HEAD_DIM, 1)
        heads.append(o)
    return jnp.concatenate([jnp.where(lo, heads[2 * i], heads[2 * i + 1]) for i in range(NSA_HEADS // 2)], axis=1)


def _moba_pick(s, own, axis):
    past = lax.broadcasted_iota(jnp.int32, s.shape, axis) < own
    return jnp.where(past, _topk_mask(jnp.where(past, s, NEG), MOBA_TOPK, axis), 0.0) - 1.0


def _sb_block(qh, kt, vt, cum, carry, keep):
    c, o = carry
    tk = kt.shape[0]
    z = _dot_nt(qh, kt)
    lk = -_softplus(z)
    if keep is not None:
        lk = jnp.where(keep, lk, 0.0)
    r = _dot_split(lk, cum)
    a = jnp.exp(lk + z + r[:, :tk] + c)
    if keep is not None:
        a = jnp.where(keep, a, 0.0)
    return c + r[:, tk:], o + _dot(a.astype(MXU_DTYPE), vt)


def _stage_chunks(src_ref, lead, n_chunks, dst_ref, c0):
    for r in range(CMP_STRIDE):
        x = src_ref[lead + (pl.ds(r, n_chunks, stride=CMP_STRIDE), slice(None))]
        dst_ref[pl.ds(c0, n_chunks), r * LANES:(r + 1) * LANES] = x


def _compress_finish(xk_ref, xv_ref, wk_ref, wv_ref, pe_ref, w2_ref, ok_ref, ov_ref):
    nc = xk_ref.shape[0]
    for x_ref, w_ref, t, o_ref in ((xk_ref, wk_ref, 0, ok_ref), (xv_ref, wv_ref, 1, ov_ref)):
        ab = _dot(x_ref[...].astype(MXU_DTYPE), w_ref[...])
        cst = _dot(pe_ref[t].astype(MXU_DTYPE), w_ref[...])
        for h in range(NSA_KV_HEADS):
            a = ab[:, h * 256:h * 256 + LANES]
            b = ab[:, h * 256 + LANES:(h + 1) * 256]
            c = cst[0:1, h * 256:h * 256 + LANES] + cst[1:2, h * 256 + LANES:(h + 1) * 256]
            hid = jax.nn.gelu(a + pltpu.roll(b, nc - 1, 0) + c)
            part = _dot(hid.astype(MXU_DTYPE), w2_ref[t, h])
            out = part if h == 0 else out + part
        o_ref[0] = out.astype(o_ref.dtype)


def _compress_prompt_kernel(rk_ref, rv_ref, wk_ref, wv_ref, pe_ref, w2_ref, ok_ref, ov_ref, xk_ref, xv_ref):
    n_chunks = rk_ref.shape[1] // CMP_STRIDE
    _stage_chunks(rk_ref, (0,), n_chunks, xk_ref, 0)
    _stage_chunks(rv_ref, (0,), n_chunks, xv_ref, 0)
    _compress_finish(xk_ref, xv_ref, wk_ref, wv_ref, pe_ref, w2_ref, ok_ref, ov_ref)


def _compress_sample_kernel(pt_ref, wk_ref, wv_ref, pe_ref, w2_ref, *refs, n_pages_step):
    del pt_ref
    pages = refs[:n_pages_step]
    ok_ref, ov_ref, xk_ref, xv_ref, rows_ref = refs[n_pages_step:]
    s = pl.program_id(1)
    per_page = pages[0].shape[3] // CMP_STRIDE
    for gp, page in enumerate(pages):
        x = page[0, 0]
        rows_ref[0] = x[:LANES].T
        rows_ref[1] = x[LANES:].T
        c0 = pl.multiple_of((s * n_pages_step + gp) * per_page, per_page)
        _stage_chunks(rows_ref, (0,), per_page, xk_ref, c0)
        _stage_chunks(rows_ref, (1,), per_page, xv_ref, c0)

    @pl.when(s == pl.num_programs(1) - 1)
    def _():
        _compress_finish(xk_ref, xv_ref, wk_ref, wv_ref, pe_ref, w2_ref, ok_ref, ov_ref)


def _compress_weights(pe, w1, w2):
    eye = jnp.eye(NSA_KV_HEADS, dtype=w1.dtype)
    big, pes = [], []
    for t in range(2):
        w1r = w1[t].reshape(2, CMP_STRIDE, HEAD_DIM, w1.shape[-1])
        wb = jnp.einsum("hH,ardn->rhdHan", eye, w1r)
        big.append(wb.reshape(CMP_STRIDE * NSA_KV_HEADS * HEAD_DIM, -1).astype(MXU_DTYPE))
        per = pe[t].reshape(2, CMP_STRIDE, 1, HEAD_DIM)
        rows = jnp.broadcast_to(per, (2, CMP_STRIDE, NSA_KV_HEADS, HEAD_DIM)).reshape(2, -1)
        pes.append(jnp.concatenate([rows, jnp.zeros((SUBLANES - 2, rows.shape[1]), rows.dtype)], axis=0))
    zero = jnp.zeros_like(w2)
    w2p = jnp.stack([jnp.concatenate([w2, zero], axis=-1), jnp.concatenate([zero, w2], axis=-1)], axis=1)
    return big[0], big[1], jnp.stack(pes), w2p.astype(MXU_DTYPE)


def _compress_out(b, nc):
    shape = jax.ShapeDtypeStruct((b, nc, LANES), MXU_DTYPE)
    return [shape, shape]


def _compress_prompt(nsa_rows, cw):
    b, t, _ = nsa_rows.shape
    nc = t // CMP_STRIDE
    wk, wv, pe2, w2p = cw
    out_spec = pl.BlockSpec((1, nc, LANES), lambda i: (i, 0, 0))
    return pl.pallas_call(
        _compress_prompt_kernel,
        grid=(b,),
        in_specs=[pl.BlockSpec((1, t, LANES), lambda i: (i, 0, 0)),
                  pl.BlockSpec((1, t, LANES), lambda i: (i, 0, 1)),
                  _full(wk.shape), _full(wv.shape), _full(pe2.shape), _full(w2p.shape)],
        out_specs=[out_spec, out_spec],
        out_shape=_compress_out(b, nc),
        scratch_shapes=[pltpu.VMEM((nc, CMP_STRIDE * LANES), _F32)] * 2,
        compiler_params=_params("arbitrary"),
        name="nsa_compress_prompt",
    )(nsa_rows, nsa_rows, wk, wv, pe2, w2p)


def _nsa_prompt_kernel(q_ref, g_ref, ck_ref, cv_ref, sk_ref, sv_ref, wk_ref, wv_ref, eblk_ref, cover_ref,
                       o_ref, *, tq, tk, n_top):
    qi = pl.program_id(1)
    t0 = qi * tq
    rows = NSA_HEADS * tq
    n_blocks = sk_ref.shape[1] // SEL_BLOCK
    qs = _stack_nsa_queries(q_ref[0])
    qpos_r = t0 + lax.broadcasted_iota(jnp.int32, (rows, 1), 0) % tq

    o_cmp, psums = _cmp_branch(qs, ck_ref[0], cv_ref[0], qpos_r, tq)
    qp_t = t0 + lax.broadcasted_iota(jnp.int32, (1, tq), 1)
    biases = []
    for psum in psums:
        imp_t = _dot_split(psum, cover_ref[...]).T
        biases.append(_nsa_select(imp_t, qp_t, n_blocks, n_top, 0).T)
    lhs = _nsa_lhs(qs, biases)

    kcol = lax.broadcasted_iota(jnp.int32, (1, tk), 1)

    chunk = rows // NSA_ROW_CHUNKS
    lhs_c = [lhs[c * chunk:(c + 1) * chunk] for c in range(NSA_ROW_CHUNKS)]
    qpos_c = [qpos_r[c * chunk:(c + 1) * chunk] for c in range(NSA_ROW_CHUNKS)]

    def sel_tile(kt, states, causal):
        start = pl.multiple_of(kt * tk, tk)
        rhs = jnp.concatenate([sk_ref[0, pl.ds(start, tk), :], eblk_ref[pl.ds(start, tk), :]], axis=1)
        v1 = _with_ones(sv_ref[0, pl.ds(start, tk), :])
        out = []
        for c in range(NSA_ROW_CHUNKS):
            z = _dot_nt(lhs_c[c], rhs)
            if causal:
                z = jnp.where(start + kcol <= qpos_c[c], z, NEG)
            out.append(_online_update1(states[c], z, v1))
        return tuple(out)

    n_full = t0 // tk
    init = tuple(_online_init1(chunk) for _ in range(NSA_ROW_CHUNKS))
    states = lax.fori_loop(0, n_full, lambda kt, st: sel_tile(kt, st, False), init)
    o_sel = jnp.concatenate([_finish1(st) for st in sel_tile(n_full, states, True)], axis=0)

    n_band = WINDOW + tq
    ws = pl.multiple_of(jnp.maximum(t0 - WINDOW, 0), tq)
    wk, wv1 = wk_ref[0, pl.ds(ws, n_band), :], _with_ones(wv_ref[0, pl.ds(ws, n_band), :])
    kpos = ws + lax.broadcasted_iota(jnp.int32, (1, n_band), 1)
    o_win = []
    for c in range(NSA_ROW_CHUNKS):
        dist = qpos_c[c] - kpos
        z = jnp.where((dist >= 0) & (dist <= WINDOW), _dot_nt(lhs_c[c][:, :LANES], wk), NEG)
        o_win.append(_finish1(_online_update1(_online_init1(chunk), z, wv1)))
    o_win = jnp.concatenate(o_win, axis=0)

    o_ref[0] = _combine_nsa(g_ref[0], o_cmp, o_sel, o_win, tq).astype(o_ref.dtype)


def _nsa_prompt(q_nsa, gates, cmp_k, cmp_v, nsa_bf, win_bf, eblk, cover, *, tq, tk):
    b, t, _ = q_nsa.shape
    nc = cmp_k.shape[1]
    n_top = min(SEL_TOPN, t // SEL_BLOCK)
    cmp_spec = pl.BlockSpec((1, nc, LANES), lambda i, qi: (i, 0, 0))
    col = lambda g: pl.BlockSpec((1, t, LANES), lambda i, qi: (i, 0, g))
    return pl.pallas_call(
        functools.partial(_nsa_prompt_kernel, tq=tq, tk=tk, n_top=n_top),
        grid=(b, t // tq),
        in_specs=[pl.BlockSpec((1, tq, NSA_HEADS * LANES), lambda i, qi: (i, qi, 0)),
                  pl.BlockSpec((1, tq, LANES), lambda i, qi: (i, qi, 0)),
                  cmp_spec, cmp_spec, col(2), col(3), col(0), col(1),
                  _full(eblk.shape), _full(cover.shape)],
        out_specs=pl.BlockSpec((1, tq, NSA_Q_COLS), lambda i, qi: (i, qi, 0)),
        out_shape=jax.ShapeDtypeStruct((b, t, NSA_Q_COLS), MXU_DTYPE),
        compiler_params=_params("parallel", "arbitrary"),
        name="nsa_prompt",
    )(q_nsa, gates, cmp_k, cmp_v, nsa_bf, nsa_bf, win_bf, win_bf, eblk, cover)


def _sb_prompt_kernel(q_ref, k_ref, v_ref, o_ref, *, tq, tk):
    qi = pl.program_id(1)
    lo = _lane_lo()
    n_pair = q_ref.shape[2] // LANES
    qh = []
    for p in range(n_pair):
        q = q_ref[0, :, p * LANES:(p + 1) * LANES]
        zero_q = jnp.zeros_like(q)
        qh += [jnp.where(lo, q, zero_q), jnp.where(lo, zero_q, q)]
    cum = _cumsum_matrix(tk)
    row = lax.broadcasted_iota(jnp.int32, (tq, tk), 0)
    col = lax.broadcasted_iota(jnp.int32, (tq, tk), 1)
    per_tile = tq // tk

    def blocks(kb, carry, keep):
        start = pl.multiple_of(kb * tk, tk)
        out = []
        for p in range(n_pair):
            kt = k_ref[0, pl.ds(start, tk), p * LANES:(p + 1) * LANES]
            vt = v_ref[0, pl.ds(start, tk), p * LANES:(p + 1) * LANES]
            out += [_sb_block(qh[2 * p + h], kt, vt, cum, carry[2 * p + h], keep) for h in range(2)]
        return tuple(out)

    def alive(carry):
        c_max = functools.reduce(jnp.maximum, [c for c, _ in carry])
        return (jnp.max(c_max) > SB_DEAD).astype(jnp.int32)

    zero = jnp.zeros((tq, LANES), _F32)
    carry = ((zero, zero),) * (2 * n_pair)
    for d in reversed(range(per_tile)):
        carry = blocks(qi * per_tile + d, carry, d * tk + col < row)

    def body(st):
        i, _, carry = st
        for d in range(per_tile):
            carry = blocks((qi - 1 - i) * per_tile + per_tile - 1 - d, carry, None)
        return i + 1, alive(carry), carry

    _, _, carry = lax.while_loop(lambda st: (st[0] < qi) & (st[1] > 0), body, (0, alive(carry), carry))
    o_ref[0] = jnp.concatenate([jnp.where(lo, carry[2 * p][1], carry[2 * p + 1][1]) for p in range(n_pair)],
                               axis=1).astype(o_ref.dtype)


def _sb_prompt(q_sb, sb_bf, *, tq, tk):
    b, t, width = q_sb.shape
    return pl.pallas_call(
        functools.partial(_sb_prompt_kernel, tq=tq, tk=tk),
        grid=(b, t // tq),
        in_specs=[pl.BlockSpec((1, tq, width), lambda i, qi: (i, qi, 0)),
                  pl.BlockSpec((1, t, width), lambda i, qi: (i, 0, 0)),
                  pl.BlockSpec((1, t, width), lambda i, qi: (i, 0, 1))],
        out_specs=pl.BlockSpec((1, tq, width), lambda i, qi: (i, qi, 0)),
        out_shape=jax.ShapeDtypeStruct((b, t, width), MXU_DTYPE),
        compiler_params=_params("parallel", "arbitrary"),
        name="sb_prompt",
    )(q_sb, sb_bf, sb_bf)


def _moba_prompt_kernel(q_ref, km_ref, k_ref, v_ref, eblk_ref, o_ref, *, tq, tk):
    qi = pl.program_id(1)
    lo = _lane_lo()
    n_pair = q_ref.shape[2] // LANES
    scale = HEAD_DIM ** -0.5
    row = lax.broadcasted_iota(jnp.int32, (tq, tq), 0)
    col = lax.broadcasted_iota(jnp.int32, (tq, tq), 1)
    causal = jnp.concatenate([col <= row] * 2, axis=0)
    own0 = pl.multiple_of(qi * tq, tq)
    pair = lambda p: slice(p * LANES, (p + 1) * LANES)
    lhs, states = [], []
    for p in range(n_pair):
        q = q_ref[0, :, pair(p)]
        halves = []
        for hh in range(2):
            qf = jnp.where(lo if hh == 0 else jnp.logical_not(lo), q, 0.0)
            gate = lax.dot_general(qf, km_ref[0, :, pair(p)], (((1,), (1,)), ((), ())), preferred_element_type=_F32,
                                   precision=lax.Precision.HIGHEST)
            bias = _moba_pick(gate.T, qi, 0).T
            halves.append(jnp.concatenate([(qf * scale).astype(MXU_DTYPE), bias.astype(MXU_DTYPE)], axis=1))
        lhs.append(jnp.concatenate(halves, axis=0))
        z = jnp.where(causal, _dot_nt(lhs[p][:, :LANES], k_ref[0, pl.ds(own0, tq), pair(p)]), NEG)
        states.append(_online_update1(_online_init1(2 * tq), z, _with_ones(v_ref[0, pl.ds(own0, tq), pair(p)])))

    def tile(j, states):
        start = pl.multiple_of(j * tk, tk)
        eblk = eblk_ref[pl.ds(start, tk), :]
        out = []
        for p in range(n_pair):
            rhs = jnp.concatenate([k_ref[0, pl.ds(start, tk), pair(p)], eblk], axis=1)
            out.append(_online_update1(states[p], _dot_nt(lhs[p], rhs), _with_ones(v_ref[0, pl.ds(start, tk), pair(p)])))
        return tuple(out)

    states = lax.fori_loop(0, (qi * tq + tk - 1) // tk, tile, tuple(states))
    outs = [_finish1(st) for st in states]
    o_ref[0] = jnp.concatenate([jnp.where(lo, o[:tq], o[tq:]) for o in outs], axis=1).astype(o_ref.dtype)


def _moba_prompt(q_mb, kmean, mb_bf, eblk, *, tk):
    b, t, width = q_mb.shape
    tq = MOBA_BLOCK
    return pl.pallas_call(
        functools.partial(_moba_prompt_kernel, tq=tq, tk=tk),
        grid=(b, t // tq),
        in_specs=[pl.BlockSpec((1, tq, width), lambda i, qi: (i, qi, 0)),
                  pl.BlockSpec((1, LANES, width), lambda i, qi: (i, 0, 0)),
                  pl.BlockSpec((1, t, width), lambda i, qi: (i, 0, 0)),
                  pl.BlockSpec((1, t, width), lambda i, qi: (i, 0, 1)),
                  _full(eblk.shape)],
        out_specs=pl.BlockSpec((1, tq, width), lambda i, qi: (i, qi, 0)),
        out_shape=jax.ShapeDtypeStruct((b, t, width), MXU_DTYPE),
        compiler_params=_params("parallel", "arbitrary"),
        name="moba_prompt",
    )(q_mb, kmean, mb_bf, mb_bf, eblk)


def _row_scores(q, keys, n):
    return [jnp.sum(q * keys[j:j + 1, :], axis=-1, keepdims=True) for j in range(n)]


def _new_key_softmax(q, keys, vals, n, visible):
    zs = [jnp.where(visible(j), z, NEG) for j, z in enumerate(_row_scores(q, keys, n))]
    m = functools.reduce(jnp.maximum, zs)
    l = jnp.zeros_like(m)
    acc = jnp.zeros((q.shape[0], vals.shape[1]), _F32)
    for j, z in enumerate(zs):
        p = jnp.exp(z - m)
        l = l + p
        acc = acc + p * vals[j:j + 1, :]
    return m, l, acc


def _stack_heads(q, n_heads):
    head = lax.broadcasted_iota(jnp.int32, (1, q.shape[1]), 1) // HEAD_DIM
    return jnp.concatenate([jnp.where(head == h, q, jnp.zeros_like(q)) for h in range(n_heads)], axis=0)


def _unstack_heads(o, n_heads):
    rows = o.shape[0] // n_heads
    head = lax.broadcasted_iota(jnp.int32, (1, o.shape[1]), 1) // HEAD_DIM
    out = jnp.zeros((rows, o.shape[1]), _F32)
    for h in range(n_heads):
        out = jnp.where(head == h, o[h * rows:(h + 1) * rows], out)
    return out


def _side_by_side(pages, f0, f1):
    return jnp.concatenate([p[0, 0, f0:f1, :] for p in pages], axis=1).astype(MXU_DTYPE)


def _key_block_onehot(n_keys, first_key, block):
    key_blk = (first_key + lax.broadcasted_iota(jnp.int32, (LANES, n_keys), 1)) // block
    return jnp.where(lax.broadcasted_iota(jnp.int32, (LANES, n_keys), 0) == key_blk, -NEG, 0.0).astype(MXU_DTYPE)


def _state_refs_store(m_ref, acc_ref, state):
    m_ref[...] = jnp.broadcast_to(state[0], m_ref.shape)
    acc_ref[...] = state[1]


def _page_specs(layer, n, block, row_block, page_of):
    def spec(gp):
        return pl.BlockSpec(block, lambda b, s, pt: (layer, pt[b, page_of(s, gp)], row_block, 0))
    return [spec(gp) for gp in range(n)]


def _sample_call(kernel_fn, name, page_table, n_steps, pre_args, pre_specs, page_arrays, page_specs, out_block,
                 out_shape, scratch):
    grid_spec = pltpu.PrefetchScalarGridSpec(
        num_scalar_prefetch=1,
        grid=(page_table.shape[0], n_steps),
        in_specs=pre_specs + page_specs,
        out_specs=out_block,
        scratch_shapes=scratch,
    )
    return pl.pallas_call(kernel_fn, grid_spec=grid_spec, out_shape=out_shape,
                          compiler_params=_params("arbitrary", "arbitrary"), name=name,
                          )(page_table, *pre_args, *page_arrays)


def _seq_block(shape):
    zeros = (0,) * (len(shape) - 1)
    return pl.BlockSpec((1,) + tuple(shape[1:]), lambda b, s, pt: (b,) + zeros)


def _const_block(shape):
    zeros = (0,) * len(shape)
    return pl.BlockSpec(tuple(shape), lambda b, s, pt: zeros)


def _compress_sample(cache, layer, page_table, cw):
    b, n_pages = page_table.shape
    page = cache.shape[3]
    nc = n_pages * page // CMP_STRIDE
    g = PAGES_PER_STEP
    wk, wv, pe2, w2p = cw
    out_spec = pl.BlockSpec((1, nc, LANES), lambda i, s, pt: (i, 0, 0))
    return _sample_call(
        functools.partial(_compress_sample_kernel, n_pages_step=g), "nsa_compress_sample", page_table, n_pages // g,
        [wk, wv, pe2, w2p], [_const_block(wk.shape), _const_block(wv.shape), _const_block(pe2.shape),
                             _const_block(w2p.shape)],
        [cache] * g, _page_specs(layer, g, (1, 1, 2 * LANES, page), 0, lambda s, gp: s * g + gp),
        [out_spec, out_spec], _compress_out(b, nc),
        [pltpu.VMEM((nc, CMP_STRIDE * LANES), _F32)] * 2 + [pltpu.VMEM((2, page, LANES), _F32)])


def _nsa_sample_kernel(pt_ref, q_ref, g_ref, ck_ref, cv_ref, new_ref, neww_ref, win_ref, cover_ref, *refs,
                       n_pages_step, n_new, past_len, n_top):
    del pt_ref
    pages = refs[:n_pages_step]
    o_ref, m_ref, acc_ref, bias_ref, ocmp_ref, owin_ref = refs[n_pages_step:]
    s = pl.program_id(1)
    tq = SAMPLE_ROWS
    rows = NSA_HEADS * tq
    qs = _stack_nsa_queries(q_ref[0])
    qi_r = lax.broadcasted_iota(jnp.int32, (rows, 1), 0) % tq
    qpos_r = past_len + qi_r

    @pl.when(s == 0)
    def _():
        qf = qs.astype(_F32)
        o_cmp, psums = _cmp_branch(qs, ck_ref[0], cv_ref[0], qpos_r, tq)
        ocmp_ref[...] = o_cmp
        qp = past_len + lax.broadcasted_iota(jnp.int32, (tq, 1), 0)
        biases = [_nsa_select(_dot_split(psum, cover_ref[...]), qp, past_len // SEL_BLOCK, n_top - 1, 1)
                  for psum in psums]
        bias_ref[...] = jnp.concatenate([b for b in biases for _ in range(NSA_GROUP)], axis=0)
        visible = lambda j: j <= qi_r
        new = new_ref[0].astype(_F32)
        state = _seed_state(*_new_key_softmax(qf, new[:, 2 * LANES:3 * LANES], new[:, 3 * LANES:], n_new, visible))
        _state_refs_store(m_ref, acc_ref, state)
        wnew = neww_ref[0].astype(_F32)
        wstate = _seed_state(*_new_key_softmax(qf, wnew[:, :LANES], wnew[:, LANES:], n_new, visible))
        w = win_ref[0, 0]
        n_win = w.shape[1]
        dist = qpos_r - (past_len - n_win + lax.broadcasted_iota(jnp.int32, (1, n_win), 1))
        z = jnp.where((dist >= 0) & (dist <= WINDOW), _dot(qs, w[:LANES].astype(MXU_DTYPE)), NEG)
        owin_ref[...] = _finish1(_online_update1(wstate, z, _with_ones(w[LANES:].astype(MXU_DTYPE), 0), nt=True))

    n_keys = n_pages_step * pages[0].shape[3]
    lhs = jnp.concatenate([qs, bias_ref[...].astype(MXU_DTYPE)], axis=1)
    rhs = jnp.concatenate([_side_by_side(pages, 0, LANES), _key_block_onehot(n_keys, s * n_keys, SEL_BLOCK)], axis=0)
    state = (m_ref[:, 0:1], acc_ref[...])
    state = _online_update1(state, _dot(lhs, rhs), _with_ones(_side_by_side(pages, LANES, 2 * LANES), 0), nt=True)
    _state_refs_store(m_ref, acc_ref, state)

    @pl.when(s == pl.num_programs(1) - 1)
    def _():
        o_sel = _finish1((m_ref[:, 0:1], acc_ref[...]))
        o_ref[0] = _combine_nsa(g_ref[0], ocmp_ref[...], o_sel, owin_ref[...], tq).astype(o_ref.dtype)


def _nsa_sample(q, gates, cmp_k, cmp_v, new_bf, new_win_bf, cache, cache_win, layer, page_table, cover, n_new):
    b, n_pages = page_table.shape
    page = cache.shape[3]
    past_len = n_pages * page
    g = PAGES_PER_STEP
    rows = NSA_HEADS * SAMPLE_ROWS
    n_top = min(SEL_TOPN, past_len // SEL_BLOCK + 1)
    win_spec = pl.BlockSpec((1, 1) + cache_win.shape[2:], lambda i, s, pt: (layer, i, 0, 0))
    out_shape = (b, SAMPLE_ROWS, NSA_Q_COLS)
    vmem = lambda w: pltpu.VMEM((rows, w), _F32)
    return _sample_call(
        functools.partial(_nsa_sample_kernel, n_pages_step=g, n_new=n_new, past_len=past_len, n_top=n_top),
        "nsa_sample", page_table, n_pages // g,
        [q, gates, cmp_k, cmp_v, new_bf, new_win_bf, cache_win, cover],
        [_seq_block(q.shape), _seq_block(gates.shape), _seq_block(cmp_k.shape), _seq_block(cmp_v.shape),
         _seq_block(new_bf.shape), _seq_block(new_win_bf.shape), win_spec, _const_block(cover.shape)],
        [cache] * g, _page_specs(layer, g, (1, 1, 2 * LANES, page), 1, lambda s, gp: s * g + gp),
        _seq_block(out_shape), jax.ShapeDtypeStruct(out_shape, MXU_DTYPE),
        [vmem(LANES), vmem(2 * LANES), vmem(LANES), vmem(LANES), vmem(LANES)])


def _sb_sample_kernel(pt_ref, q_ref, new_ref, *refs, n_pages_step, n_new):
    del pt_ref
    pages = refs[:n_pages_step]
    o_ref, c_ref, acc_ref = refs[n_pages_step:]
    s = pl.program_id(1)
    width = SB_HEADS * HEAD_DIM
    qs = _stack_heads(q_ref[0], SB_HEADS)
    rows = qs.shape[0]
    qi_r = lax.broadcasted_iota(jnp.int32, (rows, 1), 0) % SAMPLE_ROWS

    @pl.when(s == 0)
    def _():
        qf = qs.astype(_F32)
        new = new_ref[0].astype(_F32)
        zs = _row_scores(qf, new[:, :width], n_new)
        c = jnp.zeros((rows, 1), _F32)
        o = jnp.zeros((rows, width), _F32)
        for j in reversed(range(n_new)):
            seen = j < qi_r
            lk = jnp.where(seen, -_softplus(zs[j]), 0.0)
            a = jnp.where(seen, jnp.exp(lk + zs[j] + c), 0.0)
            o = o + a * new[j:j + 1, width:]
            c = c + lk
        c_ref[...] = jnp.broadcast_to(c, c_ref.shape)
        acc_ref[...] = o

    page_keys = pages[0].shape[3]
    cum = _cumsum_matrix(page_keys)
    z = _dot(qs, _side_by_side(pages, 0, width))
    lk = -_softplus(z)
    c = c_ref[...]
    later = []
    for gp in range(n_pages_step):
        r = _dot_split(lk[:, gp * page_keys:(gp + 1) * page_keys], cum)
        later.append(r[:, :page_keys] + c)
        c = c + r[:, page_keys:]
    a = jnp.exp(lk + z + jnp.concatenate(later, axis=1))
    acc_ref[...] += _dot_nt(a.astype(MXU_DTYPE), _side_by_side(pages, width, 2 * width))
    c_ref[...] = c

    @pl.when(s == pl.num_programs(1) - 1)
    def _():
        o_ref[0] = _unstack_heads(acc_ref[...], SB_HEADS).astype(o_ref.dtype)


def _sb_sample(q, new_bf, cache, layer, page_table, n_new):
    b, n_pages = page_table.shape
    page = cache.shape[3]
    g = PAGES_PER_STEP
    rows = SB_HEADS * SAMPLE_ROWS
    width = SB_HEADS * HEAD_DIM
    return _sample_call(
        functools.partial(_sb_sample_kernel, n_pages_step=g, n_new=n_new),
        "sb_sample", page_table, n_pages // g,
        [q, new_bf], [_seq_block(q.shape), _seq_block(new_bf.shape)],
        [cache] * g, _page_specs(layer, g, (1, 1, 2 * width, page), 0, lambda s, gp: n_pages - 1 - (s * g + gp)),
        _seq_block(q.shape), jax.ShapeDtypeStruct(q.shape, MXU_DTYPE),
        [pltpu.VMEM((rows, LANES), _F32), pltpu.VMEM((rows, width), _F32)])


def _moba_sample_kernel(pt_ref, q_ref, new_ref, *refs, n_pages_step, n_new, past_len):
    del pt_ref
    pages = refs[:n_pages_step]
    o_ref, m_ref, part_ref, km_ref = refs[n_pages_step:]
    s = pl.program_id(1)
    width = MOBA_HEADS * HEAD_DIM
    page_keys = pages[0].shape[3]
    per_blk = MOBA_BLOCK // page_keys
    blk_step = n_pages_step // per_blk
    n_blk = past_len // MOBA_BLOCK
    qf = _stack_heads(q_ref[0], MOBA_HEADS)
    rows = qf.shape[0]
    qs = (qf * HEAD_DIM ** -0.5).astype(MXU_DTYPE)
    qi_r = lax.broadcasted_iota(jnp.int32, (rows, 1), 0) % SAMPLE_ROWS
    lane = lax.broadcasted_iota(jnp.int32, (1, LANES), 1)

    @pl.when(s == 0)
    def _():
        km_ref[...] = jnp.zeros_like(km_ref)

    km = km_ref[...]
    for jb in range(blk_step):
        blk = s * blk_step + jb
        mine = pages[jb * per_blk:(jb + 1) * per_blk]
        mean = sum(jnp.sum(p[0, 0, :width, :], axis=1, keepdims=True) for p in mine) * (1.0 / MOBA_BLOCK)
        km = km + jnp.where(lane == blk, mean, 0.0)
        z = _dot(qs, _side_by_side(mine, 0, width))
        m = jnp.max(z, axis=-1, keepdims=True)
        p = jnp.exp(z - m).astype(MXU_DTYPE)
        part_ref[blk] = _dot_nt(p, _with_ones(_side_by_side(mine, width, 2 * width), 0))
        m_ref[blk] = jnp.broadcast_to(m, (rows, LANES))
    km_ref[...] = km

    @pl.when(s == pl.num_programs(1) - 1)
    def _():
        gate = lax.dot_general(qf, km_ref[...], (((1,), (0,)), ((), ())), preferred_element_type=_F32,
                               precision=lax.Precision.HIGHEST)
        sel = _moba_pick(gate, n_blk, 1) + 1.0
        new = new_ref[0].astype(_F32)
        m0, l0, acc0 = _new_key_softmax(qs.astype(_F32), new[:, :width], new[:, width:], n_new, lambda j: j <= qi_r)
        m_all = jnp.full((rows, LANES), NEG, _F32)
        for j in range(n_blk):
            m_all = jnp.where(lane == j, m_ref[j], m_all)
        m_top = jnp.maximum(m0, jnp.max(jnp.where(sel > 0.0, m_all, NEG), axis=-1, keepdims=True))
        w = sel * jnp.exp(m_all - m_top)
        out = jnp.exp(m0 - m_top) * _seed_state(m0, l0, acc0)[1]
        for j in range(n_blk):
            out = out + w[:, j:j + 1] * part_ref[j]
        o_ref[0] = _unstack_heads(_finish1((m_top, out)), MOBA_HEADS).astype(o_ref.dtype)


def _moba_sample(q, new_bf, cache, layer, page_table, n_new):
    b, n_pages = page_table.shape
    page = cache.shape[3]
    g = PAGES_PER_STEP
    rows = MOBA_HEADS * SAMPLE_ROWS
    width = MOBA_HEADS * HEAD_DIM
    n_blk = n_pages * page // MOBA_BLOCK
    out_shape = (b, SAMPLE_ROWS, width)
    return _sample_call(
        functools.partial(_moba_sample_kernel, n_pages_step=g, n_new=n_new, past_len=n_pages * page),
        "moba_sample", page_table, n_pages // g,
        [q, new_bf], [_seq_block(q.shape), _seq_block(new_bf.shape)],
        [cache] * g, _page_specs(layer, g, (1, 1, 2 * width, page), 0, lambda s, gp: s * g + gp),
        _seq_block(out_shape), jax.ShapeDtypeStruct(out_shape, MXU_DTYPE),
        [pltpu.VMEM((n_blk, rows, LANES), _F32), pltpu.VMEM((n_blk, rows, width + LANES), _F32),
         pltpu.VMEM((width, LANES), _F32)])


def _rope_tables(pos):
    half = HEAD_DIM // 2
    inv = ROPE_THETA ** (-jnp.arange(half, dtype=_F32) / half)
    ang = pos.astype(_F32)[:, None] * inv[None, :]
    cos, sin = jnp.cos(ang), jnp.sin(ang)
    zero = jnp.zeros_like(sin)
    reps = LANES // HEAD_DIM
    return (jnp.tile(cos, (1, 2 * reps)), jnp.tile(jnp.concatenate([-sin, zero], axis=1), (1, reps)),
            jnp.tile(jnp.concatenate([zero, sin], axis=1), (1, reps)))


def _pack_w_in(w_in):
    d = w_in.shape[0]
    o1 = NSA_Q_COLS
    o2 = o1 + NSA_KV_COLS
    o3 = o2 + NSA_G_COLS
    o4 = o3 + SB_COLS
    q, kv, g, sb, mb = w_in[:, :o1], w_in[:, o1:o2], w_in[:, o2:o3], w_in[:, o3:o4], w_in[:, o4:]
    zero = jnp.zeros((d, HEAD_DIM), w_in.dtype)
    q_groups = []
    for hh in range(NSA_HEADS):
        w = q[:, hh * HEAD_DIM:(hh + 1) * HEAD_DIM]
        q_groups += [w, zero] if hh // NSA_GROUP == 0 else [zero, w]
    gate = jnp.pad(g, ((0, 0), (0, LANES - NSA_G_COLS)))
    return jnp.concatenate(q_groups + [kv, sb, mb, gate], axis=1).astype(MXU_DTYPE)


def _block_tables(t):
    key = jnp.arange(t)[:, None]
    j = jnp.arange(LANES)[None, :]
    eblk = jnp.where(key // SEL_BLOCK == j, -NEG, 0.0).astype(MXU_DTYPE)
    eblk_mb = jnp.where(key // MOBA_BLOCK == j, -NEG, 0.0).astype(MXU_DTYPE)
    i = jnp.arange(t // CMP_STRIDE)[:, None]
    ratio = SEL_BLOCK // CMP_STRIDE
    cover = ((i <= ratio * j + ratio - 1) & (i >= ratio * j - (CMP_LEN // CMP_STRIDE - 1))).astype(MXU_DTYPE)
    return eblk, eblk_mb, cover


def _layer_weights(l, norm_g, ffn_w_gate, ffn_w_up, ffn_w_down, w_in, nsa_cmp_pe, nsa_cmp_w1, nsa_cmp_w2,
                   w_branch_nsa, w_branch_sb, w_branch_moba, w_merge_gate, w_out, w_ple_proj, w_ple_gate):
    c = lambda w: w.astype(MXU_DTYPE)
    d, f = ffn_w_gate.shape[2:]
    tf = _ff_tile(f)
    cols = lambda w: c(w).reshape(d, f // tf, tf).transpose(1, 0, 2)
    return dict(
        g=[norm_g[l, i][None, :] for i in range(norm_g.shape[1])],
        ffn=[(cols(ffn_w_gate[l, i]), cols(ffn_w_up[l, i]), c(ffn_w_down[l, i]).reshape(f // tf, tf, d))
             for i in range(2)],
        w_all=_pack_w_in(w_in[l]),
        cw=_compress_weights(nsa_cmp_pe[l], nsa_cmp_w1[l], nsa_cmp_w2[l]),
        merge=(c(w_branch_nsa[l]), c(w_branch_sb[l]), c(w_branch_moba[l]), c(w_merge_gate[l]), c(w_out[l])),
        ple=(c(w_ple_gate[l]), c(w_ple_proj[l])),
    )


def _token_tile(n, cap):
    tm = cap
    while n % tm:
        tm //= 2
    return tm


def _ff_tile(f, cap=512):
    best = LANES
    for k in range(1, f // LANES + 1):
        if f % (k * LANES) == 0 and k * LANES <= cap:
            best = k * LANES
    return best


def _prompt_layer(h, p_l, lw, tabs, consts, b, t, depth, layer, carried):
    n, d = h.shape
    tm = _token_tile(t, 512)
    tm_ffn = _token_tile(t, 1024)
    tk = 8 * LANES
    g = lw["g"]
    h = _ffn(h, g[0], g[1], *lw["ffn"][0], tm=tm_ffn)
    per_seq = t // tm
    pr = _inproj(h, g[2], lw["w_all"], tabs, lambda i: (i % per_seq, 0), tm=tm, stack=(depth, layer, b, t),
                 carried=carried)
    stacked = tuple(pr.pop(k) for k in _INPROJ_STATE)
    kmean = pr.pop("kmean").reshape(b, per_seq, SUBLANES, -1)[:, :, :tm // MOBA_BLOCK].reshape(b, t // MOBA_BLOCK, -1)
    kmean = jnp.pad(kmean, ((0, 0), (0, LANES - kmean.shape[1]), (0, 0)))
    pr = {k: v.reshape(b, t, v.shape[1]) for k, v in pr.items()}
    eblk, eblk_mb, cover = consts
    cmp_k, cmp_v = _compress_prompt(pr["cmp_rows"], lw["cw"])
    o_nsa = _nsa_prompt(pr["q_nsa"], pr["gates"], cmp_k, cmp_v, pr["nsa_bf"], pr["win_bf"], eblk, cover,
                        tq=2 * LANES, tk=tk)
    o_sb = _sb_prompt(pr["q_sb"], pr["sb_bf"], tq=2 * LANES, tk=LANES)
    o_mb = _moba_prompt(pr["q_mb"], kmean, pr["mb_bf"], eblk_mb, tk=tk)
    flat = lambda o: o.reshape(n, o.shape[2])
    h = _merge(h, g[2], g[3], flat(o_nsa), flat(o_sb), flat(o_mb), *lw["merge"], tm=tm)
    h = _ffn(h, g[4], g[5], *lw["ffn"][1], ple=(p_l, *lw["ple"], g[6]), tm=tm_ffn)
    return h, stacked, pr["win"]


def _sample_layer(h, p_l, lw, tabs, cover, caches, layer, page_table, n_new):
    n, d = h.shape
    b = page_table.shape[0]
    tm = _token_tile(n, 512)
    g = lw["g"]
    cache_nsa, cache_sb, cache_mb, cache_win = caches
    h = _ffn(h, g[0], g[1], *lw["ffn"][0], tm=tm)
    pr = _inproj(h, g[2], lw["w_all"], tabs, lambda i: (i, 0), tm=tm)
    pr = {k: v.reshape(b, SAMPLE_ROWS, v.shape[1]) for k, v in pr.items()}
    cmp_k, cmp_v = _compress_sample(cache_nsa, layer, page_table, lw["cw"])
    o_nsa = _nsa_sample(pr["q_nsa"], pr["gates"], cmp_k, cmp_v, pr["nsa_bf"], pr["win_bf"], cache_nsa, cache_win,
                        layer, page_table, cover, n_new)
    o_sb = _sb_sample(pr["q_sb"], pr["sb_bf"], cache_sb, layer, page_table, n_new)
    o_mb = _moba_sample(pr["q_mb"], pr["mb_bf"], cache_mb, layer, page_table, n_new)
    flat = lambda o: o.reshape(n, o.shape[2])
    h = _merge(h, g[2], g[3], flat(o_nsa), flat(o_sb), flat(o_mb), *lw["merge"], tm=tm)
    h = _ffn(h, g[4], g[5], *lw["ffn"][1], ple=(p_l, *lw["ple"], g[6]), tm=tm)
    return h, tuple(pr[k][:, :n_new] for k in ("nsa", "sb", "mb", "win"))


def kernel(x_prompt, x_sample, p_prompt, p_sample, cache_nsa, cache_sb, cache_moba, cache_win, page_table,
           norm_g, ffn_w_gate, ffn_w_up, ffn_w_down, w_in, nsa_cmp_pe, nsa_cmp_w1, nsa_cmp_w2,
           w_branch_nsa, w_branch_sb, w_branch_moba, w_merge_gate, w_out, w_ple_proj, w_ple_gate):
    depth = norm_g.shape[0]
    b, t, d = x_prompt.shape
    bs, ts, _ = x_sample.shape
    n_pages = page_table.shape[1]
    page = cache_nsa.shape[2]
    past_len = n_pages * page
    assert ts <= SAMPLE_ROWS and t % (8 * LANES) == 0 and t >= WINDOW + LANES
    assert t // SEL_BLOCK <= LANES and past_len // SEL_BLOCK <= LANES
    assert n_pages % PAGES_PER_STEP == 0 and MOBA_BLOCK % page == 0 and cache_win.shape[2] == WINDOW
    weights = (norm_g, ffn_w_gate, ffn_w_up, ffn_w_down, w_in, nsa_cmp_pe, nsa_cmp_w1, nsa_cmp_w2,
               w_branch_nsa, w_branch_sb, w_branch_moba, w_merge_gate, w_out, w_ple_proj, w_ple_gate)
    tabs_p = _rope_tables(jnp.arange(t, dtype=jnp.int32))
    tabs_s = _rope_tables(past_len + jnp.arange(bs * SAMPLE_ROWS, dtype=jnp.int32) % SAMPLE_ROWS)
    consts = _block_tables(t)
    cover_s = _block_tables(past_len)[2]
    pad_rows = lambda x: jnp.pad(x, ((0, 0), (0, SAMPLE_ROWS - ts), (0, 0))).reshape(bs * SAMPLE_ROWS, -1)
    feature_major = lambda c: jnp.transpose(c, (0, 1, 3, 4, 5, 2)).reshape(c.shape[0], c.shape[1], -1, c.shape[2])
    caches = tuple(feature_major(c) for c in (cache_nsa, cache_sb, cache_moba, cache_win))
    h_p = x_prompt.reshape(b * t, d)
    h_s = pad_rows(x_sample)
    stacked_p, win_p, st_s = (), [], []
    keep = min(WINDOW, t)
    for l in range(depth):
        lw = _layer_weights(l, *weights)
        h_p, stacked_p, win = _prompt_layer(h_p, p_prompt[l].reshape(b * t, -1), lw, tabs_p, consts, b, t, depth, l,
                                            stacked_p)
        win_p.append(win[:, t - keep:])
        h_s, rows = _sample_layer(h_s, pad_rows(p_sample[l]), lw, tabs_s, cover_s, caches, l, page_table, ts)
        win_fm = jnp.concatenate([caches[3][l][:, :, ts:], jnp.swapaxes(rows[3], 1, 2)], axis=2)
        st_s.append(rows[:3] + (jnp.swapaxes(win_fm, 1, 2),))
    heads = ((4, NSA_KV_HEADS), (2, SB_HEADS), (2, MOBA_HEADS))
    out = [h_p.reshape(b, t, d), h_s.reshape(bs, SAMPLE_ROWS, d)[:, :ts]]
    for i, (parts, nh) in enumerate(heads):
        out.append(jnp.swapaxes(stacked_p[i], 2, 3).reshape(depth, b, t, parts, nh, HEAD_DIM))
        out.append(jnp.stack([s[i] for s in st_s]).reshape(depth, bs, ts, parts, nh, HEAD_DIM))
    out.append(jnp.stack(win_p).reshape(depth, b, keep, 2, NSA_KV_HEADS, HEAD_DIM))
    out.append(jnp.stack([s[3] for s in st_s]).reshape(depth, bs, -1, 2, NSA_KV_HEADS, HEAD_DIM))
    return tuple(out)
```

```python
import functools

import jax
import jax.numpy as jnp
from jax import lax
from jax.experimental import pallas as pl
from jax.experimental.pallas import tpu as pltpu

HEAD_DIM = 64
NSA_HEADS = 8
NSA_KV_HEADS = 2
NSA_GROUP = NSA_HEADS // NSA_KV_HEADS
SB_HEADS = 4
MOBA_HEADS = 4
CMP_LEN = 32
CMP_STRIDE = 16
SEL_BLOCK = 64
SEL_TOPN = 16
WINDOW = 512
MOBA_BLOCK = 256
MOBA_TOPK = 3
ROPE_THETA = 10000.0
RMS_EPS = 1e-6
NEG = -1e30
FORCE_BONUS = 1e4
SB_DEAD = -104.0

LANES = 128
SUBLANES = 8
MXU_DTYPE = jnp.bfloat16
VMEM_LIMIT = 56 * 1024 * 1024
SAMPLE_ROWS = 8
PAGES_PER_STEP = 32
NSA_ROW_CHUNKS = 4

NSA_Q_COLS = NSA_HEADS * HEAD_DIM
NSA_KV_COLS = 6 * NSA_KV_HEADS * HEAD_DIM
NSA_G_COLS = 3 * NSA_HEADS
SB_COLS = 3 * SB_HEADS * HEAD_DIM
MOBA_COLS = 3 * MOBA_HEADS * HEAD_DIM

_F32 = jnp.float32


def _dot(a, b):
    return jnp.dot(a, b, preferred_element_type=_F32)


def _dot_nt(a, b):
    return lax.dot_general(a, b, (((1,), (1,)), ((), ())), preferred_element_type=_F32)


def _dot_split(x, w):
    hi = x.astype(MXU_DTYPE)
    mid = (x - hi.astype(_F32)).astype(MXU_DTYPE)
    return _dot(hi, w) + _dot(mid, w)


def _rms(x, g):
    return x * lax.rsqrt(jnp.mean(x * x, axis=-1, keepdims=True) + RMS_EPS) * g


def _params(*sem):
    return pltpu.CompilerParams(dimension_semantics=sem, vmem_limit_bytes=VMEM_LIMIT)


def _full(shape):
    n = len(shape)
    return pl.BlockSpec(shape, lambda *_: (0,) * n)


def _ffn_kernel(*refs, with_ple):
    if with_ple:
        (h_ref, gpre_ref, gpost_ref, wg_ref, wu_ref, wd_ref,
         p_ref, wpg_ref, wpp_ref, gple_ref, o_ref, xn_ref, acc_ref) = refs
    else:
        h_ref, gpre_ref, gpost_ref, wg_ref, wu_ref, wd_ref, o_ref, xn_ref, acc_ref = refs
    xn_ref[...] = _rms(h_ref[...], gpre_ref[...]).astype(MXU_DTYPE)
    acc_ref[...] = jnp.zeros_like(acc_ref)

    def chunk(j, carry):
        xn = xn_ref[...]
        g = _dot(xn, wg_ref[j])
        u = _dot(xn, wu_ref[j])
        a = (g * jax.nn.sigmoid(g)) * u
        acc_ref[...] += _dot(a.astype(MXU_DTYPE), wd_ref[j])
        return carry

    lax.fori_loop(0, wg_ref.shape[0], chunk, 0)
    h = h_ref[...] + 0.5 * _rms(acc_ref[...], gpost_ref[...])
    if with_ple:
        gate = jax.nn.sigmoid(_dot(h.astype(MXU_DTYPE), wpg_ref[...]))
        ple = gate * _dot(p_ref[...].astype(MXU_DTYPE), wpp_ref[...])
        h = h + _rms(ple, gple_ref[...])
    o_ref[...] = h


def _resident(shape):
    n = len(shape)
    return pl.BlockSpec(shape, lambda *_: (0,) * n, pipeline_mode=pl.Buffered(1))


def _ffn(h, g_pre, g_post, wg, wu, wd, ple=None, *, tm):
    n, d = h.shape
    row = lambda i: (i, 0)
    in_specs = [pl.BlockSpec((tm, d), row), _full((1, d)), _full((1, d)),
                _resident(wg.shape), _resident(wu.shape), _resident(wd.shape)]
    args = [h, g_pre, g_post, wg, wu, wd]
    if ple is not None:
        p, wpg, wpp, g_ple = ple
        in_specs += [pl.BlockSpec((tm, p.shape[1]), row), _resident(wpg.shape), _resident(wpp.shape), _full((1, d))]
        args += [p, wpg, wpp, g_ple]
    return pl.pallas_call(
        functools.partial(_ffn_kernel, with_ple=ple is not None),
        grid=(n // tm,),
        in_specs=in_specs,
        out_specs=pl.BlockSpec((tm, d), row),
        out_shape=jax.ShapeDtypeStruct((n, d), _F32),
        scratch_shapes=[pltpu.VMEM((tm, d), MXU_DTYPE), pltpu.VMEM((tm, d), _F32)],
        compiler_params=_params("parallel"),
        name="ffn_ple" if ple is not None else "ffn",
    )(*args)


_G_QNSA = 0
_G_KV = 8
_G_SB = 14
_G_MB = 20
_G_GATE = 26
_ROPE_GROUPS = tuple(range(8)) + (8, 10, 12, 20, 21, 22, 23)
_INPROJ_OUT = (("q_nsa", _G_QNSA, 8, True, None), ("nsa", _G_KV, 4, False, _F32), ("win", _G_KV + 4, 2, False, _F32),
               ("q_sb", _G_SB, 2, True, None), ("sb", _G_SB + 2, 4, False, _F32),
               ("q_mb", _G_MB, 2, False, _F32), ("mb", _G_MB + 2, 4, False, _F32))
_INPROJ_NARROW = ("nsa", "win", "sb", "mb")


_INPROJ_STATE = ("nsa", "sb", "mb")


def _inproj_kernel(h_ref, g_ref, w_ref, cos_ref, sa_ref, sb_ref, *refs, n_carried, feature_major):
    out_refs = refs[n_carried:]
    n_stacked = feature_major
    xn = _rms(h_ref[...], g_ref[...]).astype(MXU_DTYPE)
    cos, sa, sb = cos_ref[...], sa_ref[...], sb_ref[...]
    scale = HEAD_DIM ** -0.5
    tm = xn.shape[0]

    pairs = {}

    def group(gi):
        if gi not in pairs:
            g0 = gi - gi % 2
            wide = _dot(xn, w_ref[:, g0 * LANES:min(g0 + 2, _G_GATE + 1) * LANES])
            for k in range(wide.shape[1] // LANES):
                pairs[g0 + k] = wide[:, k * LANES:(k + 1) * LANES]
        y = pairs.pop(gi)
        if gi in _ROPE_GROUPS:
            y = y * cos + pltpu.roll(y, LANES - HEAD_DIM // 2, 1) * sa + pltpu.roll(y, HEAD_DIM // 2, 1) * sb
        return y

    outs = out_refs[:len(_INPROJ_OUT)]
    narrow = dict(zip(_INPROJ_NARROW, out_refs[len(_INPROJ_OUT):]))
    gate_ref = out_refs[len(_INPROJ_OUT) + len(_INPROJ_NARROW)]
    means = []
    for (name, g0, n_groups, scaled, _), o_ref in zip(_INPROJ_OUT, outs):
        for i in range(n_groups):
            y = group(g0 + i)
            if scaled:
                y = y * scale
            if n_stacked and name in _INPROJ_STATE:
                o_ref[0, 0, i * LANES:(i + 1) * LANES, :] = y.T
            else:
                o_ref[:, i * LANES:(i + 1) * LANES] = y.astype(o_ref.dtype)
            if name in narrow:
                narrow[name][:, i * LANES:(i + 1) * LANES] = y.astype(MXU_DTYPE)
            if n_stacked and name == "nsa" and i < 2:
                out_refs[-2][:, i * LANES:(i + 1) * LANES] = y
            if n_stacked and name == "mb" and i < 2:
                means.append([jnp.mean(y[r:r + MOBA_BLOCK], axis=0, keepdims=True) for r in range(0, tm, MOBA_BLOCK)])
    gate_ref[...] = jax.nn.sigmoid(group(_G_GATE))
    if n_stacked:
        rows = [jnp.concatenate([means[0][r], means[1][r]], axis=1) for r in range(tm // MOBA_BLOCK)]
        rows.append(jnp.zeros((SUBLANES - len(rows), 2 * LANES), _F32))
        out_refs[-1][...] = jnp.concatenate(rows, axis=0)


def _inproj(h, g, w_all, tabs, tab_map, *, tm, stack=None, carried=()):
    n, d = h.shape
    row = lambda i: (i, 0)
    tab_spec = pl.BlockSpec((tm, LANES), tab_map)
    sizes = {o[0]: o[2] * LANES for o in _INPROJ_OUT}
    names = [o[0] for o in _INPROJ_OUT] + [k + "_bf" for k in _INPROJ_NARROW] + ["gates"]
    shapes = [jax.ShapeDtypeStruct((n, o[2] * LANES), o[4] or MXU_DTYPE) for o in _INPROJ_OUT]
    shapes += [jax.ShapeDtypeStruct((n, sizes[k]), MXU_DTYPE) for k in _INPROJ_NARROW]
    shapes.append(jax.ShapeDtypeStruct((n, LANES), _F32))
    specs = [pl.BlockSpec((tm, s.shape[1]), row) for s in shapes]
    aliases = {}
    n_fixed_inputs = 6
    if stack is not None:
        depth, layer, b, t = stack
        per_seq = t // tm
        assert tm % MOBA_BLOCK == 0 and tm // MOBA_BLOCK <= SUBLANES and len(carried) in (0, len(_INPROJ_STATE))
        for k, name in enumerate(_INPROJ_STATE):
            idx = names.index(name)
            shapes[idx] = jax.ShapeDtypeStruct((depth, b, sizes[name], t), _F32)
            specs[idx] = pl.BlockSpec((1, 1, sizes[name], tm), lambda i: (layer, i // per_seq, 0, i % per_seq))
            if carried:
                aliases[n_fixed_inputs + k] = idx
        names += ["cmp_rows", "kmean"]
        shapes += [jax.ShapeDtypeStruct((n, 2 * LANES), _F32),
                   jax.ShapeDtypeStruct((n // tm * SUBLANES, 2 * LANES), _F32)]
        specs += [pl.BlockSpec((tm, 2 * LANES), row), pl.BlockSpec((SUBLANES, 2 * LANES), row)]
    outs = pl.pallas_call(
        functools.partial(_inproj_kernel, n_carried=len(carried), feature_major=stack is not None),
        grid=(n // tm,),
        in_specs=[pl.BlockSpec((tm, d), row), _full((1, d)), _full(w_all.shape), tab_spec, tab_spec, tab_spec]
        + [pl.BlockSpec(memory_space=pl.ANY)] * len(carried),
        out_specs=specs,
        out_shape=shapes,
        input_output_aliases=aliases,
        compiler_params=_params("parallel"),
        name="inproj",
    )(h, g, w_all, *tabs, *carried)
    return dict(zip(names, outs))


def _merge_kernel(h_ref, gpre_ref, gpost_ref, onsa_ref, osb_ref, omb_ref,
                  wbn_ref, wbs_ref, wbm_ref, wgate_ref, wout_ref, o_ref):
    d = h_ref.shape[1]
    h = h_ref[...]
    u = _rms(h, gpre_ref[...]).astype(MXU_DTYPE)
    branches = (_dot(onsa_ref[...], wbn_ref[...]), _dot(osb_ref[...], wbs_ref[...]), _dot(omb_ref[...], wbm_ref[...]))
    mix = None
    for c, br in enumerate(branches):
        term = jax.nn.sigmoid(_dot(u, wgate_ref[:, c * d:(c + 1) * d])) * br
        mix = term if mix is None else mix + term
    y = _dot(mix.astype(MXU_DTYPE), wout_ref[...])
    o_ref[...] = h + _rms(y, gpost_ref[...])


def _merge(h, g_pre, g_post, o_nsa, o_sb, o_mb, wbn, wbs, wbm, wgate, wout, *, tm):
    n, d = h.shape
    row = lambda i: (i, 0)
    return pl.pallas_call(
        _merge_kernel,
        grid=(n // tm,),
        in_specs=[pl.BlockSpec((tm, d), row), _full((1, d)), _full((1, d)),
                  pl.BlockSpec((tm, o_nsa.shape[1]), row), pl.BlockSpec((tm, o_sb.shape[1]), row),
                  pl.BlockSpec((tm, o_mb.shape[1]), row),
                  _full(wbn.shape), _full(wbs.shape), _full(wbm.shape), _full(wgate.shape), _full(wout.shape)],
        out_specs=pl.BlockSpec((tm, d), row),
        out_shape=jax.ShapeDtypeStruct((n, d), _F32),
        compiler_params=_params("parallel"),
        name="merge",
    )(h, g_pre, g_post, o_nsa, o_sb, o_mb, wbn, wbs, wbm, wgate, wout)


def _lane_lo(shape=(1, LANES)):
    return (lax.broadcasted_iota(jnp.int32, shape, len(shape) - 1) % LANES) < HEAD_DIM


def _softplus(z):
    return jnp.maximum(z, 0.0) + jnp.log1p(jnp.exp(-jnp.abs(z)))


def _cumsum_matrix(tk):
    j = lax.broadcasted_iota(jnp.int32, (tk, 2 * tk), 0)
    s = lax.broadcasted_iota(jnp.int32, (tk, 2 * tk), 1)
    return jnp.where((j > s) | (s >= tk), 1.0, 0.0).astype(MXU_DTYPE)


def _online_update1(state, z, v1, nt=False):
    m, acc = state
    m_new = jnp.maximum(m, jnp.max(z, axis=-1, keepdims=True))
    p = jnp.exp(z - m_new).astype(MXU_DTYPE)
    return m_new, jnp.exp(m - m_new) * acc + (_dot_nt(p, v1) if nt else _dot(p, v1))


def _online_init1(rows, width=LANES):
    return jnp.full((rows, 1), NEG, _F32), jnp.zeros((rows, width + LANES), _F32)


def _finish1(state):
    _, acc = state
    width = acc.shape[1] - LANES
    den = 1.0 / acc[:, width:]
    return acc[:, :width] * jnp.concatenate([den] * (width // LANES), axis=1)


def _with_ones(v, axis=1):
    shape = (v.shape[0], LANES) if axis == 1 else (LANES, v.shape[1])
    return jnp.concatenate([v, jnp.ones(shape, v.dtype)], axis=axis)


def _seed_state(m, l, acc):
    return m, jnp.concatenate([acc, jnp.broadcast_to(l, (l.shape[0], LANES))], axis=1)


def _topk_mask(val, k, axis):
    idx = lax.broadcasted_iota(jnp.int32, val.shape, axis).astype(_F32)
    sel = jnp.zeros(val.shape, _F32)
    for _ in range(k):
        m = jnp.max(val, axis=axis, keepdims=True)
        first = jnp.min(jnp.where(val == m, idx, 1e9), axis=axis, keepdims=True)
        hit = idx == first
        sel = jnp.where(hit, 1.0, sel)
        val = jnp.where(hit, -jnp.inf, val)
    return sel


def _stack_nsa_queries(q):
    return jnp.concatenate([q[:, hh * LANES:(hh + 1) * LANES] for hh in range(NSA_HEADS)], axis=0)


def _cmp_branch(qs, kc, vc, qpos, tq):
    nc = kc.shape[0]
    s = _dot_nt(qs, kc)
    cend = lax.broadcasted_iota(jnp.int32, (1, nc), 1) * CMP_STRIDE + (CMP_LEN - 1)
    vis = cend <= qpos
    sm = jnp.where(vis, s, NEG)
    m = jnp.max(sm, axis=-1, keepdims=True)
    e = jnp.where(vis, jnp.exp(sm - m), 0.0)
    l = jnp.sum(e, axis=-1, keepdims=True)
    p = e * (1.0 / jnp.where(l > 0.0, l, 1.0))
    o_cmp = _dot(p.astype(MXU_DTYPE), vc)
    psums = []
    for kv in range(NSA_KV_HEADS):
        blocks = [p[(kv * NSA_GROUP + g) * tq:(kv * NSA_GROUP + g + 1) * tq] for g in range(NSA_GROUP)]
        psums.append(functools.reduce(lambda a, b: a + b, blocks))
    return o_cmp, psums


def _nsa_select(imp, qp, n_blocks, n_top, axis):
    jblk = lax.broadcasted_iota(jnp.int32, imp.shape, axis)
    cur = qp // SEL_BLOCK
    forced = (jblk == 0) | (jblk == cur) | (jblk == cur - 1)
    valid = (jblk * SEL_BLOCK <= qp) & (jblk < n_blocks)
    val = jnp.where(valid, imp + jnp.where(forced, FORCE_BONUS, 0.0), NEG)
    return jnp.where(valid, _topk_mask(val, n_top, axis), 0.0) - 1.0


def _nsa_lhs(qs, biases):
    bias = jnp.concatenate([b for b in biases for _ in range(NSA_GROUP)], axis=0)
    return jnp.concatenate([qs, bias.astype(qs.dtype)], axis=1)


def _combine_nsa(gates, o_cmp, o_sel, o_win, tq):
    lo = _lane_lo()
    heads = []
    for hh in range(NSA_HEADS):
        r = slice(hh * tq, (hh + 1) * tq)
        o = (gates[:, 3 * hh:3 * hh + 1] * o_cmp[r] + gates[:, 3 * hh + 1:3 * hh + 2] * o_sel[r]
             + gates[:, 3 * hh + 2:3 * hh + 3] * o_win[r])
        if hh % 2 != hh // NSA_GROUP:
            o = pltpu.roll(o, HEAD_DIM, 1)
        heads.append(o)
    return jnp.concatenate([jnp.where(lo, heads[2 * i], heads[2 * i + 1]) for i in range(NSA_HEADS // 2)], axis=1)


def _moba_pick(s, own, axis):
    past = lax.broadcasted_iota(jnp.int32, s.shape, axis) < own
    return jnp.where(past, _topk_mask(jnp.where(past, s, NEG), MOBA_TOPK, axis), 0.0) - 1.0


def _sb_block(qh, kt, vt, cum, carry, keep):
    c, o = carry
    tk = kt.shape[0]
    z = _dot_nt(qh, kt)
    lk = -_softplus(z)
    if keep is not None:
        lk = jnp.where(keep, lk, 0.0)
    r = _dot_split(lk, cum)
    a = jnp.exp(lk + z + r[:, :tk] + c)
    if keep is not None:
        a = jnp.where(keep, a, 0.0)
    return c + r[:, tk:], o + _dot(a.astype(MXU_DTYPE), vt)


def _stage_chunks(src_ref, lead, n_chunks, dst_ref, c0):
    for r in range(CMP_STRIDE):
        x = src_ref[lead + (pl.ds(r, n_chunks, stride=CMP_STRIDE), slice(None))]
        dst_ref[pl.ds(c0, n_chunks), r * LANES:(r + 1) * LANES] = x


def _compress_finish(xk_ref, xv_ref, wk_ref, wv_ref, pe_ref, w2_ref, ok_ref, ov_ref):
    nc = xk_ref.shape[0]
    for x_ref, w_ref, t, o_ref in ((xk_ref, wk_ref, 0, ok_ref), (xv_ref, wv_ref, 1, ov_ref)):
        ab = _dot(x_ref[...].astype(MXU_DTYPE), w_ref[...])
        cst = _dot(pe_ref[t].astype(MXU_DTYPE), w_ref[...])
        for h in range(NSA_KV_HEADS):
            a = ab[:, h * 256:h * 256 + LANES]
            b = ab[:, h * 256 + LANES:(h + 1) * 256]
            c = cst[0:1, h * 256:h * 256 + LANES] + cst[1:2, h * 256 + LANES:(h + 1) * 256]
            hid = jax.nn.gelu(a + pltpu.roll(b, nc - 1, 0) + c)
            part = _dot(hid.astype(MXU_DTYPE), w2_ref[t, h])
            out = part if h == 0 else out + part
        o_ref[0] = out.astype(o_ref.dtype)


def _compress_prompt_kernel(rk_ref, rv_ref, wk_ref, wv_ref, pe_ref, w2_ref, ok_ref, ov_ref, xk_ref, xv_ref):
    n_chunks = rk_ref.shape[1] // CMP_STRIDE
    _stage_chunks(rk_ref, (0,), n_chunks, xk_ref, 0)
    _stage_chunks(rv_ref, (0,), n_chunks, xv_ref, 0)
    _compress_finish(xk_ref, xv_ref, wk_ref, wv_ref, pe_ref, w2_ref, ok_ref, ov_ref)


def _compress_sample_kernel(pt_ref, wk_ref, wv_ref, pe_ref, w2_ref, *refs, n_pages_step):
    del pt_ref
    pages = refs[:n_pages_step]
    ok_ref, ov_ref, xk_ref, xv_ref, rows_ref = refs[n_pages_step:]
    s = pl.program_id(1)
    per_page = pages[0].shape[3] // CMP_STRIDE
    for gp, page in enumerate(pages):
        x = page[0, 0]
        rows_ref[0] = x[:LANES].T
        rows_ref[1] = x[LANES:].T
        c0 = pl.multiple_of((s * n_pages_step + gp) * per_page, per_page)
        _stage_chunks(rows_ref, (0,), per_page, xk_ref, c0)
        _stage_chunks(rows_ref, (1,), per_page, xv_ref, c0)

    @pl.when(s == pl.num_programs(1) - 1)
    def _():
        _compress_finish(xk_ref, xv_ref, wk_ref, wv_ref, pe_ref, w2_ref, ok_ref, ov_ref)


def _compress_weights(pe, w1, w2):
    eye = jnp.eye(NSA_KV_HEADS, dtype=w1.dtype)
    big, pes = [], []
    for t in range(2):
        w1r = w1[t].reshape(2, CMP_STRIDE, HEAD_DIM, w1.shape[-1])
        wb = jnp.einsum("hH,ardn->rhdHan", eye, w1r)
        big.append(wb.reshape(CMP_STRIDE * NSA_KV_HEADS * HEAD_DIM, -1).astype(MXU_DTYPE))
        per = pe[t].reshape(2, CMP_STRIDE, 1, HEAD_DIM)
        rows = jnp.broadcast_to(per, (2, CMP_STRIDE, NSA_KV_HEADS, HEAD_DIM)).reshape(2, -1)
        pes.append(jnp.concatenate([rows, jnp.zeros((SUBLANES - 2, rows.shape[1]), rows.dtype)], axis=0))
    zero = jnp.zeros_like(w2)
    w2p = jnp.stack([jnp.concatenate([w2, zero], axis=-1), jnp.concatenate([zero, w2], axis=-1)], axis=1)
    return big[0], big[1], jnp.stack(pes), w2p.astype(MXU_DTYPE)


def _compress_out(b, nc):
    shape = jax.ShapeDtypeStruct((b, nc, LANES), MXU_DTYPE)
    return [shape, shape]


def _compress_prompt(nsa_rows, cw):
    b, t, _ = nsa_rows.shape
    nc = t // CMP_STRIDE
    wk, wv, pe2, w2p = cw
    out_spec = pl.BlockSpec((1, nc, LANES), lambda i: (i, 0, 0))
    return pl.pallas_call(
        _compress_prompt_kernel,
        grid=(b,),
        in_specs=[pl.BlockSpec((1, t, LANES), lambda i: (i, 0, 0)),
                  pl.BlockSpec((1, t, LANES), lambda i: (i, 0, 1)),
                  _full(wk.shape), _full(wv.shape), _full(pe2.shape), _full(w2p.shape)],
        out_specs=[out_spec, out_spec],
        out_shape=_compress_out(b, nc),
        scratch_shapes=[pltpu.VMEM((nc, CMP_STRIDE * LANES), _F32)] * 2,
        compiler_params=_params("arbitrary"),
        name="nsa_compress_prompt",
    )(nsa_rows, nsa_rows, wk, wv, pe2, w2p)


def _nsa_prompt_kernel(q_ref, g_ref, ck_ref, cv_ref, sk_ref, sv_ref, wk_ref, wv_ref, eblk_ref, cover_ref,
                       o_ref, *, tq, tk, n_top):
    qi = pl.program_id(1)
    t0 = qi * tq
    rows = NSA_HEADS * tq
    n_blocks = sk_ref.shape[1] // SEL_BLOCK
    qs = _stack_nsa_queries(q_ref[0])
    qpos_r = t0 + lax.broadcasted_iota(jnp.int32, (rows, 1), 0) % tq

    o_cmp, psums = _cmp_branch(qs, ck_ref[0], cv_ref[0], qpos_r, tq)
    qp_t = t0 + lax.broadcasted_iota(jnp.int32, (1, tq), 1)
    biases = []
    for psum in psums:
        imp_t = _dot_split(psum, cover_ref[...]).T
        biases.append(_nsa_select(imp_t, qp_t, n_blocks, n_top, 0).T)
    lhs = _nsa_lhs(qs, biases)

    kcol = lax.broadcasted_iota(jnp.int32, (1, tk), 1)

    chunk = rows // NSA_ROW_CHUNKS
    lhs_c = [lhs[c * chunk:(c + 1) * chunk] for c in range(NSA_ROW_CHUNKS)]
    qpos_c = [qpos_r[c * chunk:(c + 1) * chunk] for c in range(NSA_ROW_CHUNKS)]

    def sel_tile(kt, states, causal):
        start = pl.multiple_of(kt * tk, tk)
        rhs = jnp.concatenate([sk_ref[0, pl.ds(start, tk), :], eblk_ref[pl.ds(start, tk), :]], axis=1)
        v1 = _with_ones(sv_ref[0, pl.ds(start, tk), :])
        out = []
        for c in range(NSA_ROW_CHUNKS):
            z = _dot_nt(lhs_c[c], rhs)
            if causal:
                z = jnp.where(start + kcol <= qpos_c[c], z, NEG)
            out.append(_online_update1(states[c], z, v1))
        return tuple(out)

    n_full = t0 // tk
    init = tuple(_online_init1(chunk) for _ in range(NSA_ROW_CHUNKS))
    states = lax.fori_loop(0, n_full, lambda kt, st: sel_tile(kt, st, False), init)
    o_sel = jnp.concatenate([_finish1(st) for st in sel_tile(n_full, states, True)], axis=0)

    n_band = WINDOW + tq
    ws = pl.multiple_of(jnp.maximum(t0 - WINDOW, 0), tq)
    wk, wv1 = wk_ref[0, pl.ds(ws, n_band), :], _with_ones(wv_ref[0, pl.ds(ws, n_band), :])
    kpos = ws + lax.broadcasted_iota(jnp.int32, (1, n_band), 1)
    o_win = []
    for c in range(NSA_ROW_CHUNKS):
        dist = qpos_c[c] - kpos
        z = jnp.where((dist >= 0) & (dist <= WINDOW), _dot_nt(lhs_c[c][:, :LANES], wk), NEG)
        o_win.append(_finish1(_online_update1(_online_init1(chunk), z, wv1)))
    o_win = jnp.concatenate(o_win, axis=0)

    o_ref[0] = _combine_nsa(g_ref[0], o_cmp, o_sel, o_win, tq).astype(o_ref.dtype)


def _nsa_prompt(q_nsa, gates, cmp_k, cmp_v, nsa_bf, win_bf, eblk, cover, *, tq, tk):
    b, t, _ = q_nsa.shape
    nc = cmp_k.shape[1]
    n_top = min(SEL_TOPN, t // SEL_BLOCK)
    cmp_spec = pl.BlockSpec((1, nc, LANES), lambda i, qi: (i, 0, 0))
    col = lambda g: pl.BlockSpec((1, t, LANES), lambda i, qi: (i, 0, g))
    return pl.pallas_call(
        functools.partial(_nsa_prompt_kernel, tq=tq, tk=tk, n_top=n_top),
        grid=(b, t // tq),
        in_specs=[pl.BlockSpec((1, tq, NSA_HEADS * LANES), lambda i, qi: (i, qi, 0)),
                  pl.BlockSpec((1, tq, LANES), lambda i, qi: (i, qi, 0)),
                  cmp_spec, cmp_spec, col(2), col(3), col(0), col(1),
                  _full(eblk.shape), _full(cover.shape)],
        out_specs=pl.BlockSpec((1, tq, NSA_Q_COLS), lambda i, qi: (i, qi, 0)),
        out_shape=jax.ShapeDtypeStruct((b, t, NSA_Q_COLS), MXU_DTYPE),
        compiler_params=_params("parallel", "arbitrary"),
        name="nsa_prompt",
    )(q_nsa, gates, cmp_k, cmp_v, nsa_bf, nsa_bf, win_bf, win_bf, eblk, cover)


def _sb_prompt_kernel(q_ref, k_ref, v_ref, o_ref, *, tq, tk):
    qi = pl.program_id(1)
    lo = _lane_lo()
    n_pair = q_ref.shape[2] // LANES
    qh = []
    for p in range(n_pair):
        q = q_ref[0, :, p * LANES:(p + 1) * LANES]
        zero_q = jnp.zeros_like(q)
        qh.append(jnp.concatenate([jnp.where(lo, q, zero_q), jnp.where(lo, zero_q, q)], axis=0))
    cum = _cumsum_matrix(tk)
    row = lax.broadcasted_iota(jnp.int32, (2 * tq, tk), 0) % tq
    col = lax.broadcasted_iota(jnp.int32, (2 * tq, tk), 1)
    per_tile = tq // tk

    def blocks(kb, carry, keep):
        start = pl.multiple_of(kb * tk, tk)
        out = []
        for p in range(n_pair):
            kt = k_ref[0, pl.ds(start, tk), p * LANES:(p + 1) * LANES]
            vt = v_ref[0, pl.ds(start, tk), p * LANES:(p + 1) * LANES]
            out.append(_sb_block(qh[p], kt, vt, cum, carry[p], keep))
        return tuple(out)

    def alive(carry):
        c_max = functools.reduce(jnp.maximum, [c for c, _ in carry])
        return (jnp.max(c_max) > SB_DEAD).astype(jnp.int32)

    zero = jnp.zeros((2 * tq, LANES), _F32)
    carry = ((zero, zero),) * n_pair
    for d in reversed(range(per_tile)):
        carry = blocks(qi * per_tile + d, carry, d * tk + col < row)

    def body(st):
        i, _, carry = st
        for d in range(per_tile):
            carry = blocks((qi - 1 - i) * per_tile + per_tile - 1 - d, carry, None)
        return i + 1, alive(carry), carry

    _, _, carry = lax.while_loop(lambda st: (st[0] < qi) & (st[1] > 0), body, (0, alive(carry), carry))
    o_ref[0] = jnp.concatenate([jnp.where(lo, carry[p][1][:tq], carry[p][1][tq:]) for p in range(n_pair)],
                               axis=1).astype(o_ref.dtype)


def _sb_prompt(q_sb, sb_bf, *, tq, tk):
    b, t, width = q_sb.shape
    return pl.pallas_call(
        functools.partial(_sb_prompt_kernel, tq=tq, tk=tk),
        grid=(b, t // tq),
        in_specs=[pl.BlockSpec((1, tq, width), lambda i, qi: (i, qi, 0)),
                  pl.BlockSpec((1, t, width), lambda i, qi: (i, 0, 0)),
                  pl.BlockSpec((1, t, width), lambda i, qi: (i, 0, 1))],
        out_specs=pl.BlockSpec((1, tq, width), lambda i, qi: (i, qi, 0)),
        out_shape=jax.ShapeDtypeStruct((b, t, width), MXU_DTYPE),
        compiler_params=_params("parallel", "arbitrary"),
        name="sb_prompt",
    )(q_sb, sb_bf, sb_bf)


def _moba_prompt_kernel(q_ref, km_ref, k_ref, v_ref, eblk_ref, o_ref, *, tq, tk):
    qi = pl.program_id(1)
    lo = _lane_lo()
    n_pair = q_ref.shape[2] // LANES
    scale = HEAD_DIM ** -0.5
    row = lax.broadcasted_iota(jnp.int32, (tq, tq), 0)
    col = lax.broadcasted_iota(jnp.int32, (tq, tq), 1)
    causal = jnp.concatenate([col <= row] * 2, axis=0)
    own0 = pl.multiple_of(qi * tq, tq)
    pair = lambda p: slice(p * LANES, (p + 1) * LANES)
    lhs, states = [], []
    for p in range(n_pair):
        q = q_ref[0, :, pair(p)]
        halves = []
        for hh in range(2):
            qf = jnp.where(lo if hh == 0 else jnp.logical_not(lo), q, 0.0)
            gate = lax.dot_general(qf, km_ref[0, :, pair(p)], (((1,), (1,)), ((), ())), preferred_element_type=_F32,
                                   precision=lax.Precision.HIGHEST)
            bias = _moba_pick(gate.T, qi, 0).T
            halves.append(jnp.concatenate([(qf * scale).astype(MXU_DTYPE), bias.astype(MXU_DTYPE)], axis=1))
        lhs.append(jnp.concatenate(halves, axis=0))
        z = jnp.where(causal, _dot_nt(lhs[p][:, :LANES], k_ref[0, pl.ds(own0, tq), pair(p)]), NEG)
        states.append(_online_update1(_online_init1(2 * tq), z, _with_ones(v_ref[0, pl.ds(own0, tq), pair(p)])))

    def tile(j, states):
        start = pl.multiple_of(j * tk, tk)
        eblk = eblk_ref[pl.ds(start, tk), :]
        out = []
        for p in range(n_pair):
            rhs = jnp.concatenate([k_ref[0, pl.ds(start, tk), pair(p)], eblk], axis=1)
            out.append(_online_update1(states[p], _dot_nt(lhs[p], rhs), _with_ones(v_ref[0, pl.ds(start, tk), pair(p)])))
        return tuple(out)

    states = lax.fori_loop(0, (qi * tq + tk - 1) // tk, tile, tuple(states))
    outs = [_finish1(st) for st in states]
    o_ref[0] = jnp.concatenate([jnp.where(lo, o[:tq], o[tq:]) for o in outs], axis=1).astype(o_ref.dtype)


def _moba_prompt(q_mb, kmean, mb_bf, eblk, *, tk):
    b, t, width = q_mb.shape
    tq = MOBA_BLOCK
    return pl.pallas_call(
        functools.partial(_moba_prompt_kernel, tq=tq, tk=tk),
        grid=(b, t // tq),
        in_specs=[pl.BlockSpec((1, tq, width), lambda i, qi: (i, qi, 0)),
                  pl.BlockSpec((1, LANES, width), lambda i, qi: (i, 0, 0)),
                  pl.BlockSpec((1, t, width), lambda i, qi: (i, 0, 0)),
                  pl.BlockSpec((1, t, width), lambda i, qi: (i, 0, 1)),
                  _full(eblk.shape)],
        out_specs=pl.BlockSpec((1, tq, width), lambda i, qi: (i, qi, 0)),
        out_shape=jax.ShapeDtypeStruct((b, t, width), MXU_DTYPE),
        compiler_params=_params("parallel", "arbitrary"),
        name="moba_prompt",
    )(q_mb, kmean, mb_bf, mb_bf, eblk)


def _row_scores(q, keys, n):
    return [jnp.sum(q * keys[j:j + 1, :], axis=-1, keepdims=True) for j in range(n)]


def _new_key_softmax(q, keys, vals, n, visible):
    zs = [jnp.where(visible(j), z, NEG) for j, z in enumerate(_row_scores(q, keys, n))]
    m = functools.reduce(jnp.maximum, zs)
    l = jnp.zeros_like(m)
    acc = jnp.zeros((q.shape[0], vals.shape[1]), _F32)
    for j, z in enumerate(zs):
        p = jnp.exp(z - m)
        l = l + p
        acc = acc + p * vals[j:j + 1, :]
    return m, l, acc


def _stack_heads(q, n_heads):
    head = lax.broadcasted_iota(jnp.int32, (1, q.shape[1]), 1) // HEAD_DIM
    return jnp.concatenate([jnp.where(head == h, q, jnp.zeros_like(q)) for h in range(n_heads)], axis=0)


def _unstack_heads(o, n_heads):
    rows = o.shape[0] // n_heads
    head = lax.broadcasted_iota(jnp.int32, (1, o.shape[1]), 1) // HEAD_DIM
    out = jnp.zeros((rows, o.shape[1]), _F32)
    for h in range(n_heads):
        out = jnp.where(head == h, o[h * rows:(h + 1) * rows], out)
    return out


def _side_by_side(pages, f0, f1):
    return jnp.concatenate([p[0, 0, f0:f1, :] for p in pages], axis=1).astype(MXU_DTYPE)


def _key_block_onehot(n_keys, first_key, block):
    key_blk = (first_key + lax.broadcasted_iota(jnp.int32, (LANES, n_keys), 1)) // block
    return jnp.where(lax.broadcasted_iota(jnp.int32, (LANES, n_keys), 0) == key_blk, -NEG, 0.0).astype(MXU_DTYPE)


def _state_refs_store(m_ref, acc_ref, state):
    m_ref[...] = jnp.broadcast_to(state[0], m_ref.shape)
    acc_ref[...] = state[1]


def _page_specs(layer, n, block, row_block, page_of):
    def spec(gp):
        return pl.BlockSpec(block, lambda b, s, pt: (layer, pt[b, page_of(s, gp)], row_block, 0))
    return [spec(gp) for gp in range(n)]


def _sample_call(kernel_fn, name, page_table, n_steps, pre_args, pre_specs, page_arrays, page_specs, out_block,
                 out_shape, scratch):
    grid_spec = pltpu.PrefetchScalarGridSpec(
        num_scalar_prefetch=1,
        grid=(page_table.shape[0], n_steps),
        in_specs=pre_specs + page_specs,
        out_specs=out_block,
        scratch_shapes=scratch,
    )
    return pl.pallas_call(kernel_fn, grid_spec=grid_spec, out_shape=out_shape,
                          compiler_params=_params("arbitrary", "arbitrary"), name=name,
                          )(page_table, *pre_args, *page_arrays)


def _seq_block(shape):
    zeros = (0,) * (len(shape) - 1)
    return pl.BlockSpec((1,) + tuple(shape[1:]), lambda b, s, pt: (b,) + zeros)


def _const_block(shape):
    zeros = (0,) * len(shape)
    return pl.BlockSpec(tuple(shape), lambda b, s, pt: zeros)


def _compress_sample(cache, layer, page_table, cw):
    b, n_pages = page_table.shape
    page = cache.shape[3]
    nc = n_pages * page // CMP_STRIDE
    g = PAGES_PER_STEP
    wk, wv, pe2, w2p = cw
    out_spec = pl.BlockSpec((1, nc, LANES), lambda i, s, pt: (i, 0, 0))
    return _sample_call(
        functools.partial(_compress_sample_kernel, n_pages_step=g), "nsa_compress_sample", page_table, n_pages // g,
        [wk, wv, pe2, w2p], [_const_block(wk.shape), _const_block(wv.shape), _const_block(pe2.shape),
                             _const_block(w2p.shape)],
        [cache] * g, _page_specs(layer, g, (1, 1, 2 * LANES, page), 0, lambda s, gp: s * g + gp),
        [out_spec, out_spec], _compress_out(b, nc),
        [pltpu.VMEM((nc, CMP_STRIDE * LANES), _F32)] * 2 + [pltpu.VMEM((2, page, LANES), _F32)])


def _nsa_sample_kernel(pt_ref, q_ref, g_ref, ck_ref, cv_ref, new_ref, neww_ref, win_ref, cover_ref, *refs,
                       n_pages_step, n_new, past_len, n_top):
    del pt_ref
    pages = refs[:n_pages_step]
    o_ref, m_ref, acc_ref, bias_ref, ocmp_ref, owin_ref = refs[n_pages_step:]
    s = pl.program_id(1)
    tq = SAMPLE_ROWS
    rows = NSA_HEADS * tq
    qs = _stack_nsa_queries(q_ref[0])
    qi_r = lax.broadcasted_iota(jnp.int32, (rows, 1), 0) % tq
    qpos_r = past_len + qi_r

    @pl.when(s == 0)
    def _():
        qf = qs.astype(_F32)
        o_cmp, psums = _cmp_branch(qs, ck_ref[0], cv_ref[0], qpos_r, tq)
        ocmp_ref[...] = o_cmp
        qp = past_len + lax.broadcasted_iota(jnp.int32, (tq, 1), 0)
        biases = [_nsa_select(_dot_split(psum, cover_ref[...]), qp, past_len // SEL_BLOCK, n_top - 1, 1)
                  for psum in psums]
        bias_ref[...] = jnp.concatenate([b for b in biases for _ in range(NSA_GROUP)], axis=0)
        visible = lambda j: j <= qi_r
        new = new_ref[0].astype(_F32)
        state = _seed_state(*_new_key_softmax(qf, new[:, 2 * LANES:3 * LANES], new[:, 3 * LANES:], n_new, visible))
        _state_refs_store(m_ref, acc_ref, state)
        wnew = neww_ref[0].astype(_F32)
        wstate = _seed_state(*_new_key_softmax(qf, wnew[:, :LANES], wnew[:, LANES:], n_new, visible))
        w = win_ref[0, 0]
        n_win = w.shape[1]
        dist = qpos_r - (past_len - n_win + lax.broadcasted_iota(jnp.int32, (1, n_win), 1))
        z = jnp.where((dist >= 0) & (dist <= WINDOW), _dot(qs, w[:LANES].astype(MXU_DTYPE)), NEG)
        owin_ref[...] = _finish1(_online_update1(wstate, z, _with_ones(w[LANES:].astype(MXU_DTYPE), 0), nt=True))

    n_keys = n_pages_step * pages[0].shape[3]
    lhs = jnp.concatenate([qs, bias_ref[...].astype(MXU_DTYPE)], axis=1)
    rhs = jnp.concatenate([_side_by_side(pages, 0, LANES), _key_block_onehot(n_keys, s * n_keys, SEL_BLOCK)], axis=0)
    state = (m_ref[:, 0:1], acc_ref[...])
    state = _online_update1(state, _dot(lhs, rhs), _with_ones(_side_by_side(pages, LANES, 2 * LANES), 0), nt=True)
    _state_refs_store(m_ref, acc_ref, state)

    @pl.when(s == pl.num_programs(1) - 1)
    def _():
        o_sel = _finish1((m_ref[:, 0:1], acc_ref[...]))
        o_ref[0] = _combine_nsa(g_ref[0], ocmp_ref[...], o_sel, owin_ref[...], tq).astype(o_ref.dtype)


def _nsa_sample(q, gates, cmp_k, cmp_v, new_bf, new_win_bf, cache, cache_win, layer, page_table, cover, n_new):
    b, n_pages = page_table.shape
    page = cache.shape[3]
    past_len = n_pages * page
    g = PAGES_PER_STEP
    rows = NSA_HEADS * SAMPLE_ROWS
    n_top = min(SEL_TOPN, past_len // SEL_BLOCK + 1)
    win_spec = pl.BlockSpec((1, 1) + cache_win.shape[2:], lambda i, s, pt: (layer, i, 0, 0))
    out_shape = (b, SAMPLE_ROWS, NSA_Q_COLS)
    vmem = lambda w: pltpu.VMEM((rows, w), _F32)
    return _sample_call(
        functools.partial(_nsa_sample_kernel, n_pages_step=g, n_new=n_new, past_len=past_len, n_top=n_top),
        "nsa_sample", page_table, n_pages // g,
        [q, gates, cmp_k, cmp_v, new_bf, new_win_bf, cache_win, cover],
        [_seq_block(q.shape), _seq_block(gates.shape), _seq_block(cmp_k.shape), _seq_block(cmp_v.shape),
         _seq_block(new_bf.shape), _seq_block(new_win_bf.shape), win_spec, _const_block(cover.shape)],
        [cache] * g, _page_specs(layer, g, (1, 1, 2 * LANES, page), 1, lambda s, gp: s * g + gp),
        _seq_block(out_shape), jax.ShapeDtypeStruct(out_shape, MXU_DTYPE),
        [vmem(LANES), vmem(2 * LANES), vmem(LANES), vmem(LANES), vmem(LANES)])


def _sb_sample_kernel(pt_ref, q_ref, new_ref, *refs, n_pages_step, n_new):
    del pt_ref
    pages = refs[:n_pages_step]
    o_ref, c_ref, acc_ref = refs[n_pages_step:]
    s = pl.program_id(1)
    width = SB_HEADS * HEAD_DIM
    qs = _stack_heads(q_ref[0], SB_HEADS)
    rows = qs.shape[0]
    qi_r = lax.broadcasted_iota(jnp.int32, (rows, 1), 0) % SAMPLE_ROWS

    @pl.when(s == 0)
    def _():
        qf = qs.astype(_F32)
        new = new_ref[0].astype(_F32)
        zs = _row_scores(qf, new[:, :width], n_new)
        c = jnp.zeros((rows, 1), _F32)
        o = jnp.zeros((rows, width), _F32)
        for j in reversed(range(n_new)):
            seen = j < qi_r
            lk = jnp.where(seen, -_softplus(zs[j]), 0.0)
            a = jnp.where(seen, jnp.exp(lk + zs[j] + c), 0.0)
            o = o + a * new[j:j + 1, width:]
            c = c + lk
        c_ref[...] = jnp.broadcast_to(c, c_ref.shape)
        acc_ref[...] = o

    page_keys = pages[0].shape[3]
    cum = _cumsum_matrix(page_keys)
    z = _dot(qs, _side_by_side(pages, 0, width))
    lk = -_softplus(z)
    c = c_ref[...]
    later = []
    for gp in range(n_pages_step):
        r = _dot_split(lk[:, gp * page_keys:(gp + 1) * page_keys], cum)
        later.append(r[:, :page_keys] + c)
        c = c + r[:, page_keys:]
    a = jnp.exp(lk + z + jnp.concatenate(later, axis=1))
    acc_ref[...] += _dot_nt(a.astype(MXU_DTYPE), _side_by_side(pages, width, 2 * width))
    c_ref[...] = c

    @pl.when(s == pl.num_programs(1) - 1)
    def _():
        o_ref[0] = _unstack_heads(acc_ref[...], SB_HEADS).astype(o_ref.dtype)


def _sb_sample(q, new_bf, cache, layer, page_table, n_new):
    b, n_pages = page_table.shape
    page = cache.shape[3]
    g = PAGES_PER_STEP
    rows = SB_HEADS * SAMPLE_ROWS
    width = SB_HEADS * HEAD_DIM
    return _sample_call(
        functools.partial(_sb_sample_kernel, n_pages_step=g, n_new=n_new),
        "sb_sample", page_table, n_pages // g,
        [q, new_bf], [_seq_block(q.shape), _seq_block(new_bf.shape)],
        [cache] * g, _page_specs(layer, g, (1, 1, 2 * width, page), 0, lambda s, gp: n_pages - 1 - (s * g + gp)),
        _seq_block(q.shape), jax.ShapeDtypeStruct(q.shape, MXU_DTYPE),
        [pltpu.VMEM((rows, LANES), _F32), pltpu.VMEM((rows, width), _F32)])


def _moba_sample_kernel(pt_ref, q_ref, new_ref, *refs, n_pages_step, n_new, past_len):
    del pt_ref
    pages = refs[:n_pages_step]
    o_ref, m_ref, part_ref, km_ref = refs[n_pages_step:]
    s = pl.program_id(1)
    width = MOBA_HEADS * HEAD_DIM
    page_keys = pages[0].shape[3]
    per_blk = MOBA_BLOCK // page_keys
    blk_step = n_pages_step // per_blk
    n_blk = past_len // MOBA_BLOCK
    qf = _stack_heads(q_ref[0], MOBA_HEADS)
    rows = qf.shape[0]
    qs = (qf * HEAD_DIM ** -0.5).astype(MXU_DTYPE)
    qi_r = lax.broadcasted_iota(jnp.int32, (rows, 1), 0) % SAMPLE_ROWS
    lane = lax.broadcasted_iota(jnp.int32, (1, LANES), 1)

    @pl.when(s == 0)
    def _():
        km_ref[...] = jnp.zeros_like(km_ref)

    km = km_ref[...]
    for jb in range(blk_step):
        blk = s * blk_step + jb
        mine = pages[jb * per_blk:(jb + 1) * per_blk]
        mean = sum(jnp.sum(p[0, 0, :width, :], axis=1, keepdims=True) for p in mine) * (1.0 / MOBA_BLOCK)
        km = km + jnp.where(lane == blk, mean, 0.0)
        z = _dot(qs, _side_by_side(mine, 0, width))
        m = jnp.max(z, axis=-1, keepdims=True)
        p = jnp.exp(z - m).astype(MXU_DTYPE)
        part_ref[blk] = _dot_nt(p, _with_ones(_side_by_side(mine, width, 2 * width), 0))
        m_ref[blk] = jnp.broadcast_to(m, (rows, LANES))
    km_ref[...] = km

    @pl.when(s == pl.num_programs(1) - 1)
    def _():
        gate = lax.dot_general(qf, km_ref[...], (((1,), (0,)), ((), ())), preferred_element_type=_F32,
                               precision=lax.Precision.HIGHEST)
        sel = _moba_pick(gate, n_blk, 1) + 1.0
        new = new_ref[0].astype(_F32)
        m0, l0, acc0 = _new_key_softmax(qs.astype(_F32), new[:, :width], new[:, width:], n_new, lambda j: j <= qi_r)
        m_all = jnp.full((rows, LANES), NEG, _F32)
        for j in range(n_blk):
            m_all = jnp.where(lane == j, m_ref[j], m_all)
        m_top = jnp.maximum(m0, jnp.max(jnp.where(sel > 0.0, m_all, NEG), axis=-1, keepdims=True))
        w = sel * jnp.exp(m_all - m_top)
        out = jnp.exp(m0 - m_top) * _seed_state(m0, l0, acc0)[1]
        for j in range(n_blk):
            out = out + w[:, j:j + 1] * part_ref[j]
        o_ref[0] = _unstack_heads(_finish1((m_top, out)), MOBA_HEADS).astype(o_ref.dtype)


def _moba_sample(q, new_bf, cache, layer, page_table, n_new):
    b, n_pages = page_table.shape
    page = cache.shape[3]
    g = PAGES_PER_STEP
    rows = MOBA_HEADS * SAMPLE_ROWS
    width = MOBA_HEADS * HEAD_DIM
    n_blk = n_pages * page // MOBA_BLOCK
    out_shape = (b, SAMPLE_ROWS, width)
    return _sample_call(
        functools.partial(_moba_sample_kernel, n_pages_step=g, n_new=n_new, past_len=n_pages * page),
        "moba_sample", page_table, n_pages // g,
        [q, new_bf], [_seq_block(q.shape), _seq_block(new_bf.shape)],
        [cache] * g, _page_specs(layer, g, (1, 1, 2 * width, page), 0, lambda s, gp: s * g + gp),
        _seq_block(out_shape), jax.ShapeDtypeStruct(out_shape, MXU_DTYPE),
        [pltpu.VMEM((n_blk, rows, LANES), _F32), pltpu.VMEM((n_blk, rows, width + LANES), _F32),
         pltpu.VMEM((width, LANES), _F32)])


def _rope_tables(pos):
    half = HEAD_DIM // 2
    inv = ROPE_THETA ** (-jnp.arange(half, dtype=_F32) / half)
    ang = pos.astype(_F32)[:, None] * inv[None, :]
    cos, sin = jnp.cos(ang), jnp.sin(ang)
    zero = jnp.zeros_like(sin)
    reps = LANES // HEAD_DIM
    return (jnp.tile(cos, (1, 2 * reps)), jnp.tile(jnp.concatenate([-sin, zero], axis=1), (1, reps)),
            jnp.tile(jnp.concatenate([zero, sin], axis=1), (1, reps)))


def _pack_w_in(w_in):
    d = w_in.shape[0]
    o1 = NSA_Q_COLS
    o2 = o1 + NSA_KV_COLS
    o3 = o2 + NSA_G_COLS
    o4 = o3 + SB_COLS
    q, kv, g, sb, mb = w_in[:, :o1], w_in[:, o1:o2], w_in[:, o2:o3], w_in[:, o3:o4], w_in[:, o4:]
    zero = jnp.zeros((d, HEAD_DIM), w_in.dtype)
    q_groups = []
    for hh in range(NSA_HEADS):
        w = q[:, hh * HEAD_DIM:(hh + 1) * HEAD_DIM]
        q_groups += [w, zero] if hh // NSA_GROUP == 0 else [zero, w]
    gate = jnp.pad(g, ((0, 0), (0, LANES - NSA_G_COLS)))
    return jnp.concatenate(q_groups + [kv, sb, mb, gate], axis=1).astype(MXU_DTYPE)


def _block_tables(t):
    key = jnp.arange(t)[:, None]
    j = jnp.arange(LANES)[None, :]
    eblk = jnp.where(key // SEL_BLOCK == j, -NEG, 0.0).astype(MXU_DTYPE)
    eblk_mb = jnp.where(key // MOBA_BLOCK == j, -NEG, 0.0).astype(MXU_DTYPE)
    i = jnp.arange(t // CMP_STRIDE)[:, None]
    ratio = SEL_BLOCK // CMP_STRIDE
    cover = ((i <= ratio * j + ratio - 1) & (i >= ratio * j - (CMP_LEN // CMP_STRIDE - 1))).astype(MXU_DTYPE)
    return eblk, eblk_mb, cover


def _layer_weights(l, norm_g, ffn_w_gate, ffn_w_up, ffn_w_down, w_in, nsa_cmp_pe, nsa_cmp_w1, nsa_cmp_w2,
                   w_branch_nsa, w_branch_sb, w_branch_moba, w_merge_gate, w_out, w_ple_proj, w_ple_gate):
    c = lambda w: w.astype(MXU_DTYPE)
    d, f = ffn_w_gate.shape[2:]
    tf = _ff_tile(f)
    cols = lambda w: c(w).reshape(d, f // tf, tf).transpose(1, 0, 2)
    return dict(
        g=[norm_g[l, i][None, :] for i in range(norm_g.shape[1])],
        ffn=[(cols(ffn_w_gate[l, i]), cols(ffn_w_up[l, i]), c(ffn_w_down[l, i]).reshape(f // tf, tf, d))
             for i in range(2)],
        w_all=_pack_w_in(w_in[l]),
        cw=_compress_weights(nsa_cmp_pe[l], nsa_cmp_w1[l], nsa_cmp_w2[l]),
        merge=(c(w_branch_nsa[l]), c(w_branch_sb[l]), c(w_branch_moba[l]), c(w_merge_gate[l]), c(w_out[l])),
        ple=(c(w_ple_gate[l]), c(w_ple_proj[l])),
    )


def _token_tile(n, cap):
    tm = cap
    while n % tm:
        tm //= 2
    return tm


def _ff_tile(f, cap=512):
    best = LANES
    for k in range(1, f // LANES + 1):
        if f % (k * LANES) == 0 and k * LANES <= cap:
            best = k * LANES
    return best


def _prompt_layer(h, p_l, lw, tabs, consts, b, t, depth, layer, carried):
    n, d = h.shape
    tm = _token_tile(t, 512)
    tm_ffn = _token_tile(t, 1024)
    tk = 8 * LANES
    g = lw["g"]
    h = _ffn(h, g[0], g[1], *lw["ffn"][0], tm=tm_ffn)
    per_seq = t // tm
    pr = _inproj(h, g[2], lw["w_all"], tabs, lambda i: (i % per_seq, 0), tm=tm, stack=(depth, layer, b, t),
                 carried=carried)
    stacked = tuple(pr.pop(k) for k in _INPROJ_STATE)
    kmean = pr.pop("kmean").reshape(b, per_seq, SUBLANES, -1)[:, :, :tm // MOBA_BLOCK].reshape(b, t // MOBA_BLOCK, -1)
    kmean = jnp.pad(kmean, ((0, 0), (0, LANES - kmean.shape[1]), (0, 0)))
    pr = {k: v.reshape(b, t, v.shape[1]) for k, v in pr.items()}
    eblk, eblk_mb, cover = consts
    cmp_k, cmp_v = _compress_prompt(pr["cmp_rows"], lw["cw"])
    o_nsa = _nsa_prompt(pr["q_nsa"], pr["gates"], cmp_k, cmp_v, pr["nsa_bf"], pr["win_bf"], eblk, cover,
                        tq=2 * LANES, tk=tk)
    o_sb = _sb_prompt(pr["q_sb"], pr["sb_bf"], tq=2 * LANES, tk=LANES)
    o_mb = _moba_prompt(pr["q_mb"], kmean, pr["mb_bf"], eblk_mb, tk=tk)
    flat = lambda o: o.reshape(n, o.shape[2])
    h = _merge(h, g[2], g[3], flat(o_nsa), flat(o_sb), flat(o_mb), *lw["merge"], tm=tm)
    h = _ffn(h, g[4], g[5], *lw["ffn"][1], ple=(p_l, *lw["ple"], g[6]), tm=tm_ffn)
    return h, stacked, pr["win"]


def _sample_layer(h, p_l, lw, tabs, cover, caches, layer, page_table, n_new):
    n, d = h.shape
    b = page_table.shape[0]
    tm = _token_tile(n, 512)
    g = lw["g"]
    cache_nsa, cache_sb, cache_mb, cache_win = caches
    h = _ffn(h, g[0], g[1], *lw["ffn"][0], tm=tm)
    pr = _inproj(h, g[2], lw["w_all"], tabs, lambda i: (i, 0), tm=tm)
    pr = {k: v.reshape(b, SAMPLE_ROWS, v.shape[1]) for k, v in pr.items()}
    cmp_k, cmp_v = _compress_sample(cache_nsa, layer, page_table, lw["cw"])
    o_nsa = _nsa_sample(pr["q_nsa"], pr["gates"], cmp_k, cmp_v, pr["nsa_bf"], pr["win_bf"], cache_nsa, cache_win,
                        layer, page_table, cover, n_new)
    o_sb = _sb_sample(pr["q_sb"], pr["sb_bf"], cache_sb, layer, page_table, n_new)
    o_mb = _moba_sample(pr["q_mb"], pr["mb_bf"], cache_mb, layer, page_table, n_new)
    flat = lambda o: o.reshape(n, o.shape[2])
    h = _merge(h, g[2], g[3], flat(o_nsa), flat(o_sb), flat(o_mb), *lw["merge"], tm=tm)
    h = _ffn(h, g[4], g[5], *lw["ffn"][1], ple=(p_l, *lw["ple"], g[6]), tm=tm)
    return h, tuple(pr[k][:, :n_new] for k in ("nsa", "sb", "mb", "win"))


def kernel(x_prompt, x_sample, p_prompt, p_sample, cache_nsa, cache_sb, cache_moba, cache_win, page_table,
           norm_g, ffn_w_gate, ffn_w_up, ffn_w_down, w_in, nsa_cmp_pe, nsa_cmp_w1, nsa_cmp_w2,
           w_branch_nsa, w_branch_sb, w_branch_moba, w_merge_gate, w_out, w_ple_proj, w_ple_gate):
    depth = norm_g.shape[0]
    b, t, d = x_prompt.shape
    bs, ts, _ = x_sample.shape
    n_pages = page_table.shape[1]
    page = cache_nsa.shape[2]
    past_len = n_pages * page
    assert ts <= SAMPLE_ROWS and t % (8 * LANES) == 0 and t >= WINDOW + LANES
    assert t // SEL_BLOCK <= LANES and past_len // SEL_BLOCK <= LANES
    assert n_pages % PAGES_PER_STEP == 0 and MOBA_BLOCK % page == 0 and cache_win.shape[2] == WINDOW
    weights = (norm_g, ffn_w_gate, ffn_w_up, ffn_w_down, w_in, nsa_cmp_pe, nsa_cmp_w1, nsa_cmp_w2,
               w_branch_nsa, w_branch_sb, w_branch_moba, w_merge_gate, w_out, w_ple_proj, w_ple_gate)
    tabs_p = _rope_tables(jnp.arange(t, dtype=jnp.int32))
    tabs_s = _rope_tables(past_len + jnp.arange(bs * SAMPLE_ROWS, dtype=jnp.int32) % SAMPLE_ROWS)
    consts = _block_tables(t)
    cover_s = _block_tables(past_len)[2]
    pad_rows = lambda x: jnp.pad(x, ((0, 0), (0, SAMPLE_ROWS - ts), (0, 0))).reshape(bs * SAMPLE_ROWS, -1)
    feature_major = lambda c: jnp.transpose(c, (0, 1, 3, 4, 5, 2)).reshape(c.shape[0], c.shape[1], -1, c.shape[2])
    caches = tuple(feature_major(c) for c in (cache_nsa, cache_sb, cache_moba, cache_win))
    h_p = x_prompt.reshape(b * t, d)
    h_s = pad_rows(x_sample)
    stacked_p, win_p, st_s = (), [], []
    keep = min(WINDOW, t)
    for l in range(depth):
        lw = _layer_weights(l, *weights)
        h_p, stacked_p, win = _prompt_layer(h_p, p_prompt[l].reshape(b * t, -1), lw, tabs_p, consts, b, t, depth, l,
                                            stacked_p)
        win_p.append(win[:, t - keep:])
        h_s, rows = _sample_layer(h_s, pad_rows(p_sample[l]), lw, tabs_s, cover_s, caches, l, page_table, ts)
        win_fm = jnp.concatenate([caches[3][l][:, :, ts:], jnp.swapaxes(rows[3], 1, 2)], axis=2)
        st_s.append(rows[:3] + (jnp.swapaxes(win_fm, 1, 2),))
    heads = ((4, NSA_KV_HEADS), (2, SB_HEADS), (2, MOBA_HEADS))
    out = [h_p.reshape(b, t, d), h_s.reshape(bs, SAMPLE_ROWS, d)[:, :ts]]
    for i, (parts, nh) in enumerate(heads):
        out.append(jnp.swapaxes(stacked_p[i], 2, 3).reshape(depth, b, t, parts, nh, HEAD_DIM))
        out.append(jnp.stack([s[i] for s in st_s]).reshape(depth, bs, ts, parts, nh, HEAD_DIM))
    out.append(jnp.stack(win_p).reshape(depth, b, keep, 2, NSA_KV_HEADS, HEAD_DIM))
    out.append(jnp.stack([s[3] for s in st_s]).reshape(depth, bs, -1, 2, NSA_KV_HEADS, HEAD_DIM))
    return tuple(out)
```
